```python
import math
import jax, jax.numpy as jnp
from jax import lax
import numpy as np

D_MODEL = 1024
BATCH = 2
SEQ = 8192
DEPTH = 2
DEC_BATCH = 32
DEC_SEQ = 1
PAST_LEN = 16384
PAGE_SIZE = 128

N_A_LAYERS = DEPTH // 2
N_B_LAYERS = DEPTH - N_A_LAYERS
N_DENSE = (DEPTH + 1) // 2
N_MOE = DEPTH // 2
SSM_CH = 16
SSM_GROUPS = D_MODEL // SSM_CH
SSM_STATE = 64
HEAD_DIM = 64
N_KV_HEADS = D_MODEL // 128
DILATION_GROUPS = ((128, 1), (512, 4), (2048, 16))
N_DIL_GROUPS = len(DILATION_GROUPS)
MAX_WINDOW = 2048
ROT_DIM = HEAD_DIM // 4
ROPE_THETA = 500000.0
D_FF = 256 * (-(-(8 * D_MODEL // 3) // 256))
N_EXPERTS = 8
TOP_K = 2
D_FF_EXPERT = 7 * D_MODEL // 2
RMS_EPS = 1e-6
NEG_INF = -1e30

kernel_name = "yoco_s5_dilated_window_hybrid_step"


def rms_norm(x, g):
    xf = x.astype(jnp.float32)
    y = xf * lax.rsqrt(jnp.mean(xf * xf, axis=-1, keepdims=True) + RMS_EPS)
    return (y * g.astype(jnp.float32)).astype(x.dtype)


def ada_modulation(c, w, b):
    m = (jax.nn.silu(c) @ w + b)[:, None, :]
    return jnp.split(m, 6, axis=-1)


def rotary(x, pos):
    half = ROT_DIM // 2
    inv = ROPE_THETA ** (-jnp.arange(half, dtype=jnp.float32) / half)
    ang = pos.astype(jnp.float32)[:, None] * inv[None, :]
    cos = jnp.cos(ang)[None, :, None, :]
    sin = jnp.sin(ang)[None, :, None, :]
    xr = x[..., :ROT_DIM].astype(jnp.float32)
    x1, x2 = xr[..., :half], xr[..., half:]
    rot = jnp.concatenate([x1 * cos - x2 * sin, x1 * sin + x2 * cos], axis=-1).astype(x.dtype)
    return jnp.concatenate([rot, x[..., ROT_DIM:]], axis=-1)


def _ssm_combine(e1, e2):
    a1r, a1i, b1r, b1i = e1
    a2r, a2i, b2r, b2i = e2
    return (a1r * a2r - a1i * a2i,
            a1r * a2i + a1i * a2r,
            a2r * b1r - a2i * b1i + b2r,
            a2r * b1i + a2i * b1r + b2i)


def ssm_scan(u, x0_re, x0_im, lam_re, lam_im, log_dt, b_re, b_im, c_re, c_im):
    f32 = jnp.float32
    lr, li = lam_re.astype(f32), lam_im.astype(f32)
    dt = jnp.exp(log_dt.astype(f32))[:, None]
    mag = jnp.exp(lr * dt)
    a_re, a_im = mag * jnp.cos(li * dt), mag * jnp.sin(li * dt)
    den = lr * lr + li * li
    f_re = ((a_re - 1.0) * lr + a_im * li) / den
    f_im = (a_im * lr - (a_re - 1.0) * li) / den
    br, bi = b_re.astype(f32), b_im.astype(f32)
    bb_re = f_re[..., None] * br - f_im[..., None] * bi
    bb_im = f_re[..., None] * bi + f_im[..., None] * br
    bu_re = jnp.einsum('blgc,gpc->blgp', u, bb_re)
    bu_im = jnp.einsum('blgc,gpc->blgp', u, bb_im)
    ar = jnp.broadcast_to(a_re, bu_re.shape)
    ai = jnp.broadcast_to(a_im, bu_im.shape)
    pr, pi, sr, si = lax.associative_scan(_ssm_combine, (ar, ai, bu_re, bu_im), axis=1)
    if x0_re is not None:
        x0r = x0_re.astype(f32)[:, None]
        x0i = x0_im.astype(f32)[:, None]
        sr = sr + pr * x0r - pi * x0i
        si = si + pr * x0i + pi * x0r
    y = (jnp.einsum('blgp,gcp->blgc', sr, c_re.astype(f32))
         - jnp.einsum('blgp,gcp->blgc', si, c_im.astype(f32)))
    return y, sr[:, -1], si[:, -1]


def s5_mixer(h, x0_re, x0_im, w_in, lam_re, lam_im, log_dt, b_re, b_im, c_re, c_im, d, w_glu, w_out):
    B, L, D = h.shape
    u = (h @ w_in).astype(jnp.float32)
    y, fr, fi = ssm_scan(u.reshape(B, L, SSM_GROUPS, SSM_CH), x0_re, x0_im,
                         lam_re, lam_im, log_dt, b_re, b_im, c_re, c_im)
    y = y.reshape(B, L, D) + d.astype(jnp.float32) * u
    z = jax.nn.gelu(y)
    gl = z * jax.nn.sigmoid(z @ w_glu)
    return (gl @ w_out).astype(h.dtype), fr, fi


def band_attention_prompt(q, k, v, dil, span):
    B, L, H, hd = q.shape
    M = L // dil
    nb = -(-M // span)
    Mp = nb * span
    pad5 = lambda t, lo, hi: jnp.pad(t, ((0, 0), (0, 0), (lo, hi), (0, 0), (0, 0)))
    res = lambda t: pad5(t.reshape(B, M, dil, H, hd).transpose(0, 2, 1, 3, 4), 0, Mp - M)
    qb = res(q).reshape(B, dil, nb, span, H, hd)

    def key_blocks(t):
        tp = pad5(t, span, 0)
        prev = tp[:, :, :Mp].reshape(B, dil, nb, span, H, hd)
        cur = tp[:, :, span:].reshape(B, dil, nb, span, H, hd)
        return jnp.concatenate([prev, cur], axis=3)

    kb, vb = key_blocks(res(k)), key_blocks(res(v))
    s = jnp.einsum('brnqhd,brnkhd->brnhqk', qb, kb).astype(jnp.float32)
    qi = jnp.arange(span)[:, None]
    kj = jnp.arange(2 * span)[None, :]
    dist = qi + span - kj
    key_idx = jnp.arange(nb)[:, None, None] * span - span + kj[None]
    mask = (dist >= 0)[None] & (dist <= span)[None] & (key_idx >= 0)
    s = jnp.where(mask[:, None], s, NEG_INF)
    m = jnp.max(s, axis=-1, keepdims=True)
    p = jnp.exp(s - m)
    den = jnp.sum(p, axis=-1, keepdims=True)
    o = jnp.einsum('brnhqk,brnkhd->brnhqd', p, vb.astype(jnp.float32)) / den
    lse = (m + jnp.log(den))[..., 0]
    o = o.transpose(0, 1, 2, 4, 3, 5).reshape(B, dil, Mp, H, hd)[:, :, :M]
    o = o.transpose(0, 2, 1, 3, 4).reshape(B, L, H, hd)
    lse = lse.transpose(0, 1, 2, 4, 3).reshape(B, dil, Mp, H)[:, :, :M]
    lse = lse.transpose(0, 2, 1, 3).reshape(B, L, H)
    return o, lse


def gather_attention_sample(q, k_ext, v_ext, dil, span):
    B, S, H, hd = q.shape
    CL = k_ext.shape[1] - S
    idx = (CL + jnp.arange(S))[:, None] - dil * jnp.arange(span + 1)[None, :]
    valid = idx >= 0
    idx = jnp.maximum(idx, 0)
    kg = k_ext[:, idx]
    vg = v_ext[:, idx]
    s = jnp.einsum('bshd,bsnhd->bshn', q, kg).astype(jnp.float32)
    s = jnp.where(valid[None, :, None, :], s, NEG_INF)
    m = jnp.max(s, axis=-1, keepdims=True)
    p = jnp.exp(s - m)
    den = jnp.sum(p, axis=-1, keepdims=True)
    o = jnp.einsum('bshn,bsnhd->bshd', p, vg.astype(jnp.float32)) / den
    return o, (m + jnp.log(den))[..., 0]


def dilated_attention(h, pos, k_att, v_att, w_q, w_o, prompt):
    B, L, _ = h.shape
    q = (h @ w_q).reshape(B, L, N_DIL_GROUPS * N_KV_HEADS, HEAD_DIM)
    q = (rotary(q, pos) * (HEAD_DIM ** -0.5)).reshape(B, L, N_DIL_GROUPS, N_KV_HEADS, HEAD_DIM)
    attend = band_attention_prompt if prompt else gather_attention_sample
    outs, lses = [], []
    for g, (win, dil) in enumerate(DILATION_GROUPS):
        o, lse = attend(q[:, :, g], k_att, v_att, dil, win // dil)
        outs.append(o)
        lses.append(lse)
    wts = jax.nn.softmax(jnp.stack(lses), axis=0)[..., None]
    o = jnp.sum(wts * jnp.stack(outs), axis=0)
    return o.reshape(B, L, N_KV_HEADS * HEAD_DIM).astype(h.dtype) @ w_o


def shared_kv(x, pos, g, w_kv):
    B, L, _ = x.shape
    kv = rms_norm(x, g) @ w_kv
    k, v = jnp.split(kv, 2, axis=-1)
    k = rotary(k.reshape(B, L, N_KV_HEADS, HEAD_DIM), pos)
    return k, v.reshape(B, L, N_KV_HEADS, HEAD_DIM)


def swiglu(h, w_gate, w_up, w_down):
    return (jax.nn.silu(h @ w_gate) * (h @ w_up)) @ w_down


def moe_swiglu(h, w_router, b_router, w_gate, w_up, w_down):
    B, L, D = h.shape
    t = h.reshape(B * L, D)
    logits = (t @ w_router + b_router).astype(jnp.float32)
    top_val, top_idx = lax.top_k(logits, TOP_K)
    gates = jax.nn.softmax(top_val, axis=-1)
    combine = jnp.einsum('nk,nke->ne', gates, jax.nn.one_hot(top_idx, N_EXPERTS, dtype=jnp.float32))
    y = jnp.zeros((B * L, D), jnp.float32)
    for e in range(N_EXPERTS):
        he = jax.nn.silu(t @ w_gate[e]) * (t @ w_up[e])
        y = y + combine[:, e:e + 1] * (he @ w_down[e]).astype(jnp.float32)
    return y.reshape(B, L, D).astype(h.dtype)


def setup_inputs(seed: int = 0) -> dict:
    key = jax.random.key(seed)
    ks = iter(jax.random.split(key, 48))
    f32 = jnp.float32
    D = D_MODEL

    def nrm(shape, scale=1.0):
        return jax.random.normal(next(ks), shape, f32) * scale

    cache_len = min(MAX_WINDOW, PAST_LEN)
    ga = (N_A_LAYERS, SSM_GROUPS, SSM_STATE)
    n_idx = jnp.arange(SSM_STATE, dtype=f32)
    return {
        'x_prompt': nrm((BATCH, SEQ, D)),
        'x_sample': nrm((DEC_BATCH, DEC_SEQ, D)),
        'state_ssm_re': nrm((N_A_LAYERS, DEC_BATCH, SSM_GROUPS, SSM_STATE), 0.1),
        'state_ssm_im': nrm((N_A_LAYERS, DEC_BATCH, SSM_GROUPS, SSM_STATE), 0.1),
        'cache_k': nrm((DEC_BATCH, cache_len, N_KV_HEADS, HEAD_DIM)),
        'cache_v': nrm((DEC_BATCH, cache_len, N_KV_HEADS, HEAD_DIM)),
        'c_prompt': nrm((BATCH, D)),
        'c_sample': nrm((DEC_BATCH, D)),
        'ada_w': nrm((DEPTH, D, 6 * D), D ** -0.5),
        'ada_b': nrm((DEPTH, 6 * D), 0.01),
        'norm_g': 1.0 + nrm((DEPTH, 4, D), 0.01),
        'ssm_w_in': nrm((N_A_LAYERS, D, D), D ** -0.5),
        'ssm_lam_re': -0.5 + nrm(ga, 0.01),
        'ssm_lam_im': math.pi * n_idx + nrm(ga, 0.01),
        'ssm_log_dt': jax.random.uniform(next(ks), (N_A_LAYERS, SSM_GROUPS), f32,
                                         math.log(1e-3), math.log(1e-1)),
        'ssm_b_re': nrm((N_A_LAYERS, SSM_GROUPS, SSM_STATE, SSM_CH), (2 * SSM_CH) ** -0.5),
        'ssm_b_im': nrm((N_A_LAYERS, SSM_GROUPS, SSM_STATE, SSM_CH), (2 * SSM_CH) ** -0.5),
        'ssm_c_re': nrm((N_A_LAYERS, SSM_GROUPS, SSM_CH, SSM_STATE), (2 * SSM_STATE) ** -0.5),
        'ssm_c_im': nrm((N_A_LAYERS, SSM_GROUPS, SSM_CH, SSM_STATE), (2 * SSM_STATE) ** -0.5),
        'ssm_d': nrm((N_A_LAYERS, D)),
        'ssm_w_glu': nrm((N_A_LAYERS, D, D), D ** -0.5),
        'ssm_w_out': nrm((N_A_LAYERS, D, D), D ** -0.5),
        'kv_norm_g': 1.0 + nrm((D,), 0.01),
        'w_kv': nrm((D, 2 * N_KV_HEADS * HEAD_DIM), D ** -0.5),
        'attn_w_q': nrm((N_B_LAYERS, D, N_DIL_GROUPS * N_KV_HEADS * HEAD_DIM), D ** -0.5),
        'attn_w_o': nrm((N_B_LAYERS, N_KV_HEADS * HEAD_DIM, D), (N_KV_HEADS * HEAD_DIM) ** -0.5),
        'ffn_w_gate': nrm((N_DENSE, D, D_FF), D ** -0.5),
        'ffn_w_up': nrm((N_DENSE, D, D_FF), D ** -0.5),
        'ffn_w_down': nrm((N_DENSE, D_FF, D), D_FF ** -0.5),
        'moe_w_router': nrm((N_MOE, D, N_EXPERTS), D ** -0.5),
        'moe_b_router': nrm((N_MOE, N_EXPERTS), 0.01),
        'moe_w_gate': nrm((N_MOE, N_EXPERTS, D, D_FF_EXPERT), D ** -0.5),
        'moe_w_up': nrm((N_MOE, N_EXPERTS, D, D_FF_EXPERT), D ** -0.5),
        'moe_w_down': nrm((N_MOE, N_EXPERTS, D_FF_EXPERT, D), D_FF_EXPERT ** -0.5),
    }


def reference(x_prompt, x_sample, state_ssm_re, state_ssm_im, cache_k, cache_v, c_prompt, c_sample,
              ada_w, ada_b, norm_g,
              ssm_w_in, ssm_lam_re, ssm_lam_im, ssm_log_dt, ssm_b_re, ssm_b_im, ssm_c_re, ssm_c_im,
              ssm_d, ssm_w_glu, ssm_w_out,
              kv_norm_g, w_kv, attn_w_q, attn_w_o,
              ffn_w_gate, ffn_w_up, ffn_w_down,
              moe_w_router, moe_b_router, moe_w_gate, moe_w_up, moe_w_down):

    def trunk(x, c, pos, ssm0_re, ssm0_im, k_past, v_past):
        prompt = k_past is None
        fin_re, fin_im = [], []
        k_att = v_att = k_rows = v_rows = None
        for l in range(DEPTH):
            sh1, sc1, g1, sh2, sc2, g2 = ada_modulation(c, ada_w[l], ada_b[l])
            if l == N_A_LAYERS:
                k_new, v_new = shared_kv(x, pos, kv_norm_g, w_kv)
                if prompt:
                    k_att, v_att = k_new, v_new
                    keep = min(MAX_WINDOW, x.shape[1])
                else:
                    k_att = jnp.concatenate([k_past, k_new.astype(k_past.dtype)], axis=1)
                    v_att = jnp.concatenate([v_past, v_new.astype(v_past.dtype)], axis=1)
                    keep = k_past.shape[1]
                k_rows, v_rows = k_att[:, -keep:], v_att[:, -keep:]
            h = rms_norm(x, norm_g[l, 0]) * (1.0 + sc1) + sh1
            if l < N_A_LAYERS:
                a = l
                out, fr, fi = s5_mixer(h, None if prompt else ssm0_re[a], None if prompt else ssm0_im[a],
                                       ssm_w_in[a], ssm_lam_re[a], ssm_lam_im[a], ssm_log_dt[a],
                                       ssm_b_re[a], ssm_b_im[a], ssm_c_re[a], ssm_c_im[a],
                                       ssm_d[a], ssm_w_glu[a], ssm_w_out[a])
                fin_re.append(fr)
                fin_im.append(fi)
            else:
                b = l - N_A_LAYERS
                out = dilated_attention(h, pos, k_att, v_att, attn_w_q[b], attn_w_o[b], prompt)
            x = x + (g1 * rms_norm(out, norm_g[l, 1])).astype(x.dtype)
            h = rms_norm(x, norm_g[l, 2]) * (1.0 + sc2) + sh2
            if l % 2 == 0:
                i = l // 2
                out = swiglu(h, ffn_w_gate[i], ffn_w_up[i], ffn_w_down[i])
            else:
                i = l // 2
                out = moe_swiglu(h, moe_w_router[i], moe_b_router[i], moe_w_gate[i], moe_w_up[i], moe_w_down[i])
            x = x + (g2 * rms_norm(out, norm_g[l, 3])).astype(x.dtype)
        return x, jnp.stack(fin_re), jnp.stack(fin_im), k_rows, v_rows

    pos_p = jnp.arange(x_prompt.shape[1], dtype=jnp.int32)
    pos_s = PAST_LEN + jnp.arange(x_sample.shape[1], dtype=jnp.int32)
    y_p, sr_p, si_p, k_p, v_p = trunk(x_prompt, c_prompt, pos_p, None, None, None, None)
    y_s, sr_s, si_s, k_s, v_s = trunk(x_sample, c_sample, pos_s, state_ssm_re, state_ssm_im, cache_k, cache_v)
    return (y_p, y_s, sr_p, si_p, sr_s, si_s, k_p, v_p, k_s, v_s)
```

```python
import functools
import math

import jax
import jax.numpy as jnp
from jax import lax
from jax.experimental import pallas as pl
from jax.experimental.pallas import tpu as pltpu

F32 = jnp.float32
BF16 = jnp.bfloat16
HIGHEST = lax.Precision.HIGHEST

D_MODEL = 1024
DEPTH = 2
PAST_LEN = 16384
SSM_CH = 16
SSM_GROUPS = D_MODEL // SSM_CH
SSM_STATE = 64
SSM_FLAT = SSM_GROUPS * SSM_STATE
HEAD_DIM = 64
N_KV_HEADS = D_MODEL // 128
KV_DIM = N_KV_HEADS * HEAD_DIM
DILATION_GROUPS = ((128, 1), (512, 4), (2048, 16))
N_DIL_GROUPS = len(DILATION_GROUPS)
SPAN = 128
MAX_WINDOW = 2048
ROT_DIM = HEAD_DIM // 4
ROPE_THETA = 500000.0
N_EXPERTS = 8
RMS_EPS = 1e-6
NEG_INF = -1e30

LANES = 128
SUBLANES = 8
SLABS = D_MODEL // LANES
GROUPS_PER_SLAB = LANES // SSM_CH
SLAB_STATES = GROUPS_PER_SLAB * SSM_STATE
STATE_TILES = SSM_FLAT // LANES
VMEM_LIMIT = 56 * 1024 * 1024


def _params(*sem):
    return pltpu.CompilerParams(dimension_semantics=sem, vmem_limit_bytes=VMEM_LIMIT)


def _mm(a, w, precise=True):
    if w.dtype == BF16 or not precise:
        return jnp.dot(a.astype(BF16), w.astype(BF16), preferred_element_type=F32)
    return jnp.dot(a.astype(F32), w, preferred_element_type=F32, precision=HIGHEST)


def _rms(x, g):
    return x * lax.rsqrt(jnp.mean(x * x, axis=-1, keepdims=True) + RMS_EPS) * g


def _sigmoid(x):
    return 1.0 / (1.0 + jnp.exp(-x))


def _silu(x):
    return x * _sigmoid(x)


def _gelu_tanh(x):
    return 0.5 * x * (1.0 + jnp.tanh(math.sqrt(2.0 / math.pi) * (x + 0.044715 * (x * x * x))))


def _rows(tm, n):
    return pl.BlockSpec((None, tm, n), lambda b, i, *_: (b, i, 0))


def _mod(arr, tm):
    if arr.shape[1] == 1:
        return pl.BlockSpec((None, 1, arr.shape[2]), lambda b, i, *_: (b, 0, 0))
    return pl.BlockSpec((None, tm, arr.shape[2]), lambda b, i, *_: (b, i, 0))


def _const(arr):
    nd = arr.ndim
    return pl.BlockSpec(arr.shape, lambda *_: (0,) * nd)


def _ada_kernel(c_ref, w_ref, b_ref, o_ref):
    o_ref[...] = _mm(_silu(c_ref[...]), w_ref[...]) + b_ref[...]


def _ada_call(c_all, ada_w, ada_b):
    rows = c_all.shape[0]
    b4 = ada_b.reshape(DEPTH, 6, 1, D_MODEL)
    return pl.pallas_call(
        _ada_kernel,
        grid=(DEPTH, 6),
        in_specs=[pl.BlockSpec((rows, D_MODEL), lambda l, k: (0, 0)),
                  pl.BlockSpec((None, D_MODEL, D_MODEL), lambda l, k: (l, 0, k)),
                  pl.BlockSpec((None, None, 1, D_MODEL), lambda l, k: (l, k, 0, 0))],
        out_specs=pl.BlockSpec((None, None, rows, D_MODEL), lambda l, k: (l, k, 0, 0)),
        out_shape=jax.ShapeDtypeStruct((DEPTH, 6, rows, D_MODEL), F32),
        compiler_params=_params("arbitrary", "arbitrary"),
        name="ada_modulation",
    )(c_all, ada_w, b4)


def _inproj_kernel(x_ref, g_ref, sc_ref, sh_ref, w_ref, u_ref):
    h = _rms(x_ref[...], g_ref[...]) * (1.0 + sc_ref[...]) + sh_ref[...]
    u_ref[...] = _mm(h, w_ref[...])


def _inproj_call(x, g, sc, sh, w, tm):
    bm, r, d = x.shape
    return pl.pallas_call(
        _inproj_kernel,
        grid=(bm, r // tm),
        in_specs=[_rows(tm, d), _const(g), _mod(sc, tm), _mod(sh, tm), _const(w)],
        out_specs=_rows(tm, w.shape[1]),
        out_shape=jax.ShapeDtypeStruct((bm, r, w.shape[1]), F32),
        compiler_params=_params("arbitrary", "arbitrary"),
        name="s5_in_proj",
    )(x, g, sc, sh, w)


def _ssm_prompt_kernel(u_ref, wbr_ref, wbi_ref, ar_ref, ai_ref, cr_ref, ci_ref, d_ref,
                       z_ref, fr_ref, fi_ref, br_scr, bi_scr, sr_scr, si_scr, *, t_chunk):
    tb_n = t_chunk // SUBLANES

    @pl.when(pl.program_id(1) == 0)
    def _():
        sr_scr[...] = jnp.zeros_like(sr_scr)
        si_scr[...] = jnp.zeros_like(si_scr)

    ub = u_ref[...].astype(BF16)
    for s in range(SLABS):
        us = ub[:, s * LANES:(s + 1) * LANES]
        bre = jnp.dot(us, wbr_ref[s], preferred_element_type=F32)
        bim = jnp.dot(us, wbi_ref[s], preferred_element_type=F32)
        for jj in range(SLAB_STATES // LANES):
            j = s * (SLAB_STATES // LANES) + jj
            br_scr[:, j * SUBLANES:(j + 1) * SUBLANES, :] = (
                bre[:, jj * LANES:(jj + 1) * LANES].reshape(tb_n, SUBLANES, LANES))
            bi_scr[:, j * SUBLANES:(j + 1) * SUBLANES, :] = (
                bim[:, jj * LANES:(jj + 1) * LANES].reshape(tb_n, SUBLANES, LANES))

    ar = ar_ref[...]
    ai = ai_ref[...]

    def body(tb, carry):
        sr, si = carry
        for r in range(SUBLANES):
            step = pl.ds(r, STATE_TILES, stride=SUBLANES)
            nsr = ar * sr - ai * si + br_scr[tb, step, :]
            nsi = ar * si + ai * sr + bi_scr[tb, step, :]
            sr, si = nsr, nsi
            br_scr[tb, step, :] = sr
            bi_scr[tb, step, :] = si
        return sr, si

    sr, si = lax.fori_loop(0, tb_n, body, (sr_scr[...], si_scr[...]))
    sr_scr[...] = sr
    si_scr[...] = si
    fr_ref[...] = sr
    fi_ref[...] = si

    for s in range(SLABS):
        y = None
        for jj in range(SLAB_STATES // LANES):
            j = s * (SLAB_STATES // LANES) + jj
            rows = slice(j * SUBLANES, (j + 1) * SUBLANES)
            cols = slice(jj * LANES, (jj + 1) * LANES)
            sre = br_scr[:, rows, :].reshape(t_chunk, LANES).astype(BF16)
            sim = bi_scr[:, rows, :].reshape(t_chunk, LANES).astype(BF16)
            part = (jnp.dot(sre, cr_ref[s, cols, :], preferred_element_type=F32)
                    - jnp.dot(sim, ci_ref[s, cols, :], preferred_element_type=F32))
            y = part if y is None else y + part
        sl = slice(s * LANES, (s + 1) * LANES)
        z_ref[:, sl] = _gelu_tanh(y + d_ref[:, sl] * u_ref[:, sl])


def _ssm_prompt_call(u, wbr, wbi, ar, ai, cr, ci, d, t_chunk):
    b, l, dm = u.shape
    kern = functools.partial(_ssm_prompt_kernel, t_chunk=t_chunk)
    state_spec = pl.BlockSpec((None, STATE_TILES, LANES), lambda bb, c: (bb, 0, 0))
    scr = pltpu.VMEM((t_chunk // SUBLANES, STATE_TILES * SUBLANES, LANES), F32)
    return pl.pallas_call(
        kern,
        grid=(b, l // t_chunk),
        in_specs=[_rows(t_chunk, dm), _const(wbr), _const(wbi), _const(ar), _const(ai),
                  _const(cr), _const(ci), _const(d)],
        out_specs=[_rows(t_chunk, dm), state_spec, state_spec],
        out_shape=[jax.ShapeDtypeStruct((b, l, dm), F32),
                   jax.ShapeDtypeStruct((b, STATE_TILES, LANES), F32),
                   jax.ShapeDtypeStruct((b, STATE_TILES, LANES), F32)],
        scratch_shapes=[scr, scr, pltpu.VMEM((STATE_TILES, LANES), F32),
                        pltpu.VMEM((STATE_TILES, LANES), F32)],
        compiler_params=_params("arbitrary", "arbitrary"),
        name="s5_scan_prompt",
    )(u, wbr, wbi, ar, ai, cr, ci, d)


def _ssm_sample_kernel(u_ref, x0r_ref, x0i_ref, wbr_ref, wbi_ref, ar_ref, ai_ref, cr_ref, ci_ref,
                       d_ref, z_ref, fr_ref, fi_ref):
    for s in range(SLABS):
        sl = slice(s * LANES, (s + 1) * LANES)
        st = slice(s * SLAB_STATES, (s + 1) * SLAB_STATES)
        us = u_ref[:, sl]
        a_r, a_i = ar_ref[:, st], ai_ref[:, st]
        x0r, x0i = x0r_ref[:, st], x0i_ref[:, st]
        sr = _mm(us, wbr_ref[s]) + a_r * x0r - a_i * x0i
        si = _mm(us, wbi_ref[s]) + a_r * x0i + a_i * x0r
        fr_ref[:, st] = sr
        fi_ref[:, st] = si
        y = _mm(sr, cr_ref[s]) - _mm(si, ci_ref[s])
        z_ref[:, sl] = _gelu_tanh(y + d_ref[:, sl] * us)


def _ssm_sample_call(u, x0r, x0i, wbr, wbi, ar, ai, cr, ci, d):
    n = u.shape[0]
    args = (u, x0r, x0i, wbr, wbi, ar, ai, cr, ci, d)
    return pl.pallas_call(
        _ssm_sample_kernel,
        grid=(1,),
        in_specs=[_const(a) for a in args],
        out_specs=[pl.BlockSpec((n, D_MODEL), lambda i: (0, 0)),
                   pl.BlockSpec((n, SSM_FLAT), lambda i: (0, 0)),
                   pl.BlockSpec((n, SSM_FLAT), lambda i: (0, 0))],
        out_shape=[jax.ShapeDtypeStruct((n, D_MODEL), F32),
                   jax.ShapeDtypeStruct((n, SSM_FLAT), F32),
                   jax.ShapeDtypeStruct((n, SSM_FLAT), F32)],
        compiler_params=_params("arbitrary"),
        name="s5_step_sample",
    )(*args)


def _s5out_kernel(z_ref, x_ref, wg_ref, wo_ref, gn_ref, gate_ref, o_ref):
    z = z_ref[...]
    gl = z * _sigmoid(_mm(z, wg_ref[...]))
    out = _mm(gl, wo_ref[...])
    o_ref[...] = x_ref[...] + gate_ref[...] * _rms(out, gn_ref[...])


def _s5out_call(z, x, wg, wo, gn, gate, tm):
    bm, r, d = x.shape
    return pl.pallas_call(
        _s5out_kernel,
        grid=(bm, r // tm),
        in_specs=[_rows(tm, d), _rows(tm, d), _const(wg), _const(wo), _const(gn), _mod(gate, tm)],
        out_specs=_rows(tm, d),
        out_shape=jax.ShapeDtypeStruct((bm, r, d), F32),
        compiler_params=_params("arbitrary", "arbitrary"),
        name="s5_glu_out_proj",
    )(z, x, wg, wo, gn, gate)


def _ffn_kernel(x_ref, g_ref, sc_ref, sh_ref, wg_ref, wu_ref, wd_ref, gn_ref, gate_ref, o_ref,
                h_scr, acc_scr):
    f = pl.program_id(2)

    @pl.when(f == 0)
    def _():
        h = _rms(x_ref[...], g_ref[...]) * (1.0 + sc_ref[...]) + sh_ref[...]
        h_scr[...] = h.astype(h_scr.dtype)
        acc_scr[...] = jnp.zeros_like(acc_scr)

    h = h_scr[...]
    he = _silu(_mm(h, wg_ref[...])) * _mm(h, wu_ref[...])
    acc_scr[...] += _mm(he, wd_ref[...])

    @pl.when(f == pl.num_programs(2) - 1)
    def _():
        o_ref[...] = x_ref[...] + gate_ref[...] * _rms(acc_scr[...], gn_ref[...])


def _ffn_call(x, g, sc, sh, wg, wu, wd, gn, gate, tm, tf):
    bm, r, d = x.shape
    ff = wg.shape[1]
    return pl.pallas_call(
        _ffn_kernel,
        grid=(bm, r // tm, ff // tf),
        in_specs=[_rows(tm, d), _const(g), _mod(sc, tm), _mod(sh, tm),
                  pl.BlockSpec((d, tf), lambda b, i, f: (0, f)),
                  pl.BlockSpec((d, tf), lambda b, i, f: (0, f)),
                  pl.BlockSpec((tf, d), lambda b, i, f: (f, 0)),
                  _const(gn), _mod(gate, tm)],
        out_specs=_rows(tm, d),
        out_shape=jax.ShapeDtypeStruct((bm, r, d), F32),
        scratch_shapes=[pltpu.VMEM((tm, d), wg.dtype), pltpu.VMEM((tm, d), F32)],
        compiler_params=_params("arbitrary", "arbitrary", "arbitrary"),
        name="dense_swiglu",
    )(x, g, sc, sh, wg, wu, wd, gn, gate)


def _rope(t, cos, sin_lo, sin_hi):
    half = ROT_DIM // 2
    outs = []
    for s in range(t.shape[1] // LANES):
        ts = t[:, s * LANES:(s + 1) * LANES]
        outs.append(ts * cos + pltpu.roll(ts, LANES - half, 1) * sin_lo + pltpu.roll(ts, half, 1) * sin_hi)
    return jnp.concatenate(outs, axis=1)


def _qkv_kernel(x_ref, gkv_ref, g_ref, sc_ref, sh_ref, wkv_ref, wq_ref, cos_ref, slo_ref, shi_ref,
                k_ref, v_ref, q_ref):
    x = x_ref[...]
    xn = x * lax.rsqrt(jnp.mean(x * x, axis=-1, keepdims=True) + RMS_EPS)
    kv = _mm(xn * gkv_ref[...], wkv_ref[...])
    q = _mm((xn * g_ref[...]) * (1.0 + sc_ref[...]) + sh_ref[...], wq_ref[...])
    cos, slo, shi = cos_ref[...], slo_ref[...], shi_ref[...]
    k_ref[...] = _rope(kv[:, :KV_DIM], cos, slo, shi)
    v_ref[...] = kv[:, KV_DIM:]
    q_ref[...] = _rope(q, cos, slo, shi) * (HEAD_DIM ** -0.5)


def _qkv_call(x, gkv, g, sc, sh, wkv, wq, cos, slo, shi, tm):
    bm, r, d = x.shape
    nq = wq.shape[1]
    tab = pl.BlockSpec((tm, LANES), lambda b, i: (i, 0))
    return pl.pallas_call(
        _qkv_kernel,
        grid=(bm, r // tm),
        in_specs=[_rows(tm, d), _const(gkv), _const(g), _mod(sc, tm), _mod(sh, tm),
                  _const(wkv), _const(wq), tab, tab, tab],
        out_specs=[_rows(tm, KV_DIM), _rows(tm, KV_DIM), _rows(tm, nq)],
        out_shape=[jax.ShapeDtypeStruct((bm, r, KV_DIM), F32),
                   jax.ShapeDtypeStruct((bm, r, KV_DIM), F32),
                   jax.ShapeDtypeStruct((bm, r, nq), F32)],
        compiler_params=_params("arbitrary", "arbitrary"),
        name="qkv_proj_rope",
    )(x, gkv, g, sc, sh, wkv, wq, cos, slo, shi)


def _band_kernel(q_ref, kp_ref, kc_ref, vp_ref, vc_ref, o_ref, l_ref, *, nb):
    qi = lax.broadcasted_iota(jnp.int32, (SPAN, 2 * SPAN), 0)
    kj = lax.broadcasted_iota(jnp.int32, (SPAN, 2 * SPAN), 1)
    band = (kj >= qi) & (kj <= qi + SPAN)
    band_first = band & (kj >= jnp.where(pl.program_id(2) == 0, SPAN, 0))
    lane = lax.broadcasted_iota(jnp.int32, (SPAN, LANES), 1)
    head0 = lane < HEAD_DIM
    for i in range(nb):
        cur = slice(i * SPAN, (i + 1) * SPAN)
        prev = slice((i - 1) * SPAN, i * SPAN)
        for hp in range(KV_DIM // LANES):
            cols = slice(hp * LANES, (hp + 1) * LANES)
            q2 = q_ref[cur, cols]
            if i == 0:
                k2 = jnp.concatenate([kp_ref[:, cols], kc_ref[cur, cols]], axis=0).astype(BF16)
                v2 = jnp.concatenate([vp_ref[:, cols], vc_ref[cur, cols]], axis=0).astype(BF16)
                mask = band_first
            else:
                k2 = jnp.concatenate([kc_ref[prev, cols], kc_ref[cur, cols]], axis=0).astype(BF16)
                v2 = jnp.concatenate([vc_ref[prev, cols], vc_ref[cur, cols]], axis=0).astype(BF16)
                mask = band
            o_pair = None
            l_pair = None
            for hh in range(2):
                sel = head0 if hh == 0 else jnp.logical_not(head0)
                qm = jnp.where(sel, q2, 0.0).astype(BF16)
                s = lax.dot_general(qm, k2, (((1,), (1,)), ((), ())), preferred_element_type=F32)
                s = jnp.where(mask, s, NEG_INF)
                m = jnp.max(s, axis=-1, keepdims=True)
                p = jnp.exp(s - m)
                den = jnp.sum(p, axis=-1, keepdims=True)
                o = jnp.dot(p.astype(BF16), v2, preferred_element_type=F32) / den
                lse = jnp.broadcast_to(m + jnp.log(den), (SPAN, LANES))
                o_pair = o if o_pair is None else jnp.where(head0, o_pair, o)
                l_pair = lse if l_pair is None else jnp.where(head0, l_pair, lse)
            o_ref[cur, cols] = o_pair
            l_ref[cur, cols] = l_pair


def _band_call(q, k, v, g, dil, nb):
    b, l, _ = k.shape
    m = l // dil
    qv = q.reshape(b, m, dil * N_DIL_GROUPS * KV_DIM)
    kv_ = k.reshape(b, m, dil * KV_DIM)
    vv = v.reshape(b, m, dil * KV_DIM)
    tq = nb * SPAN
    cur = pl.BlockSpec((None, tq, KV_DIM), lambda bb, r, n: (bb, n, r))
    prev = pl.BlockSpec((None, SPAN, KV_DIM), lambda bb, r, n: (bb, jnp.maximum(n * nb - 1, 0), r))
    qspec = pl.BlockSpec((None, tq, KV_DIM), lambda bb, r, n: (bb, n, r * N_DIL_GROUPS + g))
    o, lse = pl.pallas_call(
        functools.partial(_band_kernel, nb=nb),
        grid=(b, dil, m // tq),
        in_specs=[qspec, prev, cur, prev, cur],
        out_specs=[cur, cur],
        out_shape=[jax.ShapeDtypeStruct((b, m, dil * KV_DIM), F32),
                   jax.ShapeDtypeStruct((b, m, dil * KV_DIM), F32)],
        compiler_params=_params("arbitrary", "arbitrary", "arbitrary"),
        name=f"band_attention_dil{dil}",
    )(qv, kv_, kv_, vv, vv)
    return o.reshape(b, l, KV_DIM), lse.reshape(b, l, KV_DIM)


def _sample_attn_kernel(q_ref, kn_ref, vn_ref, k1_ref, k4_ref, k16_ref, v1_ref, v4_ref, v16_ref,
                        o_ref, *, bb):
    c = lax.broadcasted_iota(jnp.int32, (KV_DIM, LANES), 0)
    h = lax.broadcasted_iota(jnp.int32, (KV_DIM, LANES), 1)
    head_sum = jnp.where(c // HEAD_DIM == h, 1.0, 0.0).astype(F32)
    h2 = lax.broadcasted_iota(jnp.int32, (LANES, KV_DIM), 0)
    c2 = lax.broadcasted_iota(jnp.int32, (LANES, KV_DIM), 1)
    head_spread = jnp.where(c2 // HEAD_DIM == h2, 1.0, 0.0).astype(F32)
    row = lax.broadcasted_iota(jnp.int32, (SPAN + SUBLANES, LANES), 0)
    valid = row <= SPAN
    for b in range(bb):
        kn = jnp.broadcast_to(kn_ref[b:b + 1, :], (SUBLANES, KV_DIM))
        vn = jnp.broadcast_to(vn_ref[b:b + 1, :], (SUBLANES, KV_DIM))
        outs, lses = [], []
        for g, (kc_ref, vc_ref) in enumerate(((k1_ref, v1_ref), (k4_ref, v4_ref), (k16_ref, v16_ref))):
            qg = q_ref[b:b + 1, g * KV_DIM:(g + 1) * KV_DIM]
            keys = jnp.concatenate([kc_ref[b], kn], axis=0)
            vals = jnp.concatenate([vc_ref[b], vn], axis=0)
            s = _mm(keys * qg, head_sum)
            s = jnp.where(valid, s, NEG_INF)
            m = jnp.max(s, axis=0, keepdims=True)
            p = jnp.where(valid, jnp.exp(s - m), 0.0)
            den = jnp.sum(p, axis=0, keepdims=True)
            pw = _mm(p / den, head_spread)
            outs.append(jnp.sum(pw * vals, axis=0, keepdims=True))
            lse8 = jnp.broadcast_to(m + jnp.log(den), (SUBLANES, LANES))
            lses.append(_mm(lse8, head_spread)[0:1, :])
        mx = jnp.maximum(jnp.maximum(lses[0], lses[1]), lses[2])
        es = [jnp.exp(l - mx) for l in lses]
        tot = es[0] + es[1] + es[2]
        o_ref[b:b + 1, :] = (es[0] * outs[0] + es[1] * outs[1] + es[2] * outs[2]) / tot


def _sample_attn_call(q, k_new, v_new, cache_k, cache_v, bb):
    n = q.shape[0]
    cl = cache_k.shape[1]
    views = []
    for cache in (cache_k, cache_v):
        for _, dil in DILATION_GROUPS:
            views.append(cache.reshape(n, cl // dil, dil * KV_DIM))

    def cspec(dil):
        last = cl // dil // SPAN - 1
        return pl.BlockSpec((bb, SPAN, KV_DIM), lambda i: (i, last, 0))

    cspecs = [cspec(dil) for _, dil in DILATION_GROUPS] * 2
    return pl.pallas_call(
        functools.partial(_sample_attn_kernel, bb=bb),
        grid=(n // bb,),
        in_specs=[pl.BlockSpec((bb, q.shape[1]), lambda i: (i, 0)),
                  pl.BlockSpec((bb, KV_DIM), lambda i: (i, 0)),
                  pl.BlockSpec((bb, KV_DIM), lambda i: (i, 0))] + cspecs,
        out_specs=pl.BlockSpec((bb, KV_DIM), lambda i: (i, 0)),
        out_shape=jax.ShapeDtypeStruct((n, KV_DIM), F32),
        compiler_params=_params("arbitrary"),
        name="window_attention_sample",
    )(q, k_new, v_new, *views)


def _attn_out_kernel(o1_ref, o2_ref, o3_ref, l1_ref, l2_ref, l3_ref, x_ref, wo_ref, gn_ref, gate_ref,
                     y_ref):
    l1, l2, l3 = l1_ref[...], l2_ref[...], l3_ref[...]
    mx = jnp.maximum(jnp.maximum(l1, l2), l3)
    e1, e2, e3 = jnp.exp(l1 - mx), jnp.exp(l2 - mx), jnp.exp(l3 - mx)
    o = (e1 * o1_ref[...] + e2 * o2_ref[...] + e3 * o3_ref[...]) / (e1 + e2 + e3)
    out = _mm(o, wo_ref[...])
    y_ref[...] = x_ref[...] + gate_ref[...] * _rms(out, gn_ref[...])


def _attn_out_call(os_, ls_, x, wo, gn, gate, tm):
    bm, r, d = x.shape
    kd = _rows(tm, KV_DIM)
    return pl.pallas_call(
        _attn_out_kernel,
        grid=(bm, r // tm),
        in_specs=[kd] * 6 + [_rows(tm, d), _const(wo), _const(gn), _mod(gate, tm)],
        out_specs=_rows(tm, d),
        out_shape=jax.ShapeDtypeStruct((bm, r, d), F32),
        compiler_params=_params("arbitrary", "arbitrary"),
        name="attn_merge_out_proj",
    )(*os_, *ls_, x, wo, gn, gate)


def _proj_res_kernel(o_ref, x_ref, wo_ref, gn_ref, gate_ref, y_ref):
    out = _mm(o_ref[...], wo_ref[...])
    y_ref[...] = x_ref[...] + gate_ref[...] * _rms(out, gn_ref[...])


def _proj_res_call(o, x, wo, gn, gate, tm):
    bm, r, d = x.shape
    return pl.pallas_call(
        _proj_res_kernel,
        grid=(bm, r // tm),
        in_specs=[_rows(tm, o.shape[2]), _rows(tm, d), _const(wo), _const(gn), _mod(gate, tm)],
        out_specs=_rows(tm, d),
        out_shape=jax.ShapeDtypeStruct((bm, r, d), F32),
        compiler_params=_params("arbitrary", "arbitrary"),
        name="attn_out_proj",
    )(o, x, wo, gn, gate)


def _router_kernel(x_ref, g_ref, sc_ref, sh_ref, wr_ref, br_ref, h_ref, comb_ref):
    h = _rms(x_ref[...], g_ref[...]) * (1.0 + sc_ref[...]) + sh_ref[...]
    h_ref[...] = h.astype(h_ref.dtype)
    logits = _mm(h, wr_ref[...]) + br_ref[...]
    lane = lax.broadcasted_iota(jnp.int32, logits.shape, 1).astype(F32)
    m1 = jnp.max(logits, axis=-1, keepdims=True)
    i1 = jnp.min(jnp.where(logits == m1, lane, float(LANES)), axis=-1, keepdims=True)
    rest = jnp.where(lane == i1, -jnp.inf, logits)
    m2 = jnp.max(rest, axis=-1, keepdims=True)
    i2 = jnp.min(jnp.where(rest == m2, lane, float(LANES)), axis=-1, keepdims=True)
    e = jnp.exp(m2 - m1)
    g1 = 1.0 / (1.0 + e)
    g2 = e / (1.0 + e)
    comb_ref[...] = jnp.where(lane == i1, g1, 0.0) + jnp.where(lane == i2, g2, 0.0)


def _router_call(x, g, sc, sh, wr, br, h_dtype, tm):
    bm, r, d = x.shape
    return pl.pallas_call(
        _router_kernel,
        grid=(bm, r // tm),
        in_specs=[_rows(tm, d), _const(g), _mod(sc, tm), _mod(sh, tm), _const(wr), _const(br)],
        out_specs=[_rows(tm, d), _rows(tm, LANES)],
        out_shape=[jax.ShapeDtypeStruct((bm, r, d), h_dtype),
                   jax.ShapeDtypeStruct((bm, r, LANES), F32)],
        compiler_params=_params("arbitrary", "arbitrary"),
        name="moe_router",
    )(x, g, sc, sh, wr, br)


def _moe_kernel(h_ref, comb_ref, x_ref, wg_ref, wu_ref, wd_ref, gn_ref, gate_ref, o_ref, acc_scr):
    e = pl.program_id(2)
    f = pl.program_id(3)

    @pl.when((e == 0) & (f == 0))
    def _():
        acc_scr[...] = jnp.zeros_like(acc_scr)

    comb = comb_ref[...]
    lane = lax.broadcasted_iota(jnp.int32, comb.shape, 1)
    cw = jnp.sum(jnp.where(lane == e, comb, 0.0), axis=-1, keepdims=True)
    h = h_ref[...]
    he = _silu(_mm(h, wg_ref[...], False)) * _mm(h, wu_ref[...], False)
    acc_scr[...] += cw * _mm(he, wd_ref[...], False)

    @pl.when((e == pl.num_programs(2) - 1) & (f == pl.num_programs(3) - 1))
    def _():
        o_ref[...] = x_ref[...] + gate_ref[...] * _rms(acc_scr[...], gn_ref[...])


def _moe_call(h, comb, x, wg, wu, wd, gn, gate, tm, tf):
    bm, r, d = x.shape
    ne, _, ff = wg.shape
    return pl.pallas_call(
        _moe_kernel,
        grid=(bm, r // tm, ne, ff // tf),
        in_specs=[_rows(tm, d), _rows(tm, LANES), _rows(tm, d),
                  pl.BlockSpec((None, d, tf), lambda b, i, e, f: (e, 0, f)),
                  pl.BlockSpec((None, d, tf), lambda b, i, e, f: (e, 0, f)),
                  pl.BlockSpec((None, tf, d), lambda b, i, e, f: (e, f, 0)),
                  _const(gn), _mod(gate, tm)],
        out_specs=_rows(tm, d),
        out_shape=jax.ShapeDtypeStruct((bm, r, d), F32),
        scratch_shapes=[pltpu.VMEM((tm, d), F32)],
        compiler_params=_params("arbitrary", "arbitrary", "arbitrary", "arbitrary"),
        name="moe_experts",
    )(h, comb, x, wg, wu, wd, gn, gate)


def _ssm_params(lam_re, lam_im, log_dt, b_re, b_im, c_re, c_im, wdtype):
    dt = jnp.exp(log_dt)[:, None]
    mag = jnp.exp(lam_re * dt)
    a_re, a_im = mag * jnp.cos(lam_im * dt), mag * jnp.sin(lam_im * dt)
    den = lam_re * lam_re + lam_im * lam_im
    f_re = ((a_re - 1.0) * lam_re + a_im * lam_im) / den
    f_im = (a_im * lam_re - (a_re - 1.0) * lam_im) / den
    bb_re = f_re[..., None] * b_re - f_im[..., None] * b_im
    bb_im = f_re[..., None] * b_im + f_im[..., None] * b_re
    eye = jnp.eye(GROUPS_PER_SLAB, dtype=F32)

    def in_map(bb):
        t = bb.reshape(SLABS, GROUPS_PER_SLAB, SSM_STATE, SSM_CH)
        w = jnp.einsum('sgpc,gh->sgchp', t, eye)
        return w.reshape(SLABS, LANES, SLAB_STATES).astype(wdtype)

    def out_map(cc):
        t = cc.reshape(SLABS, GROUPS_PER_SLAB, SSM_CH, SSM_STATE)
        w = jnp.einsum('sgcp,gh->sgphc', t, eye)
        return w.reshape(SLABS, SLAB_STATES, LANES).astype(wdtype)

    return a_re, a_im, in_map(bb_re), in_map(bb_im), out_map(c_re), out_map(c_im)


def _rope_tables(pos):
    half = ROT_DIM // 2
    inv = ROPE_THETA ** (-jnp.arange(half, dtype=F32) / half)
    ang = pos.astype(F32)[:, None] * inv[None, :]
    cos, sin = jnp.cos(ang), jnp.sin(ang)
    n = pos.shape[0]
    pad = jnp.zeros((n, HEAD_DIM - ROT_DIM), F32)
    zero = jnp.zeros((n, half), F32)
    cos_h = jnp.concatenate([cos, cos, pad + 1.0], axis=1)
    lo_h = jnp.concatenate([-sin, zero, pad], axis=1)
    hi_h = jnp.concatenate([zero, sin, pad], axis=1)
    rep = LANES // HEAD_DIM
    return jnp.tile(cos_h, (1, rep)), jnp.tile(lo_h, (1, rep)), jnp.tile(hi_h, (1, rep))


def _trunk(x, mods, pos, wts, prompt, ssm0=None, cache=None):
    bm, r, d = x.shape
    tm = 512 if prompt else r
    wd = BF16 if prompt else F32
    cast = lambda w: w.astype(wd)
    norm_g = wts['norm_g'].reshape(DEPTH, 4, 1, d)

    sh1, sc1, g1, sh2, sc2, g2 = mods[0]
    u = _inproj_call(x, norm_g[0, 0], sc1, sh1, cast(wts['ssm_w_in'][0]), tm)
    a_re, a_im, wbr, wbi, cr, ci = _ssm_params(
        wts['ssm_lam_re'][0], wts['ssm_lam_im'][0], wts['ssm_log_dt'][0], wts['ssm_b_re'][0],
        wts['ssm_b_im'][0], wts['ssm_c_re'][0], wts['ssm_c_im'][0], wd)
    dvec = wts['ssm_d'][0].reshape(1, d)
    if prompt:
        z, fr, fi = _ssm_prompt_call(u, wbr, wbi, a_re.reshape(STATE_TILES, LANES),
                                     a_im.reshape(STATE_TILES, LANES), cr, ci, dvec, 256)
    else:
        z, fr, fi = _ssm_sample_call(u[0], ssm0[0].reshape(r, SSM_FLAT), ssm0[1].reshape(r, SSM_FLAT),
                                     wbr, wbi, a_re.reshape(1, SSM_FLAT), a_im.reshape(1, SSM_FLAT),
                                     cr, ci, dvec)
        z = z[None]
    fr = fr.reshape(-1, SSM_GROUPS, SSM_STATE)[None]
    fi = fi.reshape(-1, SSM_GROUPS, SSM_STATE)[None]
    x = _s5out_call(z, x, cast(wts['ssm_w_glu'][0]), cast(wts['ssm_w_out'][0]), norm_g[0, 1], g1, tm)
    tf = wts['ffn_w_gate'].shape[2] // 2
    x = _ffn_call(x, norm_g[0, 2], sc2, sh2, cast(wts['ffn_w_gate'][0]), cast(wts['ffn_w_up'][0]),
                  cast(wts['ffn_w_down'][0]), norm_g[0, 3], g2, tm, tf)

    sh1, sc1, g1, sh2, sc2, g2 = mods[1]
    cos, slo, shi = _rope_tables(pos)
    k, v, q = _qkv_call(x, wts['kv_norm_g'].reshape(1, d), norm_g[1, 0], sc1, sh1, cast(wts['w_kv']),
                        cast(wts['attn_w_q'][0]), cos, slo, shi, tm)
    wo = cast(wts['attn_w_o'][0])
    if prompt:
        os_, ls_ = [], []
        for g, (_, dil) in enumerate(DILATION_GROUPS):
            o, lse = _band_call(q, k, v, g, dil, min(4, r // dil // SPAN))
            os_.append(o)
            ls_.append(lse)
        x = _attn_out_call(os_, ls_, x, wo, norm_g[1, 1], g1, tm)
    else:
        o = _sample_attn_call(q[0], k[0], v[0], cache[0].reshape(r, -1, KV_DIM),
                              cache[1].reshape(r, -1, KV_DIM), SUBLANES)
        x = _proj_res_call(o[None], x, wo, norm_g[1, 1], g1, tm)
    wr = cast(jnp.zeros((d, LANES), F32).at[:, :N_EXPERTS].set(wts['moe_w_router'][0]))
    br = jnp.full((1, LANES), NEG_INF, F32).at[0, :N_EXPERTS].set(wts['moe_b_router'][0])
    h, comb = _router_call(x, norm_g[1, 2], sc2, sh2, wr, br, wd, tm)
    tfe = wts['moe_w_gate'].shape[3] // (2 if prompt else 4)
    x = _moe_call(h, comb, x, cast(wts['moe_w_gate'][0]), cast(wts['moe_w_up'][0]),
                  cast(wts['moe_w_down'][0]), norm_g[1, 3], g2, tm, tfe)
    return x, fr, fi, k, v


def kernel(x_prompt, x_sample, state_ssm_re, state_ssm_im, cache_k, cache_v, c_prompt, c_sample, ada_w, ada_b, norm_g, ssm_w_in, ssm_lam_re, ssm_lam_im, ssm_log_dt, ssm_b_re, ssm_b_im, ssm_c_re, ssm_c_im, ssm_d, ssm_w_glu, ssm_w_out, kv_norm_g, w_kv, attn_w_q, attn_w_o, ffn_w_gate, ffn_w_up, ffn_w_down, moe_w_router, moe_b_router, moe_w_gate, moe_w_up, moe_w_down):
    wts = dict(norm_g=norm_g, ssm_w_in=ssm_w_in, ssm_lam_re=ssm_lam_re, ssm_lam_im=ssm_lam_im,
               ssm_log_dt=ssm_log_dt, ssm_b_re=ssm_b_re, ssm_b_im=ssm_b_im, ssm_c_re=ssm_c_re,
               ssm_c_im=ssm_c_im, ssm_d=ssm_d, ssm_w_glu=ssm_w_glu, ssm_w_out=ssm_w_out,
               kv_norm_g=kv_norm_g, w_kv=w_kv, attn_w_q=attn_w_q, attn_w_o=attn_w_o,
               ffn_w_gate=ffn_w_gate, ffn_w_up=ffn_w_up, ffn_w_down=ffn_w_down,
               moe_w_router=moe_w_router, moe_b_router=moe_b_router, moe_w_gate=moe_w_gate,
               moe_w_up=moe_w_up, moe_w_down=moe_w_down)
    nb, seq, d = x_prompt.shape
    ns = x_sample.shape[0]
    assert x_sample.shape[1] == 1 and ns % SUBLANES == 0

    pad = (-(ns + nb)) % SUBLANES
    c_all = jnp.concatenate([c_sample, c_prompt, jnp.zeros((pad, d), F32)], axis=0)
    mod_all = _ada_call(c_all, ada_w, ada_b)
    mods_s = [[mod_all[l, k, :ns][None] for k in range(6)] for l in range(DEPTH)]
    mods_p = [[mod_all[l, k, ns:ns + nb][:, None, :] for k in range(6)] for l in range(DEPTH)]

    pos_p = jnp.arange(seq, dtype=jnp.int32)
    pos_s = jnp.full((ns,), PAST_LEN, dtype=jnp.int32)
    y_p, sr_p, si_p, k_p, v_p = _trunk(x_prompt, mods_p, pos_p, wts, True)
    xs = x_sample.reshape(1, ns, d)
    y_s, sr_s, si_s, k_s, v_s = _trunk(xs, mods_s, pos_s, wts, False,
                                       ssm0=(state_ssm_re[0], state_ssm_im[0]), cache=(cache_k, cache_v))

    keep = min(MAX_WINDOW, seq)
    k_rows_p = k_p[:, seq - keep:].reshape(nb, keep, N_KV_HEADS, HEAD_DIM)
    v_rows_p = v_p[:, seq - keep:].reshape(nb, keep, N_KV_HEADS, HEAD_DIM)
    k_rows_s = jnp.concatenate([cache_k[:, 1:], k_s[0].reshape(ns, 1, N_KV_HEADS, HEAD_DIM)], axis=1)
    v_rows_s = jnp.concatenate([cache_v[:, 1:], v_s[0].reshape(ns, 1, N_KV_HEADS, HEAD_DIM)], axis=1)
    return (y_p, y_s.reshape(ns, 1, d), sr_p, si_p, sr_s, si_s, k_rows_p, v_rows_p, k_rows_s, v_rows_s)
```

```python
import functools
import math

import jax
import jax.numpy as jnp
from jax import lax
from jax.experimental import pallas as pl
from jax.experimental.pallas import tpu as pltpu

F32 = jnp.float32
BF16 = jnp.bfloat16
HIGHEST = lax.Precision.HIGHEST

D_MODEL = 1024
DEPTH = 2
PAST_LEN = 16384
SSM_CH = 16
SSM_GROUPS = D_MODEL // SSM_CH
SSM_STATE = 64
SSM_FLAT = SSM_GROUPS * SSM_STATE
HEAD_DIM = 64
N_KV_HEADS = D_MODEL // 128
KV_DIM = N_KV_HEADS * HEAD_DIM
DILATION_GROUPS = ((128, 1), (512, 4), (2048, 16))
N_DIL_GROUPS = len(DILATION_GROUPS)
SPAN = 128
MAX_WINDOW = 2048
ROT_DIM = HEAD_DIM // 4
ROPE_THETA = 500000.0
N_EXPERTS = 8
RMS_EPS = 1e-6
NEG_INF = -1e30

LANES = 128
SUBLANES = 8
SLABS = D_MODEL // LANES
GROUPS_PER_SLAB = LANES // SSM_CH
SLAB_STATES = GROUPS_PER_SLAB * SSM_STATE
STATE_TILES = SSM_FLAT // LANES
MOE_TILE_ROWS = 512
VMEM_LIMIT = 56 * 1024 * 1024


def _params(*sem):
    return pltpu.CompilerParams(dimension_semantics=sem, vmem_limit_bytes=VMEM_LIMIT)


def _mm(a, w, precise=True):
    if w.dtype == BF16 or not precise:
        return jnp.dot(a.astype(BF16), w.astype(BF16), preferred_element_type=F32)
    return jnp.dot(a.astype(F32), w, preferred_element_type=F32, precision=HIGHEST)


def _rms(x, g):
    return x * lax.rsqrt(jnp.mean(x * x, axis=-1, keepdims=True) + RMS_EPS) * g


def _sigmoid(x):
    return 1.0 / (1.0 + jnp.exp(-x))


def _silu(x):
    return x * _sigmoid(x)


def _gelu_tanh(x):
    return 0.5 * x * (1.0 + jnp.tanh(math.sqrt(2.0 / math.pi) * (x + 0.044715 * (x * x * x))))


def _rows(tm, n):
    return pl.BlockSpec((None, tm, n), lambda b, i, *_: (b, i, 0))


def _mod(arr, tm):
    if arr.shape[1] == 1:
        return pl.BlockSpec((None, 1, arr.shape[2]), lambda b, i, *_: (b, 0, 0))
    return pl.BlockSpec((None, tm, arr.shape[2]), lambda b, i, *_: (b, i, 0))


def _const(arr):
    nd = arr.ndim
    return pl.BlockSpec(arr.shape, lambda *_: (0,) * nd)


def _ada_kernel(c_ref, w_ref, b_ref, o_ref):
    o_ref[...] = _mm(_silu(c_ref[...]), w_ref[...]) + b_ref[...]


def _ada_call(c_all, ada_w, ada_b):
    rows = c_all.shape[0]
    b4 = ada_b.reshape(DEPTH, 6, 1, D_MODEL)
    return pl.pallas_call(
        _ada_kernel,
        grid=(DEPTH, 6),
        in_specs=[pl.BlockSpec((rows, D_MODEL), lambda l, k: (0, 0)),
                  pl.BlockSpec((None, D_MODEL, D_MODEL), lambda l, k: (l, 0, k)),
                  pl.BlockSpec((None, None, 1, D_MODEL), lambda l, k: (l, k, 0, 0))],
        out_specs=pl.BlockSpec((None, None, rows, D_MODEL), lambda l, k: (l, k, 0, 0)),
        out_shape=jax.ShapeDtypeStruct((DEPTH, 6, rows, D_MODEL), F32),
        compiler_params=_params("arbitrary", "arbitrary"),
        name="ada_modulation",
    )(c_all, ada_w, b4)


def _inproj_kernel(x_ref, g_ref, sc_ref, sh_ref, w_ref, u_ref):
    h = _rms(x_ref[...], g_ref[...]) * (1.0 + sc_ref[...]) + sh_ref[...]
    u_ref[...] = _mm(h, w_ref[...])


def _inproj_call(x, g, sc, sh, w, tm):
    bm, r, d = x.shape
    return pl.pallas_call(
        _inproj_kernel,
        grid=(bm, r // tm),
        in_specs=[_rows(tm, d), _const(g), _mod(sc, tm), _mod(sh, tm), _const(w)],
        out_specs=_rows(tm, w.shape[1]),
        out_shape=jax.ShapeDtypeStruct((bm, r, w.shape[1]), F32),
        compiler_params=_params("arbitrary", "arbitrary"),
        name="s5_in_proj",
    )(x, g, sc, sh, w)


def _ssm_prompt_kernel(u_ref, wbr_ref, wbi_ref, ar_ref, ai_ref, cr_ref, ci_ref, d_ref,
                       z_ref, fr_ref, fi_ref, br_scr, bi_scr, sr_scr, si_scr, *, t_chunk):
    tb_n = t_chunk // SUBLANES

    @pl.when(pl.program_id(1) == 0)
    def _():
        sr_scr[...] = jnp.zeros_like(sr_scr)
        si_scr[...] = jnp.zeros_like(si_scr)

    ub = u_ref[...].astype(BF16)
    for s in range(SLABS):
        us = ub[:, s * LANES:(s + 1) * LANES]
        bre = jnp.dot(us, wbr_ref[s], preferred_element_type=F32)
        bim = jnp.dot(us, wbi_ref[s], preferred_element_type=F32)
        for jj in range(SLAB_STATES // LANES):
            j = s * (SLAB_STATES // LANES) + jj
            br_scr[:, j * SUBLANES:(j + 1) * SUBLANES, :] = (
                bre[:, jj * LANES:(jj + 1) * LANES].reshape(tb_n, SUBLANES, LANES))
            bi_scr[:, j * SUBLANES:(j + 1) * SUBLANES, :] = (
                bim[:, jj * LANES:(jj + 1) * LANES].reshape(tb_n, SUBLANES, LANES))

    ar = ar_ref[...]
    ai = ai_ref[...]

    def body(tb, carry):
        sr, si = carry
        for r in range(SUBLANES):
            step = pl.ds(r, STATE_TILES, stride=SUBLANES)
            nsr = ar * sr - ai * si + br_scr[tb, step, :]
            nsi = ar * si + ai * sr + bi_scr[tb, step, :]
            sr, si = nsr, nsi
            br_scr[tb, step, :] = sr
            bi_scr[tb, step, :] = si
        return sr, si

    sr, si = lax.fori_loop(0, tb_n, body, (sr_scr[...], si_scr[...]))
    sr_scr[...] = sr
    si_scr[...] = si
    fr_ref[...] = sr
    fi_ref[...] = si

    for s in range(SLABS):
        y = None
        for jj in range(SLAB_STATES // LANES):
            j = s * (SLAB_STATES // LANES) + jj
            rows = slice(j * SUBLANES, (j + 1) * SUBLANES)
            cols = slice(jj * LANES, (jj + 1) * LANES)
            sre = br_scr[:, rows, :].reshape(t_chunk, LANES).astype(BF16)
            sim = bi_scr[:, rows, :].reshape(t_chunk, LANES).astype(BF16)
            part = (jnp.dot(sre, cr_ref[s, cols, :], preferred_element_type=F32)
                    - jnp.dot(sim, ci_ref[s, cols, :], preferred_element_type=F32))
            y = part if y is None else y + part
        sl = slice(s * LANES, (s + 1) * LANES)
        z_ref[:, sl] = _gelu_tanh(y + d_ref[:, sl] * u_ref[:, sl])


def _ssm_prompt_call(u, wbr, wbi, ar, ai, cr, ci, d, t_chunk):
    b, l, dm = u.shape
    kern = functools.partial(_ssm_prompt_kernel, t_chunk=t_chunk)
    state_spec = pl.BlockSpec((None, STATE_TILES, LANES), lambda bb, c: (bb, 0, 0))
    scr = pltpu.VMEM((t_chunk // SUBLANES, STATE_TILES * SUBLANES, LANES), F32)
    return pl.pallas_call(
        kern,
        grid=(b, l // t_chunk),
        in_specs=[_rows(t_chunk, dm), _const(wbr), _const(wbi), _const(ar), _const(ai),
                  _const(cr), _const(ci), _const(d)],
        out_specs=[_rows(t_chunk, dm), state_spec, state_spec],
        out_shape=[jax.ShapeDtypeStruct((b, l, dm), F32),
                   jax.ShapeDtypeStruct((b, STATE_TILES, LANES), F32),
                   jax.ShapeDtypeStruct((b, STATE_TILES, LANES), F32)],
        scratch_shapes=[scr, scr, pltpu.VMEM((STATE_TILES, LANES), F32),
                        pltpu.VMEM((STATE_TILES, LANES), F32)],
        compiler_params=_params("arbitrary", "arbitrary"),
        name="s5_scan_prompt",
    )(u, wbr, wbi, ar, ai, cr, ci, d)


def _ssm_sample_kernel(u_ref, x0r_ref, x0i_ref, wbr_ref, wbi_ref, ar_ref, ai_ref, cr_ref, ci_ref,
                       d_ref, z_ref, fr_ref, fi_ref):
    for s in range(SLABS):
        sl = slice(s * LANES, (s + 1) * LANES)
        st = slice(s * SLAB_STATES, (s + 1) * SLAB_STATES)
        us = u_ref[:, sl]
        a_r, a_i = ar_ref[:, st], ai_ref[:, st]
        x0r, x0i = x0r_ref[:, st], x0i_ref[:, st]
        sr = _mm(us, wbr_ref[s]) + a_r * x0r - a_i * x0i
        si = _mm(us, wbi_ref[s]) + a_r * x0i + a_i * x0r
        fr_ref[:, st] = sr
        fi_ref[:, st] = si
        y = _mm(sr, cr_ref[s]) - _mm(si, ci_ref[s])
        z_ref[:, sl] = _gelu_tanh(y + d_ref[:, sl] * us)


def _ssm_sample_call(u, x0r, x0i, wbr, wbi, ar, ai, cr, ci, d):
    n = u.shape[0]
    args = (u, x0r, x0i, wbr, wbi, ar, ai, cr, ci, d)
    return pl.pallas_call(
        _ssm_sample_kernel,
        grid=(1,),
        in_specs=[_const(a) for a in args],
        out_specs=[pl.BlockSpec((n, D_MODEL), lambda i: (0, 0)),
                   pl.BlockSpec((n, SSM_FLAT), lambda i: (0, 0)),
                   pl.BlockSpec((n, SSM_FLAT), lambda i: (0, 0))],
        out_shape=[jax.ShapeDtypeStruct((n, D_MODEL), F32),
                   jax.ShapeDtypeStruct((n, SSM_FLAT), F32),
                   jax.ShapeDtypeStruct((n, SSM_FLAT), F32)],
        compiler_params=_params("arbitrary"),
        name="s5_step_sample",
    )(*args)


def _s5out_kernel(z_ref, x_ref, wg_ref, wo_ref, gn_ref, gate_ref, o_ref):
    z = z_ref[...]
    gl = z * _sigmoid(_mm(z, wg_ref[...]))
    out = _mm(gl, wo_ref[...])
    o_ref[...] = x_ref[...] + gate_ref[...] * _rms(out, gn_ref[...])


def _s5out_call(z, x, wg, wo, gn, gate, tm):
    bm, r, d = x.shape
    return pl.pallas_call(
        _s5out_kernel,
        grid=(bm, r // tm),
        in_specs=[_rows(tm, d), _rows(tm, d), _const(wg), _const(wo), _const(gn), _mod(gate, tm)],
        out_specs=_rows(tm, d),
        out_shape=jax.ShapeDtypeStruct((bm, r, d), F32),
        compiler_params=_params("arbitrary", "arbitrary"),
        name="s5_glu_out_proj",
    )(z, x, wg, wo, gn, gate)


def _ffn_kernel(x_ref, g_ref, sc_ref, sh_ref, wg_ref, wu_ref, wd_ref, gn_ref, gate_ref, o_ref,
                h_scr, acc_scr):
    f = pl.program_id(2)

    @pl.when(f == 0)
    def _():
        h = _rms(x_ref[...], g_ref[...]) * (1.0 + sc_ref[...]) + sh_ref[...]
        h_scr[...] = h.astype(h_scr.dtype)
        acc_scr[...] = jnp.zeros_like(acc_scr)

    h = h_scr[...]
    he = _silu(_mm(h, wg_ref[...])) * _mm(h, wu_ref[...])
    acc_scr[...] += _mm(he, wd_ref[...])

    @pl.when(f == pl.num_programs(2) - 1)
    def _():
        o_ref[...] = x_ref[...] + gate_ref[...] * _rms(acc_scr[...], gn_ref[...])


def _ffn_call(x, g, sc, sh, wg, wu, wd, gn, gate, tm, tf):
    bm, r, d = x.shape
    ff = wg.shape[1]
    return pl.pallas_call(
        _ffn_kernel,
        grid=(bm, r // tm, ff // tf),
        in_specs=[_rows(tm, d), _const(g), _mod(sc, tm), _mod(sh, tm),
                  pl.BlockSpec((d, tf), lambda b, i, f: (0, f)),
                  pl.BlockSpec((d, tf), lambda b, i, f: (0, f)),
                  pl.BlockSpec((tf, d), lambda b, i, f: (f, 0)),
                  _const(gn), _mod(gate, tm)],
        out_specs=_rows(tm, d),
        out_shape=jax.ShapeDtypeStruct((bm, r, d), F32),
        scratch_shapes=[pltpu.VMEM((tm, d), wg.dtype), pltpu.VMEM((tm, d), F32)],
        compiler_params=_params("arbitrary", "arbitrary", "arbitrary"),
        name="dense_swiglu",
    )(x, g, sc, sh, wg, wu, wd, gn, gate)


def _rope(t, cos, sin_lo, sin_hi):
    half = ROT_DIM // 2
    outs = []
    for s in range(t.shape[1] // LANES):
        ts = t[:, s * LANES:(s + 1) * LANES]
        outs.append(ts * cos + pltpu.roll(ts, LANES - half, 1) * sin_lo + pltpu.roll(ts, half, 1) * sin_hi)
    return jnp.concatenate(outs, axis=1)


def _qkv_values(x_ref, gkv_ref, g_ref, sc_ref, sh_ref, wkv_ref, wq_ref, cos_ref, slo_ref, shi_ref):
    x = x_ref[...]
    xn = x * lax.rsqrt(jnp.mean(x * x, axis=-1, keepdims=True) + RMS_EPS)
    kv = _mm(xn * gkv_ref[...], wkv_ref[...])
    q = _mm((xn * g_ref[...]) * (1.0 + sc_ref[...]) + sh_ref[...], wq_ref[...])
    cos, slo, shi = cos_ref[...], slo_ref[...], shi_ref[...]
    k = _rope(kv[:, :KV_DIM], cos, slo, shi)
    v = kv[:, KV_DIM:]
    q = _rope(q, cos, slo, shi) * (HEAD_DIM ** -0.5)
    return k, v, q


def _qkv_sample_kernel(*refs):
    k_ref, v_ref, q_ref = refs[10:]
    k, v, q = _qkv_values(*refs[:10])
    k_ref[...] = k
    v_ref[...] = v
    q_ref[...] = q


def _store_residues(dst_ref, val, scr, dil):
    if dil == 1:
        dst_ref[0] = val.astype(dst_ref.dtype)
        return
    slabs = val.shape[1] // LANES
    for s in range(slabs):
        scr[s] = val[:, s * LANES:(s + 1) * LANES]
    n = val.shape[0] // dil
    for r in range(dil):
        rows = [scr[s, pl.ds(r, n, stride=dil), :] for s in range(slabs)]
        dst_ref[r] = jnp.concatenate(rows, axis=1).astype(dst_ref.dtype)


def _qkv_prompt_kernel(*refs):
    k_ref, v_ref = refs[10:12]
    kd_refs = refs[12:12 + N_DIL_GROUPS]
    vd_refs = refs[12 + N_DIL_GROUPS:12 + 2 * N_DIL_GROUPS]
    qd_refs = refs[12 + 2 * N_DIL_GROUPS:12 + 3 * N_DIL_GROUPS]
    scr = refs[12 + 3 * N_DIL_GROUPS]
    k, v, q = _qkv_values(*refs[:10])
    k_ref[...] = k
    v_ref[...] = v
    for g, (_, dil) in enumerate(DILATION_GROUPS):
        _store_residues(kd_refs[g], k, scr, dil)
        _store_residues(vd_refs[g], v, scr, dil)
        _store_residues(qd_refs[g], q[:, g * KV_DIM:(g + 1) * KV_DIM], scr, dil)


def _qkv_call(x, gkv, g, sc, sh, wkv, wq, cos, slo, shi, tm, prompt):
    bm, r, d = x.shape
    nq = wq.shape[1]
    tab = pl.BlockSpec((tm, LANES), lambda b, i: (i, 0))
    in_specs = [_rows(tm, d), _const(gkv), _const(g), _mod(sc, tm), _mod(sh, tm),
                _const(wkv), _const(wq), tab, tab, tab]
    out_specs = [_rows(tm, KV_DIM), _rows(tm, KV_DIM)]
    out_shape = [jax.ShapeDtypeStruct((bm, r, KV_DIM), F32), jax.ShapeDtypeStruct((bm, r, KV_DIM), F32)]
    if not prompt:
        return pl.pallas_call(
            _qkv_sample_kernel,
            grid=(bm, r // tm),
            in_specs=in_specs,
            out_specs=out_specs + [_rows(tm, nq)],
            out_shape=out_shape + [jax.ShapeDtypeStruct((bm, r, nq), F32)],
            compiler_params=_params("arbitrary", "arbitrary"),
            name="qkv_proj_rope_sample",
        )(x, gkv, g, sc, sh, wkv, wq, cos, slo, shi)
    for _ in range(3):
        for _, dil in DILATION_GROUPS:
            out_specs.append(pl.BlockSpec((None, dil, tm // dil, KV_DIM), lambda b, i: (b, 0, i, 0)))
            out_shape.append(jax.ShapeDtypeStruct((bm, dil, r // dil, KV_DIM), BF16))
    outs = pl.pallas_call(
        _qkv_prompt_kernel,
        grid=(bm, r // tm),
        in_specs=in_specs,
        out_specs=out_specs,
        out_shape=out_shape,
        scratch_shapes=[pltpu.VMEM((KV_DIM // LANES, tm, LANES), F32)],
        compiler_params=_params("arbitrary", "arbitrary"),
        name="qkv_proj_rope_prompt",
    )(x, gkv, g, sc, sh, wkv, wq, cos, slo, shi)
    n = N_DIL_GROUPS
    return outs[0], outs[1], outs[2:2 + n], outs[2 + n:2 + 2 * n], outs[2 + 2 * n:2 + 3 * n]


def _band_kernel(q_ref, kp_ref, kc_ref, vp_ref, vc_ref, o_ref, l_ref, *, nb):
    qi = lax.broadcasted_iota(jnp.int32, (SPAN, 2 * SPAN), 0)
    kj = lax.broadcasted_iota(jnp.int32, (SPAN, 2 * SPAN), 1)
    band = (kj >= qi) & (kj <= qi + SPAN)
    band_first = band & (kj >= jnp.where(pl.program_id(2) == 0, SPAN, 0))
    lane = lax.broadcasted_iota(jnp.int32, (SPAN, LANES), 1)
    head0 = lane < HEAD_DIM
    for i in range(nb):
        cur = slice(i * SPAN, (i + 1) * SPAN)
        prev = slice((i - 1) * SPAN, i * SPAN)
        for hp in range(KV_DIM // LANES):
            cols = slice(hp * LANES, (hp + 1) * LANES)
            q2 = q_ref[cur, cols]
            if i == 0:
                k2 = jnp.concatenate([kp_ref[:, cols], kc_ref[cur, cols]], axis=0)
                v2 = jnp.concatenate([vp_ref[:, cols], vc_ref[cur, cols]], axis=0)
                mask = band_first
            else:
                k2 = jnp.concatenate([kc_ref[prev, cols], kc_ref[cur, cols]], axis=0)
                v2 = jnp.concatenate([vc_ref[prev, cols], vc_ref[cur, cols]], axis=0)
                mask = band
            o_pair = None
            l_pair = None
            for hh in range(2):
                sel = head0 if hh == 0 else jnp.logical_not(head0)
                qm = jnp.where(sel, q2, jnp.zeros_like(q2))
                s = lax.dot_general(qm, k2, (((1,), (1,)), ((), ())), preferred_element_type=F32)
                s = jnp.where(mask, s, NEG_INF)
                m = jnp.max(s, axis=-1, keepdims=True)
                p = jnp.exp(s - m)
                den = jnp.sum(p, axis=-1, keepdims=True)
                o = jnp.dot(p.astype(BF16), v2, preferred_element_type=F32) / den
                lse = jnp.broadcast_to(m + jnp.log(den), (SPAN, LANES))
                o_pair = o if o_pair is None else jnp.where(head0, o_pair, o)
                l_pair = lse if l_pair is None else jnp.where(head0, l_pair, lse)
            o_ref[cur, cols] = o_pair
            l_ref[cur, cols] = l_pair


def _band_call(q, k, v, nb):
    b, dil, m, _ = k.shape
    tq = nb * SPAN
    cur = pl.BlockSpec((None, None, tq, KV_DIM), lambda bb, r, n: (bb, r, n, 0))
    prev = pl.BlockSpec((None, None, SPAN, KV_DIM), lambda bb, r, n: (bb, r, jnp.maximum(n * nb - 1, 0), 0))
    return pl.pallas_call(
        functools.partial(_band_kernel, nb=nb),
        grid=(b, dil, m // tq),
        in_specs=[cur, prev, cur, prev, cur],
        out_specs=[cur, cur],
        out_shape=[jax.ShapeDtypeStruct((b, dil, m, KV_DIM), F32),
                   jax.ShapeDtypeStruct((b, dil, m, KV_DIM), F32)],
        compiler_params=_params("arbitrary", "arbitrary", "arbitrary"),
        name=f"band_attention_dil{dil}",
    )(q, k, k, v, v)


def _sample_attn_kernel(q_ref, kn_ref, vn_ref, k1_ref, k4_ref, k16_ref, v1_ref, v4_ref, v16_ref,
                        o_ref, *, bb):
    for b in range(bb):
        kn, vn = kn_ref[b], vn_ref[b]
        outs, lses = [], []
        for g, (kc_ref, vc_ref) in enumerate(((k1_ref, v1_ref), (k4_ref, v4_ref), (k16_ref, v16_ref))):
            qg = q_ref[b, g]
            s = jnp.sum(kc_ref[b] * qg[None], axis=-1, keepdims=True)
            s_new = jnp.sum(kn * qg, axis=-1, keepdims=True)
            m = jnp.maximum(jnp.max(s, axis=0), s_new)
            p = jnp.exp(s - m[None])
            p_new = jnp.exp(s_new - m)
            den = jnp.sum(p, axis=0) + p_new
            outs.append((jnp.sum(p * vc_ref[b], axis=0) + p_new * vn) / den)
            lses.append(m + jnp.log(den))
        mx = jnp.maximum(jnp.maximum(lses[0], lses[1]), lses[2])
        es = [jnp.exp(l - mx) for l in lses]
        o_ref[b] = (es[0] * outs[0] + es[1] * outs[1] + es[2] * outs[2]) / (es[0] + es[1] + es[2])


def _sample_attn_call(q, k_new, v_new, cache_k, cache_v, bb):
    n, cl, nh, hd = cache_k.shape
    views, cspecs = [], []
    for cache in (cache_k, cache_v):
        for _, dil in DILATION_GROUPS:
            views.append(cache.reshape(n, cl // dil, dil, nh, hd))
            last = cl // dil // SPAN - 1
            cspecs.append(pl.BlockSpec((bb, SPAN, None, nh, hd), lambda i, last=last: (i, last, 0, 0, 0)))
    new = pl.BlockSpec((bb, nh, hd), lambda i: (i, 0, 0))
    return pl.pallas_call(
        functools.partial(_sample_attn_kernel, bb=bb),
        grid=(n // bb,),
        in_specs=[pl.BlockSpec((bb, N_DIL_GROUPS, nh, hd), lambda i: (i, 0, 0, 0)), new, new] + cspecs,
        out_specs=new,
        out_shape=jax.ShapeDtypeStruct((n, nh, hd), F32),
        compiler_params=_params("arbitrary"),
        name="window_attention_sample",
    )(q, k_new, v_new, *views)


def _attn_out_kernel(o1_ref, o2_ref, o3_ref, l1_ref, l2_ref, l3_ref, x_ref, wo_ref, gn_ref, gate_ref,
                     y_ref, *scrs):
    vals = []
    for ref, scr, (_, dil) in zip((o1_ref, o2_ref, o3_ref, l1_ref, l2_ref, l3_ref), scrs,
                                  DILATION_GROUPS + DILATION_GROUPS):
        if dil == 1:
            vals.append(ref[0])
            continue
        n = ref.shape[1]
        slabs = ref.shape[2] // LANES
        for r in range(dil):
            for s in range(slabs):
                scr[s, pl.ds(r, n, stride=dil), :] = ref[r, :, s * LANES:(s + 1) * LANES]
        vals.append(jnp.concatenate([scr[s] for s in range(slabs)], axis=1))
    o1, o2, o3, l1, l2, l3 = vals
    mx = jnp.maximum(jnp.maximum(l1, l2), l3)
    e1, e2, e3 = jnp.exp(l1 - mx), jnp.exp(l2 - mx), jnp.exp(l3 - mx)
    o = (e1 * o1 + e2 * o2 + e3 * o3) / (e1 + e2 + e3)
    out = _mm(o, wo_ref[...])
    y_ref[...] = x_ref[...] + gate_ref[...] * _rms(out, gn_ref[...])


def _attn_out_call(os_, ls_, x, wo, gn, gate, tm):
    bm, r, d = x.shape
    res = [pl.BlockSpec((None, dil, tm // dil, KV_DIM), lambda b, i: (b, 0, i, 0))
           for _, dil in DILATION_GROUPS]
    return pl.pallas_call(
        _attn_out_kernel,
        grid=(bm, r // tm),
        in_specs=res + res + [_rows(tm, d), _const(wo), _const(gn), _mod(gate, tm)],
        out_specs=_rows(tm, d),
        out_shape=jax.ShapeDtypeStruct((bm, r, d), F32),
        scratch_shapes=[pltpu.VMEM((KV_DIM // LANES, tm, LANES), F32)] * (2 * N_DIL_GROUPS),
        compiler_params=_params("arbitrary", "arbitrary"),
        name="attn_merge_out_proj",
    )(*os_, *ls_, x, wo, gn, gate)


def _proj_res_kernel(o_ref, x_ref, wo_ref, gn_ref, gate_ref, y_ref):
    out = _mm(o_ref[...], wo_ref[...])
    y_ref[...] = x_ref[...] + gate_ref[...] * _rms(out, gn_ref[...])


def _proj_res_call(o, x, wo, gn, gate, tm):
    bm, r, d = x.shape
    return pl.pallas_call(
        _proj_res_kernel,
        grid=(bm, r // tm),
        in_specs=[_rows(tm, o.shape[2]), _rows(tm, d), _const(wo), _const(gn), _mod(gate, tm)],
        out_specs=_rows(tm, d),
        out_shape=jax.ShapeDtypeStruct((bm, r, d), F32),
        compiler_params=_params("arbitrary", "arbitrary"),
        name="attn_out_proj",
    )(o, x, wo, gn, gate)


def _top2(logits):
    lane = lax.broadcasted_iota(jnp.int32, logits.shape, 1).astype(F32)
    m1 = jnp.max(logits, axis=-1, keepdims=True)
    i1 = jnp.min(jnp.where(logits == m1, lane, float(LANES)), axis=-1, keepdims=True)
    rest = jnp.where(lane == i1, -jnp.inf, logits)
    m2 = jnp.max(rest, axis=-1, keepdims=True)
    i2 = jnp.min(jnp.where(rest == m2, lane, float(LANES)), axis=-1, keepdims=True)
    e = jnp.exp(m2 - m1)
    return lane, i1, i2, 1.0 / (1.0 + e), e / (1.0 + e)


def _route_kernel(x_ref, g_ref, sc_ref, sh_ref, wr_ref, br_ref, h_ref, route_ref, cnt_ref, carry_scr,
                  *, stride):
    @pl.when((pl.program_id(0) == 0) & (pl.program_id(1) == 0))
    def _():
        carry_scr[...] = jnp.zeros_like(carry_scr)

    h = _rms(x_ref[...], g_ref[...]) * (1.0 + sc_ref[...]) + sh_ref[...]
    tm = h.shape[0]
    for c in range(SLABS):
        h_ref[pl.ds(c, tm, stride=SUBLANES), :] = h[:, c * LANES:(c + 1) * LANES]
    logits = _mm(h, wr_ref[...]) + br_ref[...]
    lane, i1, i2, g1, g2 = _top2(logits)
    onehot = jnp.where(lane == i1, 1.0, 0.0) + jnp.where(lane == i2, 1.0, 0.0)
    row = lax.broadcasted_iota(jnp.int32, (tm, tm), 0)
    col = lax.broadcasted_iota(jnp.int32, (tm, tm), 1)
    earlier = jnp.where(col < row, 1.0, 0.0).astype(BF16)
    before = jnp.dot(earlier, onehot.astype(BF16), preferred_element_type=F32) + carry_scr[0:1, :]
    r1 = jnp.sum(jnp.where(lane == i1, before, 0.0), axis=-1, keepdims=True)
    r2 = jnp.sum(jnp.where(lane == i2, before, 0.0), axis=-1, keepdims=True)
    p1 = i1 * float(stride) + r1
    p2 = i2 * float(stride) + r2
    route_ref[...] = (jnp.where(lane == 0.0, p1, 0.0) + jnp.where(lane == 1.0, p2, 0.0)
                      + jnp.where(lane == 2.0, g1, 0.0) + jnp.where(lane == 3.0, g2, 0.0))
    carry = carry_scr[...] + jnp.sum(onehot, axis=0, keepdims=True)
    carry_scr[...] = carry
    cnt_ref[...] = carry


def _route_call(x, g, sc, sh, wr, br, tm, stride):
    bm, r, d = x.shape
    nt = r // tm
    return pl.pallas_call(
        functools.partial(_route_kernel, stride=stride),
        grid=(bm, nt),
        in_specs=[_rows(tm, d), _const(g), _mod(sc, tm), _mod(sh, tm), _const(wr), _const(br)],
        out_specs=[pl.BlockSpec((tm * SUBLANES, LANES), lambda b, i: (b * nt + i, 0)),
                   _rows(tm, LANES),
                   pl.BlockSpec((SUBLANES, LANES), lambda b, i: (0, 0))],
        out_shape=[jax.ShapeDtypeStruct((bm * r * SUBLANES, LANES), F32),
                   jax.ShapeDtypeStruct((bm, r, LANES), F32),
                   jax.ShapeDtypeStruct((SUBLANES, LANES), F32)],
        scratch_shapes=[pltpu.VMEM((SUBLANES, LANES), F32)],
        compiler_params=_params("arbitrary", "arbitrary"),
        name="moe_route",
    )(x, g, sc, sh, wr, br)


def _token_tile(ref, idx):
    return ref.at[pl.ds(pl.multiple_of(idx * SUBLANES, SUBLANES), SUBLANES), :]


def _dispatch_kernel(cnt_ref, pos_ref, h_hbm, xs_hbm, zero_scr, sem, zsem, *, tm, stride, tmx):
    i = pl.program_id(0)

    def issue(t, carry):
        src = _token_tile(h_hbm, i * tm + t)
        for k in range(2):
            pltpu.make_async_copy(src, _token_tile(xs_hbm, pos_ref[0, k * tm + t]), sem).start()
        return carry

    lax.fori_loop(0, tm, issue, 0)
    rows = 2 * tm * SUBLANES
    pltpu.make_async_copy(h_hbm.at[pl.ds(0, rows), :], xs_hbm.at[pl.ds(0, rows), :], sem).wait()

    @pl.when(i == pl.num_programs(0) - 1)
    def _():
        zero_scr[...] = jnp.zeros_like(zero_scr)
        copies = []
        for e in range(N_EXPERTS):
            start = pl.multiple_of((e * stride + cnt_ref[e]) * SUBLANES, SUBLANES)
            copies.append(pltpu.make_async_copy(zero_scr, xs_hbm.at[pl.ds(start, tmx * SUBLANES), :], zsem))
            copies[-1].start()
        for cp in copies:
            cp.wait()


def _dispatch_call(cnt, pos, h2d, tm, stride, tmx):
    n_tiles = pos.shape[0]
    rows = N_EXPERTS * stride * SUBLANES
    return pl.pallas_call(
        functools.partial(_dispatch_kernel, tm=tm, stride=stride, tmx=tmx),
        grid_spec=pltpu.PrefetchScalarGridSpec(
            num_scalar_prefetch=1,
            grid=(n_tiles,),
            in_specs=[pl.BlockSpec((None, 1, 2 * tm), lambda i, cnt: (i, 0, 0), memory_space=pltpu.SMEM),
                      pl.BlockSpec(memory_space=pl.ANY)],
            out_specs=pl.BlockSpec(memory_space=pl.ANY),
            scratch_shapes=[pltpu.VMEM((tmx * SUBLANES, LANES), F32),
                            pltpu.SemaphoreType.DMA(()), pltpu.SemaphoreType.DMA(())],
        ),
        out_shape=jax.ShapeDtypeStruct((rows, LANES), F32),
        compiler_params=_params("arbitrary"),
        name="moe_dispatch",
    )(cnt, pos, h2d)


def _expert_kernel(blk_ref, exp_ref, nused_ref, xs_ref, wg_ref, wu_ref, wd_ref, o_ref, x_scr, acc_scr,
                   *, tmx):
    j = pl.program_id(0)
    f = pl.program_id(1)

    @pl.when(j < nused_ref[0])
    def _():
        @pl.when(f == 0)
        def _():
            for c in range(SLABS):
                x_scr[:, c * LANES:(c + 1) * LANES] = (
                    xs_ref[pl.ds(c, tmx, stride=SUBLANES), :].astype(x_scr.dtype))
            acc_scr[...] = jnp.zeros_like(acc_scr)

        x = x_scr[...]
        he = _silu(_mm(x, wg_ref[...])) * _mm(x, wu_ref[...])
        acc_scr[...] += _mm(he, wd_ref[...])

        @pl.when(f == pl.num_programs(1) - 1)
        def _():
            for c in range(SLABS):
                o_ref[pl.ds(c, tmx, stride=SUBLANES), :] = acc_scr[:, c * LANES:(c + 1) * LANES]


def _expert_call(blk, exp, nused, xs, wg, wu, wd, tmx, tf):
    ne, d, ff = wg.shape
    nf = ff // tf

    def fidx(j, f, nused):
        return jnp.where(j < nused[0], f, nf - 1)

    row_spec = pl.BlockSpec((tmx * SUBLANES, LANES), lambda j, f, blk, exp, nu: (blk[j], 0))
    return pl.pallas_call(
        functools.partial(_expert_kernel, tmx=tmx),
        grid_spec=pltpu.PrefetchScalarGridSpec(
            num_scalar_prefetch=3,
            grid=(blk.shape[0], nf),
            in_specs=[row_spec,
                      pl.BlockSpec((None, d, tf), lambda j, f, blk, exp, nu: (exp[j], 0, fidx(j, f, nu))),
                      pl.BlockSpec((None, d, tf), lambda j, f, blk, exp, nu: (exp[j], 0, fidx(j, f, nu))),
                      pl.BlockSpec((None, tf, d), lambda j, f, blk, exp, nu: (exp[j], fidx(j, f, nu), 0))],
            out_specs=row_spec,
            scratch_shapes=[pltpu.VMEM((tmx, d), wg.dtype), pltpu.VMEM((tmx, d), F32)],
        ),
        out_shape=jax.ShapeDtypeStruct(xs.shape, F32),
        compiler_params=_params("arbitrary", "arbitrary"),
        name="moe_grouped_experts",
    )(blk, exp, nused, xs, wg, wu, wd)


def _combine_kernel(pos_ref, route_ref, x_ref, gn_ref, gate_ref, out_hbm, y_ref, buf, sem, *, tm):
    def issue(t, carry):
        for k in range(2):
            pltpu.make_async_copy(_token_tile(out_hbm, pos_ref[0, k * tm + t]),
                                  _token_tile(buf, k * tm + t), sem).start()
        return carry

    lax.fori_loop(0, tm, issue, 0)
    pltpu.make_async_copy(out_hbm.at[pl.ds(0, 2 * tm * SUBLANES), :], buf, sem).wait()
    route = route_ref[...]
    g1, g2 = route[:, 2:3], route[:, 3:4]
    parts = []
    for c in range(SLABS):
        parts.append(g1 * buf[pl.ds(c, tm, stride=SUBLANES), :]
                     + g2 * buf[pl.ds(tm * SUBLANES + c, tm, stride=SUBLANES), :])
    y = jnp.concatenate(parts, axis=1)
    y_ref[...] = x_ref[...] + gate_ref[...] * _rms(y, gn_ref[...])


def _combine_call(pos, route, x, gn, gate, out2d, tm):
    bm, r, d = x.shape
    nt = r // tm
    return pl.pallas_call(
        functools.partial(_combine_kernel, tm=tm),
        grid=(bm, nt),
        in_specs=[pl.BlockSpec((None, 1, 2 * tm), lambda b, i: (b * nt + i, 0, 0), memory_space=pltpu.SMEM),
                  _rows(tm, LANES), _rows(tm, d), _const(gn), _mod(gate, tm),
                  pl.BlockSpec(memory_space=pl.ANY)],
        out_specs=_rows(tm, d),
        out_shape=jax.ShapeDtypeStruct((bm, r, d), F32),
        scratch_shapes=[pltpu.VMEM((2 * tm * SUBLANES, LANES), F32), pltpu.SemaphoreType.DMA(())],
        compiler_params=_params("arbitrary", "arbitrary"),
        name="moe_combine",
    )(pos, route, x, gn, gate, out2d)


def _moe_sparse(x, g, sc, sh, wr, br, wg, wu, wd, gn, gate, tm):
    bm, r, d = x.shape
    m = bm * r
    tmx = MOE_TILE_ROWS
    stride = m + tmx
    n_tiles = 2 * m // tmx + N_EXPERTS
    h2d, route, cnt = _route_call(x, g, sc, sh, wr, br, tm, stride)
    cnt = cnt[0, :N_EXPERTS].astype(jnp.int32)
    pos = route[..., :2].astype(jnp.int32).reshape(m // tm, tm, 2).transpose(0, 2, 1).reshape(m // tm, 1, 2 * tm)
    xs = _dispatch_call(cnt, pos, h2d, tm, stride, tmx)
    per = (cnt + tmx - 1) // tmx
    ends = jnp.cumsum(per)
    j = jnp.minimum(jnp.arange(n_tiles, dtype=jnp.int32), ends[-1] - 1)
    exp = jnp.sum((j[:, None] >= ends[None, :]).astype(jnp.int32), axis=1)
    blk = exp * (stride // tmx) + j - (ends - per)[exp]
    out2d = _expert_call(blk, exp, ends[-1:], xs, wg, wu, wd, tmx, wg.shape[2] // 2)
    return _combine_call(pos, route, x, gn, gate, out2d, tm)


def _router_kernel(x_ref, g_ref, sc_ref, sh_ref, wr_ref, br_ref, h_ref, comb_ref):
    h = _rms(x_ref[...], g_ref[...]) * (1.0 + sc_ref[...]) + sh_ref[...]
    h_ref[...] = h.astype(h_ref.dtype)
    logits = _mm(h, wr_ref[...]) + br_ref[...]
    lane, i1, i2, g1, g2 = _top2(logits)
    comb_ref[...] = jnp.where(lane == i1, g1, 0.0) + jnp.where(lane == i2, g2, 0.0)


def _router_call(x, g, sc, sh, wr, br, h_dtype, tm):
    bm, r, d = x.shape
    return pl.pallas_call(
        _router_kernel,
        grid=(bm, r // tm),
        in_specs=[_rows(tm, d), _const(g), _mod(sc, tm), _mod(sh, tm), _const(wr), _const(br)],
        out_specs=[_rows(tm, d), _rows(tm, LANES)],
        out_shape=[jax.ShapeDtypeStruct((bm, r, d), h_dtype),
                   jax.ShapeDtypeStruct((bm, r, LANES), F32)],
        compiler_params=_params("arbitrary", "arbitrary"),
        name="moe_router",
    )(x, g, sc, sh, wr, br)


def _moe_kernel(h_ref, comb_ref, x_ref, wg_ref, wu_ref, wd_ref, gn_ref, gate_ref, o_ref, acc_scr):
    e = pl.program_id(2)
    f = pl.program_id(3)

    @pl.when((e == 0) & (f == 0))
    def _():
        acc_scr[...] = jnp.zeros_like(acc_scr)

    comb = comb_ref[...]
    lane = lax.broadcasted_iota(jnp.int32, comb.shape, 1)
    cw = jnp.sum(jnp.where(lane == e, comb, 0.0), axis=-1, keepdims=True)
    h = h_ref[...]
    he = _silu(_mm(h, wg_ref[...], False)) * _mm(h, wu_ref[...], False)
    acc_scr[...] += cw * _mm(he, wd_ref[...], False)

    @pl.when((e == pl.num_programs(2) - 1) & (f == pl.num_programs(3) - 1))
    def _():
        o_ref[...] = x_ref[...] + gate_ref[...] * _rms(acc_scr[...], gn_ref[...])


def _moe_call(h, comb, x, wg, wu, wd, gn, gate, tm, tf):
    bm, r, d = x.shape
    ne, _, ff = wg.shape
    return pl.pallas_call(
        _moe_kernel,
        grid=(bm, r // tm, ne, ff // tf),
        in_specs=[_rows(tm, d), _rows(tm, LANES), _rows(tm, d),
                  pl.BlockSpec((None, d, tf), lambda b, i, e, f: (e, 0, f)),
                  pl.BlockSpec((None, d, tf), lambda b, i, e, f: (e, 0, f)),
                  pl.BlockSpec((None, tf, d), lambda b, i, e, f: (e, f, 0)),
                  _const(gn), _mod(gate, tm)],
        out_specs=_rows(tm, d),
        out_shape=jax.ShapeDtypeStruct((bm, r, d), F32),
        scratch_shapes=[pltpu.VMEM((tm, d), F32)],
        compiler_params=_params("arbitrary", "arbitrary", "arbitrary", "arbitrary"),
        name="moe_experts",
    )(h, comb, x, wg, wu, wd, gn, gate)


def _ssm_params(lam_re, lam_im, log_dt, b_re, b_im, c_re, c_im, wdtype):
    dt = jnp.exp(log_dt)[:, None]
    mag = jnp.exp(lam_re * dt)
    a_re, a_im = mag * jnp.cos(lam_im * dt), mag * jnp.sin(lam_im * dt)
    den = lam_re * lam_re + lam_im * lam_im
    f_re = ((a_re - 1.0) * lam_re + a_im * lam_im) / den
    f_im = (a_im * lam_re - (a_re - 1.0) * lam_im) / den
    bb_re = f_re[..., None] * b_re - f_im[..., None] * b_im
    bb_im = f_re[..., None] * b_im + f_im[..., None] * b_re
    eye = jnp.eye(GROUPS_PER_SLAB, dtype=F32)

    def in_map(bb):
        t = bb.reshape(SLABS, GROUPS_PER_SLAB, SSM_STATE, SSM_CH)
        w = jnp.einsum('sgpc,gh->sgchp', t, eye)
        return w.reshape(SLABS, LANES, SLAB_STATES).astype(wdtype)

    def out_map(cc):
        t = cc.reshape(SLABS, GROUPS_PER_SLAB, SSM_CH, SSM_STATE)
        w = jnp.einsum('sgcp,gh->sgphc', t, eye)
        return w.reshape(SLABS, SLAB_STATES, LANES).astype(wdtype)

    return a_re, a_im, in_map(bb_re), in_map(bb_im), out_map(c_re), out_map(c_im)


def _rope_tables(pos):
    half = ROT_DIM // 2
    inv = ROPE_THETA ** (-jnp.arange(half, dtype=F32) / half)
    ang = pos.astype(F32)[:, None] * inv[None, :]
    cos, sin = jnp.cos(ang), jnp.sin(ang)
    n = pos.shape[0]
    pad = jnp.zeros((n, HEAD_DIM - ROT_DIM), F32)
    zero = jnp.zeros((n, half), F32)
    cos_h = jnp.concatenate([cos, cos, pad + 1.0], axis=1)
    lo_h = jnp.concatenate([-sin, zero, pad], axis=1)
    hi_h = jnp.concatenate([zero, sin, pad], axis=1)
    rep = LANES // HEAD_DIM
    return jnp.tile(cos_h, (1, rep)), jnp.tile(lo_h, (1, rep)), jnp.tile(hi_h, (1, rep))


def _trunk(x, mods, pos, wts, prompt, ssm0=None, cache=None):
    bm, r, d = x.shape
    tm = 512 if prompt else r
    wd = BF16 if prompt else F32
    cast = lambda w: w.astype(wd)
    norm_g = wts['norm_g'].reshape(DEPTH, 4, 1, d)

    sh1, sc1, g1, sh2, sc2, g2 = mods[0]
    u = _inproj_call(x, norm_g[0, 0], sc1, sh1, cast(wts['ssm_w_in'][0]), tm)
    a_re, a_im, wbr, wbi, cr, ci = _ssm_params(
        wts['ssm_lam_re'][0], wts['ssm_lam_im'][0], wts['ssm_log_dt'][0], wts['ssm_b_re'][0],
        wts['ssm_b_im'][0], wts['ssm_c_re'][0], wts['ssm_c_im'][0], wd)
    dvec = wts['ssm_d'][0].reshape(1, d)
    if prompt:
        z, fr, fi = _ssm_prompt_call(u, wbr, wbi, a_re.reshape(STATE_TILES, LANES),
                                     a_im.reshape(STATE_TILES, LANES), cr, ci, dvec, 256)
    else:
        z, fr, fi = _ssm_sample_call(u[0], ssm0[0].reshape(r, SSM_FLAT), ssm0[1].reshape(r, SSM_FLAT),
                                     wbr, wbi, a_re.reshape(1, SSM_FLAT), a_im.reshape(1, SSM_FLAT),
                                     cr, ci, dvec)
        z = z[None]
    fr = fr.reshape(-1, SSM_GROUPS, SSM_STATE)[None]
    fi = fi.reshape(-1, SSM_GROUPS, SSM_STATE)[None]
    x = _s5out_call(z, x, cast(wts['ssm_w_glu'][0]), cast(wts['ssm_w_out'][0]), norm_g[0, 1], g1, tm)
    tf = wts['ffn_w_gate'].shape[2] // 2
    x = _ffn_call(x, norm_g[0, 2], sc2, sh2, cast(wts['ffn_w_gate'][0]), cast(wts['ffn_w_up'][0]),
                  cast(wts['ffn_w_down'][0]), norm_g[0, 3], g2, tm, tf)

    sh1, sc1, g1, sh2, sc2, g2 = mods[1]
    cos, slo, shi = _rope_tables(pos)
    qkv = _qkv_call(x, wts['kv_norm_g'].reshape(1, d), norm_g[1, 0], sc1, sh1, cast(wts['w_kv']),
                    cast(wts['attn_w_q'][0]), cos, slo, shi, tm, prompt)
    wo = cast(wts['attn_w_o'][0])
    wr = cast(jnp.zeros((d, LANES), F32).at[:, :N_EXPERTS].set(wts['moe_w_router'][0]))
    br = jnp.full((1, LANES), NEG_INF, F32).at[0, :N_EXPERTS].set(wts['moe_b_router'][0])
    wg, wu, wdn = cast(wts['moe_w_gate'][0]), cast(wts['moe_w_up'][0]), cast(wts['moe_w_down'][0])
    if prompt:
        k, v, kd, vd, qd = qkv
        os_, ls_ = [], []
        for g, (_, dil) in enumerate(DILATION_GROUPS):
            o, lse = _band_call(qd[g], kd[g], vd[g], min(4, r // dil // SPAN))
            os_.append(o)
            ls_.append(lse)
        x = _attn_out_call(os_, ls_, x, wo, norm_g[1, 1], g1, tm)
        x = _moe_sparse(x, norm_g[1, 2], sc2, sh2, wr, br, wg, wu, wdn, norm_g[1, 3], g2, tm)
    else:
        k, v, q = qkv
        o = _sample_attn_call(q[0].reshape(r, N_DIL_GROUPS, N_KV_HEADS, HEAD_DIM),
                              k[0].reshape(r, N_KV_HEADS, HEAD_DIM), v[0].reshape(r, N_KV_HEADS, HEAD_DIM),
                              cache[0], cache[1], 4)
        x = _proj_res_call(o.reshape(1, r, KV_DIM), x, wo, norm_g[1, 1], g1, tm)
        h, comb = _router_call(x, norm_g[1, 2], sc2, sh2, wr, br, wd, tm)
        x = _moe_call(h, comb, x, wg, wu, wdn, norm_g[1, 3], g2, tm, wg.shape[2] // 4)
    return x, fr, fi, k, v


def kernel(x_prompt, x_sample, state_ssm_re, state_ssm_im, cache_k, cache_v, c_prompt, c_sample, ada_w, ada_b, norm_g, ssm_w_in, ssm_lam_re, ssm_lam_im, ssm_log_dt, ssm_b_re, ssm_b_im, ssm_c_re, ssm_c_im, ssm_d, ssm_w_glu, ssm_w_out, kv_norm_g, w_kv, attn_w_q, attn_w_o, ffn_w_gate, ffn_w_up, ffn_w_down, moe_w_router, moe_b_router, moe_w_gate, moe_w_up, moe_w_down):
    wts = dict(norm_g=norm_g, ssm_w_in=ssm_w_in, ssm_lam_re=ssm_lam_re, ssm_lam_im=ssm_lam_im,
               ssm_log_dt=ssm_log_dt, ssm_b_re=ssm_b_re, ssm_b_im=ssm_b_im, ssm_c_re=ssm_c_re,
               ssm_c_im=ssm_c_im, ssm_d=ssm_d, ssm_w_glu=ssm_w_glu, ssm_w_out=ssm_w_out,
               kv_norm_g=kv_norm_g, w_kv=w_kv, attn_w_q=attn_w_q, attn_w_o=attn_w_o,
               ffn_w_gate=ffn_w_gate, ffn_w_up=ffn_w_up, ffn_w_down=ffn_w_down,
               moe_w_router=moe_w_router, moe_b_router=moe_b_router, moe_w_gate=moe_w_gate,
               moe_w_up=moe_w_up, moe_w_down=moe_w_down)
    nb, seq, d = x_prompt.shape
    ns = x_sample.shape[0]
    assert x_sample.shape[1] == 1 and ns % SUBLANES == 0

    pad = (-(ns + nb)) % SUBLANES
    c_all = jnp.concatenate([c_sample, c_prompt, jnp.zeros((pad, d), F32)], axis=0)
    mod_all = _ada_call(c_all, ada_w, ada_b)
    mods_s = [[mod_all[l, k, :ns][None] for k in range(6)] for l in range(DEPTH)]
    mods_p = [[mod_all[l, k, ns:ns + nb][:, None, :] for k in range(6)] for l in range(DEPTH)]

    pos_p = jnp.arange(seq, dtype=jnp.int32)
    pos_s = jnp.full((ns,), PAST_LEN, dtype=jnp.int32)
    y_p, sr_p, si_p, k_p, v_p = _trunk(x_prompt, mods_p, pos_p, wts, True)
    xs = x_sample.reshape(1, ns, d)
    y_s, sr_s, si_s, k_s, v_s = _trunk(xs, mods_s, pos_s, wts, False,
                                       ssm0=(state_ssm_re[0], state_ssm_im[0]), cache=(cache_k, cache_v))

    keep = min(MAX_WINDOW, seq)
    k_rows_p = k_p[:, seq - keep:].reshape(nb, keep, N_KV_HEADS, HEAD_DIM)
    v_rows_p = v_p[:, seq - keep:].reshape(nb, keep, N_KV_HEADS, HEAD_DIM)
    k_rows_s = jnp.concatenate([cache_k[:, 1:], k_s[0].reshape(ns, 1, N_KV_HEADS, HEAD_DIM)], axis=1)
    v_rows_s = jnp.concatenate([cache_v[:, 1:], v_s[0].reshape(ns, 1, N_KV_HEADS, HEAD_DIM)], axis=1)
    return (y_p, y_s.reshape(ns, 1, d), sr_p, si_p, sr_s, si_s, k_rows_p, v_rows_p, k_rows_s, v_rows_s)
```

```python
import functools
import math

import jax
import jax.numpy as jnp
from jax import lax
from jax.experimental import pallas as pl
from jax.experimental.pallas import tpu as pltpu

F32 = jnp.float32
BF16 = jnp.bfloat16
HIGHEST = lax.Precision.HIGHEST

D_MODEL = 1024
DEPTH = 2
PAST_LEN = 16384
SSM_CH = 16
SSM_GROUPS = D_MODEL // SSM_CH
SSM_STATE = 64
SSM_FLAT = SSM_GROUPS * SSM_STATE
HEAD_DIM = 64
N_KV_HEADS = D_MODEL // 128
KV_DIM = N_KV_HEADS * HEAD_DIM
DILATION_GROUPS = ((128, 1), (512, 4), (2048, 16))
N_DIL_GROUPS = len(DILATION_GROUPS)
SPAN = 128
MAX_WINDOW = 2048
ROT_DIM = HEAD_DIM // 4
ROPE_THETA = 500000.0
N_EXPERTS = 8
RMS_EPS = 1e-6
NEG_INF = -1e30

LANES = 128
SUBLANES = 8
SLABS = D_MODEL // LANES
GROUPS_PER_SLAB = LANES // SSM_CH
SLAB_STATES = GROUPS_PER_SLAB * SSM_STATE
STATE_TILES = SSM_FLAT // LANES
MOE_TILE_ROWS = 512
VMEM_LIMIT = 56 * 1024 * 1024


def _params(*sem):
    return pltpu.CompilerParams(dimension_semantics=sem, vmem_limit_bytes=VMEM_LIMIT)


def _mm(a, w, precise=True):
    if w.dtype == BF16 or not precise:
        return jnp.dot(a.astype(BF16), w.astype(BF16), preferred_element_type=F32)
    return jnp.dot(a.astype(F32), w, preferred_element_type=F32, precision=HIGHEST)


def _rms(x, g):
    return x * lax.rsqrt(jnp.mean(x * x, axis=-1, keepdims=True) + RMS_EPS) * g


def _sigmoid(x):
    return 1.0 / (1.0 + jnp.exp(-x))


def _silu(x):
    return x * _sigmoid(x)


def _gelu_tanh(x):
    return 0.5 * x * (1.0 + jnp.tanh(math.sqrt(2.0 / math.pi) * (x + 0.044715 * (x * x * x))))


def _rows(tm, n):
    return pl.BlockSpec((None, tm, n), lambda b, i, *_: (b, i, 0))


def _mod(arr, tm):
    if arr.shape[1] == 1:
        return pl.BlockSpec((None, 1, arr.shape[2]), lambda b, i, *_: (b, 0, 0))
    return pl.BlockSpec((None, tm, arr.shape[2]), lambda b, i, *_: (b, i, 0))


def _const(arr):
    nd = arr.ndim
    return pl.BlockSpec(arr.shape, lambda *_: (0,) * nd)


def _ada_kernel(c_ref, w_ref, b_ref, o_ref):
    o_ref[...] = _mm(_silu(c_ref[...]), w_ref[...]) + b_ref[...]


def _ada_call(c_all, ada_w, ada_b):
    rows = c_all.shape[0]
    b4 = ada_b.reshape(DEPTH, 6, 1, D_MODEL)
    return pl.pallas_call(
        _ada_kernel,
        grid=(DEPTH, 6),
        in_specs=[pl.BlockSpec((rows, D_MODEL), lambda l, k: (0, 0)),
                  pl.BlockSpec((None, D_MODEL, D_MODEL), lambda l, k: (l, 0, k)),
                  pl.BlockSpec((None, None, 1, D_MODEL), lambda l, k: (l, k, 0, 0))],
        out_specs=pl.BlockSpec((None, None, rows, D_MODEL), lambda l, k: (l, k, 0, 0)),
        out_shape=jax.ShapeDtypeStruct((DEPTH, 6, rows, D_MODEL), F32),
        compiler_params=_params("arbitrary", "arbitrary"),
        name="ada_modulation",
    )(c_all, ada_w, b4)


def _inproj_kernel(x_ref, g_ref, sc_ref, sh_ref, w_ref, u_ref):
    h = _rms(x_ref[...], g_ref[...]) * (1.0 + sc_ref[...]) + sh_ref[...]
    u_ref[...] = _mm(h, w_ref[...])


def _inproj_call(x, g, sc, sh, w, tm):
    bm, r, d = x.shape
    return pl.pallas_call(
        _inproj_kernel,
        grid=(bm, r // tm),
        in_specs=[_rows(tm, d), _const(g), _mod(sc, tm), _mod(sh, tm), _const(w)],
        out_specs=_rows(tm, w.shape[1]),
        out_shape=jax.ShapeDtypeStruct((bm, r, w.shape[1]), F32),
        compiler_params=_params("arbitrary", "arbitrary"),
        name="s5_in_proj",
    )(x, g, sc, sh, w)


def _ssm_prompt_kernel(u_ref, wbr_ref, wbi_ref, ar_ref, ai_ref, cr_ref, ci_ref, d_ref,
                       z_ref, fr_ref, fi_ref, br_scr, bi_scr, sr_scr, si_scr, *, t_chunk, nseq):
    tb_n = t_chunk // SUBLANES
    tiles = SLAB_STATES // LANES

    @pl.when(pl.program_id(0) == 0)
    def _():
        sr_scr[...] = jnp.zeros_like(sr_scr)
        si_scr[...] = jnp.zeros_like(si_scr)

    for q in range(nseq):
        ub = u_ref[q].astype(BF16)
        for s in range(SLABS):
            us = ub[:, s * LANES:(s + 1) * LANES]
            bre = jnp.dot(us, wbr_ref[s], preferred_element_type=F32)
            bim = jnp.dot(us, wbi_ref[s], preferred_element_type=F32)
            for jj in range(tiles):
                rows = slice((s * tiles + jj) * SUBLANES, (s * tiles + jj + 1) * SUBLANES)
                cols = slice(jj * LANES, (jj + 1) * LANES)
                br_scr[q, :, rows, :] = bre[:, cols].reshape(tb_n, SUBLANES, LANES)
                bi_scr[q, :, rows, :] = bim[:, cols].reshape(tb_n, SUBLANES, LANES)

    ar = ar_ref[...]
    ai = ai_ref[...]

    def body(tb, carry):
        carry = list(carry)
        for r in range(SUBLANES):
            step = pl.ds(r, STATE_TILES, stride=SUBLANES)
            for q in range(nseq):
                sr, si = carry[q]
                nsr = ar * sr - ai * si + br_scr[q, tb, step, :]
                nsi = ar * si + ai * sr + bi_scr[q, tb, step, :]
                carry[q] = (nsr, nsi)
                br_scr[q, tb, step, :] = nsr
                bi_scr[q, tb, step, :] = nsi
        return tuple(carry)

    final = lax.fori_loop(0, tb_n, body, tuple((sr_scr[q], si_scr[q]) for q in range(nseq)))
    for q in range(nseq):
        sr_scr[q], si_scr[q] = final[q]
        fr_ref[q], fi_ref[q] = final[q]

    for q in range(nseq):
        for s in range(SLABS):
            y = None
            for jj in range(tiles):
                rows = slice((s * tiles + jj) * SUBLANES, (s * tiles + jj + 1) * SUBLANES)
                cols = slice(jj * LANES, (jj + 1) * LANES)
                sre = br_scr[q, :, rows, :].reshape(t_chunk, LANES).astype(BF16)
                sim = bi_scr[q, :, rows, :].reshape(t_chunk, LANES).astype(BF16)
                part = (jnp.dot(sre, cr_ref[s, cols, :], preferred_element_type=F32)
                        - jnp.dot(sim, ci_ref[s, cols, :], preferred_element_type=F32))
                y = part if y is None else y + part
            sl = slice(s * LANES, (s + 1) * LANES)
            z_ref[q, :, sl] = _gelu_tanh(y + d_ref[:, sl] * u_ref[q, :, sl])


def _ssm_prompt_call(u, wbr, wbi, ar, ai, cr, ci, d, t_chunk):
    b, l, dm = u.shape
    kern = functools.partial(_ssm_prompt_kernel, t_chunk=t_chunk, nseq=b)
    seq_spec = pl.BlockSpec((b, t_chunk, dm), lambda c: (0, c, 0))
    state_spec = pl.BlockSpec((b, STATE_TILES, LANES), lambda c: (0, 0, 0))
    scr = pltpu.VMEM((b, t_chunk // SUBLANES, STATE_TILES * SUBLANES, LANES), F32)
    carry = pltpu.VMEM((b, STATE_TILES, LANES), F32)
    return pl.pallas_call(
        kern,
        grid=(l // t_chunk,),
        in_specs=[seq_spec, _const(wbr), _const(wbi), _const(ar), _const(ai),
                  _const(cr), _const(ci), _const(d)],
        out_specs=[seq_spec, state_spec, state_spec],
        out_shape=[jax.ShapeDtypeStruct((b, l, dm), F32),
                   jax.ShapeDtypeStruct((b, STATE_TILES, LANES), F32),
                   jax.ShapeDtypeStruct((b, STATE_TILES, LANES), F32)],
        scratch_shapes=[scr, scr, carry, carry],
        compiler_params=_params("arbitrary"),
        name="s5_scan_prompt",
    )(u, wbr, wbi, ar, ai, cr, ci, d)


def _ssm_sample_kernel(u_ref, x0r_ref, x0i_ref, wbr_ref, wbi_ref, ar_ref, ai_ref, cr_ref, ci_ref,
                       d_ref, z_ref, fr_ref, fi_ref):
    for s in range(SLABS):
        sl = slice(s * LANES, (s + 1) * LANES)
        st = slice(s * SLAB_STATES, (s + 1) * SLAB_STATES)
        us = u_ref[:, sl]
        a_r, a_i = ar_ref[:, st], ai_ref[:, st]
        x0r, x0i = x0r_ref[:, st], x0i_ref[:, st]
        sr = _mm(us, wbr_ref[s]) + a_r * x0r - a_i * x0i
        si = _mm(us, wbi_ref[s]) + a_r * x0i + a_i * x0r
        fr_ref[:, st] = sr
        fi_ref[:, st] = si
        y = _mm(sr, cr_ref[s]) - _mm(si, ci_ref[s])
        z_ref[:, sl] = _gelu_tanh(y + d_ref[:, sl] * us)


def _ssm_sample_call(u, x0r, x0i, wbr, wbi, ar, ai, cr, ci, d):
    n = u.shape[0]
    args = (u, x0r, x0i, wbr, wbi, ar, ai, cr, ci, d)
    return pl.pallas_call(
        _ssm_sample_kernel,
        grid=(1,),
        in_specs=[_const(a) for a in args],
        out_specs=[pl.BlockSpec((n, D_MODEL), lambda i: (0, 0)),
                   pl.BlockSpec((n, SSM_FLAT), lambda i: (0, 0)),
                   pl.BlockSpec((n, SSM_FLAT), lambda i: (0, 0))],
        out_shape=[jax.ShapeDtypeStruct((n, D_MODEL), F32),
                   jax.ShapeDtypeStruct((n, SSM_FLAT), F32),
                   jax.ShapeDtypeStruct((n, SSM_FLAT), F32)],
        compiler_params=_params("arbitrary"),
        name="s5_step_sample",
    )(*args)


def _s5out_kernel(z_ref, x_ref, wg_ref, wo_ref, gn_ref, gate_ref, o_ref):
    z = z_ref[...]
    gl = z * _sigmoid(_mm(z, wg_ref[...]))
    out = _mm(gl, wo_ref[...])
    o_ref[...] = x_ref[...] + gate_ref[...] * _rms(out, gn_ref[...])


def _s5out_call(z, x, wg, wo, gn, gate, tm):
    bm, r, d = x.shape
    return pl.pallas_call(
        _s5out_kernel,
        grid=(bm, r // tm),
        in_specs=[_rows(tm, d), _rows(tm, d), _const(wg), _const(wo), _const(gn), _mod(gate, tm)],
        out_specs=_rows(tm, d),
        out_shape=jax.ShapeDtypeStruct((bm, r, d), F32),
        compiler_params=_params("arbitrary", "arbitrary"),
        name="s5_glu_out_proj",
    )(z, x, wg, wo, gn, gate)


def _ffn_kernel(x_ref, g_ref, sc_ref, sh_ref, wg_ref, wu_ref, wd_ref, gn_ref, gate_ref, o_ref,
                h_scr, acc_scr):
    f = pl.program_id(2)

    @pl.when(f == 0)
    def _():
        h = _rms(x_ref[...], g_ref[...]) * (1.0 + sc_ref[...]) + sh_ref[...]
        h_scr[...] = h.astype(h_scr.dtype)
        acc_scr[...] = jnp.zeros_like(acc_scr)

    h = h_scr[...]
    he = _silu(_mm(h, wg_ref[...])) * _mm(h, wu_ref[...])
    acc_scr[...] += _mm(he, wd_ref[...])

    @pl.when(f == pl.num_programs(2) - 1)
    def _():
        o_ref[...] = x_ref[...] + gate_ref[...] * _rms(acc_scr[...], gn_ref[...])


def _ffn_call(x, g, sc, sh, wg, wu, wd, gn, gate, tm, tf):
    bm, r, d = x.shape
    ff = wg.shape[1]
    return pl.pallas_call(
        _ffn_kernel,
        grid=(bm, r // tm, ff // tf),
        in_specs=[_rows(tm, d), _const(g), _mod(sc, tm), _mod(sh, tm),
                  pl.BlockSpec((d, tf), lambda b, i, f: (0, f)),
                  pl.BlockSpec((d, tf), lambda b, i, f: (0, f)),
                  pl.BlockSpec((tf, d), lambda b, i, f: (f, 0)),
                  _const(gn), _mod(gate, tm)],
        out_specs=_rows(tm, d),
        out_shape=jax.ShapeDtypeStruct((bm, r, d), F32),
        scratch_shapes=[pltpu.VMEM((tm, d), wg.dtype), pltpu.VMEM((tm, d), F32)],
        compiler_params=_params("arbitrary", "arbitrary", "arbitrary"),
        name="dense_swiglu",
    )(x, g, sc, sh, wg, wu, wd, gn, gate)


def _rope(t, cos, sin_lo, sin_hi):
    half = ROT_DIM // 2
    outs = []
    for s in range(t.shape[1] // LANES):
        ts = t[:, s * LANES:(s + 1) * LANES]
        outs.append(ts * cos + pltpu.roll(ts, LANES - half, 1) * sin_lo + pltpu.roll(ts, half, 1) * sin_hi)
    return jnp.concatenate(outs, axis=1)


def _qkv_values(x_ref, gkv_ref, g_ref, sc_ref, sh_ref, wkv_ref, wq_ref, cos_ref, slo_ref, shi_ref):
    x = x_ref[...]
    xn = x * lax.rsqrt(jnp.mean(x * x, axis=-1, keepdims=True) + RMS_EPS)
    kv = _mm(xn * gkv_ref[...], wkv_ref[...])
    q = _mm((xn * g_ref[...]) * (1.0 + sc_ref[...]) + sh_ref[...], wq_ref[...])
    cos, slo, shi = cos_ref[...], slo_ref[...], shi_ref[...]
    k = _rope(kv[:, :KV_DIM], cos, slo, shi)
    v = kv[:, KV_DIM:]
    q = _rope(q, cos, slo, shi) * (HEAD_DIM ** -0.5)
    return k, v, q


def _qkv_sample_kernel(*refs):
    k_ref, v_ref, q_ref = refs[10:]
    k, v, q = _qkv_values(*refs[:10])
    k_ref[...] = k
    v_ref[...] = v
    q_ref[...] = q


def _store_residues(dst_ref, val, scr, dil):
    if dil == 1:
        dst_ref[0] = val.astype(dst_ref.dtype)
        return
    slabs = val.shape[1] // LANES
    for s in range(slabs):
        scr[s] = val[:, s * LANES:(s + 1) * LANES]
    n = val.shape[0] // dil
    for r in range(dil):
        rows = [scr[s, pl.ds(r, n, stride=dil), :] for s in range(slabs)]
        dst_ref[r] = jnp.concatenate(rows, axis=1).astype(dst_ref.dtype)


def _qkv_prompt_kernel(*refs):
    k_ref, v_ref = refs[10:12]
    kd_refs = refs[12:12 + N_DIL_GROUPS]
    vd_refs = refs[12 + N_DIL_GROUPS:12 + 2 * N_DIL_GROUPS]
    qd_refs = refs[12 + 2 * N_DIL_GROUPS:12 + 3 * N_DIL_GROUPS]
    scr = refs[12 + 3 * N_DIL_GROUPS]
    k, v, q = _qkv_values(*refs[:10])
    k_ref[...] = k
    v_ref[...] = v
    for g, (_, dil) in enumerate(DILATION_GROUPS):
        _store_residues(kd_refs[g], k, scr, dil)
        _store_residues(vd_refs[g], v, scr, dil)
        _store_residues(qd_refs[g], q[:, g * KV_DIM:(g + 1) * KV_DIM], scr, dil)


def _qkv_call(x, gkv, g, sc, sh, wkv, wq, cos, slo, shi, tm, prompt):
    bm, r, d = x.shape
    nq = wq.shape[1]
    tab = pl.BlockSpec((tm, LANES), lambda b, i: (i, 0))
    in_specs = [_rows(tm, d), _const(gkv), _const(g), _mod(sc, tm), _mod(sh, tm),
                _const(wkv), _const(wq), tab, tab, tab]
    out_specs = [_rows(tm, KV_DIM), _rows(tm, KV_DIM)]
    out_shape = [jax.ShapeDtypeStruct((bm, r, KV_DIM), F32), jax.ShapeDtypeStruct((bm, r, KV_DIM), F32)]
    if not prompt:
        return pl.pallas_call(
            _qkv_sample_kernel,
            grid=(bm, r // tm),
            in_specs=in_specs,
            out_specs=out_specs + [_rows(tm, nq)],
            out_shape=out_shape + [jax.ShapeDtypeStruct((bm, r, nq), F32)],
            compiler_params=_params("arbitrary", "arbitrary"),
            name="qkv_proj_rope_sample",
        )(x, gkv, g, sc, sh, wkv, wq, cos, slo, shi)
    for _ in range(3):
        for _, dil in DILATION_GROUPS:
            out_specs.append(pl.BlockSpec((None, dil, tm // dil, KV_DIM), lambda b, i: (b, 0, i, 0)))
            out_shape.append(jax.ShapeDtypeStruct((bm, dil, r // dil, KV_DIM), BF16))
    outs = pl.pallas_call(
        _qkv_prompt_kernel,
        grid=(bm, r // tm),
        in_specs=in_specs,
        out_specs=out_specs,
        out_shape=out_shape,
        scratch_shapes=[pltpu.VMEM((KV_DIM // LANES, tm, LANES), F32)],
        compiler_params=_params("arbitrary", "arbitrary"),
        name="qkv_proj_rope_prompt",
    )(x, gkv, g, sc, sh, wkv, wq, cos, slo, shi)
    n = N_DIL_GROUPS
    return outs[0], outs[1], outs[2:2 + n], outs[2 + n:2 + 2 * n], outs[2 + 2 * n:2 + 3 * n]


def _band_kernel(q_ref, kp_ref, kc_ref, vp_ref, vc_ref, o_ref, l_ref, *, nb):
    qi = lax.broadcasted_iota(jnp.int32, (SPAN, 2 * SPAN), 0)
    kj = lax.broadcasted_iota(jnp.int32, (SPAN, 2 * SPAN), 1)
    band = (kj >= qi) & (kj <= qi + SPAN)
    band_first = band & (kj >= jnp.where(pl.program_id(2) == 0, SPAN, 0))
    lane = lax.broadcasted_iota(jnp.int32, (SPAN, LANES), 1)
    head0 = lane < HEAD_DIM
    for i in range(nb):
        cur = slice(i * SPAN, (i + 1) * SPAN)
        prev = slice((i - 1) * SPAN, i * SPAN)
        for hp in range(KV_DIM // LANES):
            cols = slice(hp * LANES, (hp + 1) * LANES)
            q2 = q_ref[cur, cols]
            if i == 0:
                k2 = jnp.concatenate([kp_ref[:, cols], kc_ref[cur, cols]], axis=0)
                v2 = jnp.concatenate([vp_ref[:, cols], vc_ref[cur, cols]], axis=0)
                mask = band_first
            else:
                k2 = jnp.concatenate([kc_ref[prev, cols], kc_ref[cur, cols]], axis=0)
                v2 = jnp.concatenate([vc_ref[prev, cols], vc_ref[cur, cols]], axis=0)
                mask = band
            o_pair = None
            l_pair = None
            for hh in range(2):
                sel = head0 if hh == 0 else jnp.logical_not(head0)
                qm = jnp.where(sel, q2, jnp.zeros_like(q2))
                s = lax.dot_general(qm, k2, (((1,), (1,)), ((), ())), preferred_element_type=F32)
                s = jnp.where(mask, s, NEG_INF)
                m = jnp.max(s, axis=-1, keepdims=True)
                p = jnp.exp(s - m)
                den = jnp.sum(p, axis=-1, keepdims=True)
                o = jnp.dot(p.astype(BF16), v2, preferred_element_type=F32) / den
                lse = jnp.broadcast_to(m + jnp.log(den), (SPAN, LANES))
                o_pair = o if o_pair is None else jnp.where(head0, o_pair, o)
                l_pair = lse if l_pair is None else jnp.where(head0, l_pair, lse)
            o_ref[cur, cols] = o_pair
            l_ref[cur, cols] = l_pair


def _band_call(q, k, v, nb):
    b, dil, m, _ = k.shape
    tq = nb * SPAN
    cur = pl.BlockSpec((None, None, tq, KV_DIM), lambda bb, r, n: (bb, r, n, 0))
    prev = pl.BlockSpec((None, None, SPAN, KV_DIM), lambda bb, r, n: (bb, r, jnp.maximum(n * nb - 1, 0), 0))
    return pl.pallas_call(
        functools.partial(_band_kernel, nb=nb),
        grid=(b, dil, m // tq),
        in_specs=[cur, prev, cur, prev, cur],
        out_specs=[cur, cur],
        out_shape=[jax.ShapeDtypeStruct((b, dil, m, KV_DIM), F32),
                   jax.ShapeDtypeStruct((b, dil, m, KV_DIM), F32)],
        compiler_params=_params("arbitrary", "arbitrary", "arbitrary"),
        name=f"band_attention_dil{dil}",
    )(q, k, k, v, v)


def _sample_attn_kernel(q_ref, kn_ref, vn_ref, k1_ref, k4_ref, k16_ref, v1_ref, v4_ref, v16_ref,
                        o_ref, *, bb):
    for b in range(bb):
        kn, vn = kn_ref[b], vn_ref[b]
        outs, lses = [], []
        for g, (kc_ref, vc_ref) in enumerate(((k1_ref, v1_ref), (k4_ref, v4_ref), (k16_ref, v16_ref))):
            qg = q_ref[b, g]
            s = jnp.sum(kc_ref[b] * qg[None], axis=-1, keepdims=True)
            s_new = jnp.sum(kn * qg, axis=-1, keepdims=True)
            m = jnp.maximum(jnp.max(s, axis=0), s_new)
            p = jnp.exp(s - m[None])
            p_new = jnp.exp(s_new - m)
            den = jnp.sum(p, axis=0) + p_new
            outs.append((jnp.sum(p * vc_ref[b], axis=0) + p_new * vn) / den)
            lses.append(m + jnp.log(den))
        mx = jnp.maximum(jnp.maximum(lses[0], lses[1]), lses[2])
        es = [jnp.exp(l - mx) for l in lses]
        o_ref[b] = (es[0] * outs[0] + es[1] * outs[1] + es[2] * outs[2]) / (es[0] + es[1] + es[2])


def _sample_attn_call(q, k_new, v_new, cache_k, cache_v, bb):
    n, cl, nh, hd = cache_k.shape
    views, cspecs = [], []
    for cache in (cache_k, cache_v):
        for _, dil in DILATION_GROUPS:
            views.append(cache.reshape(n, cl // dil, dil, nh, hd))
            last = cl // dil // SPAN - 1
            cspecs.append(pl.BlockSpec((bb, SPAN, None, nh, hd), lambda i, last=last: (i, last, 0, 0, 0)))
    new = pl.BlockSpec((bb, nh, hd), lambda i: (i, 0, 0))
    return pl.pallas_call(
        functools.partial(_sample_attn_kernel, bb=bb),
        grid=(n // bb,),
        in_specs=[pl.BlockSpec((bb, N_DIL_GROUPS, nh, hd), lambda i: (i, 0, 0, 0)), new, new] + cspecs,
        out_specs=new,
        out_shape=jax.ShapeDtypeStruct((n, nh, hd), F32),
        compiler_params=_params("arbitrary"),
        name="window_attention_sample",
    )(q, k_new, v_new, *views)


def _attn_out_kernel(o1_ref, o2_ref, o3_ref, l1_ref, l2_ref, l3_ref, x_ref, wo_ref, gn_ref, gate_ref,
                     y_ref, *scrs):
    vals = []
    for ref, scr, (_, dil) in zip((o1_ref, o2_ref, o3_ref, l1_ref, l2_ref, l3_ref), scrs,
                                  DILATION_GROUPS + DILATION_GROUPS):
        if dil == 1:
            vals.append(ref[0])
            continue
        n = ref.shape[1]
        slabs = ref.shape[2] // LANES
        for r in range(dil):
            for s in range(slabs):
                scr[s, pl.ds(r, n, stride=dil), :] = ref[r, :, s * LANES:(s + 1) * LANES]
        vals.append(jnp.concatenate([scr[s] for s in range(slabs)], axis=1))
    o1, o2, o3, l1, l2, l3 = vals
    mx = jnp.maximum(jnp.maximum(l1, l2), l3)
    e1, e2, e3 = jnp.exp(l1 - mx), jnp.exp(l2 - mx), jnp.exp(l3 - mx)
    o = (e1 * o1 + e2 * o2 + e3 * o3) / (e1 + e2 + e3)
    out = _mm(o, wo_ref[...])
    y_ref[...] = x_ref[...] + gate_ref[...] * _rms(out, gn_ref[...])


def _attn_out_call(os_, ls_, x, wo, gn, gate, tm):
    bm, r, d = x.shape
    res = [pl.BlockSpec((None, dil, tm // dil, KV_DIM), lambda b, i: (b, 0, i, 0))
           for _, dil in DILATION_GROUPS]
    return pl.pallas_call(
        _attn_out_kernel,
        grid=(bm, r // tm),
        in_specs=res + res + [_rows(tm, d), _const(wo), _const(gn), _mod(gate, tm)],
        out_specs=_rows(tm, d),
        out_shape=jax.ShapeDtypeStruct((bm, r, d), F32),
        scratch_shapes=[pltpu.VMEM((KV_DIM // LANES, tm, LANES), F32)] * (2 * N_DIL_GROUPS),
        compiler_params=_params("arbitrary", "arbitrary"),
        name="attn_merge_out_proj",
    )(*os_, *ls_, x, wo, gn, gate)


def _proj_res_kernel(o_ref, x_ref, wo_ref, gn_ref, gate_ref, y_ref):
    out = _mm(o_ref[...], wo_ref[...])
    y_ref[...] = x_ref[...] + gate_ref[...] * _rms(out, gn_ref[...])


def _proj_res_call(o, x, wo, gn, gate, tm):
    bm, r, d = x.shape
    return pl.pallas_call(
        _proj_res_kernel,
        grid=(bm, r // tm),
        in_specs=[_rows(tm, o.shape[2]), _rows(tm, d), _const(wo), _const(gn), _mod(gate, tm)],
        out_specs=_rows(tm, d),
        out_shape=jax.ShapeDtypeStruct((bm, r, d), F32),
        compiler_params=_params("arbitrary", "arbitrary"),
        name="attn_out_proj",
    )(o, x, wo, gn, gate)


def _top2(logits):
    lane = lax.broadcasted_iota(jnp.int32, logits.shape, 1).astype(F32)
    m1 = jnp.max(logits, axis=-1, keepdims=True)
    i1 = jnp.min(jnp.where(logits == m1, lane, float(LANES)), axis=-1, keepdims=True)
    rest = jnp.where(lane == i1, -jnp.inf, logits)
    m2 = jnp.max(rest, axis=-1, keepdims=True)
    i2 = jnp.min(jnp.where(rest == m2, lane, float(LANES)), axis=-1, keepdims=True)
    e = jnp.exp(m2 - m1)
    return lane, i1, i2, 1.0 / (1.0 + e), e / (1.0 + e)


def _route_kernel(x_ref, g_ref, sc_ref, sh_ref, wr_ref, br_ref, h_ref, route_ref, cnt_ref, carry_scr,
                  *, stride):
    @pl.when((pl.program_id(0) == 0) & (pl.program_id(1) == 0))
    def _():
        carry_scr[...] = jnp.zeros_like(carry_scr)

    h = _rms(x_ref[...], g_ref[...]) * (1.0 + sc_ref[...]) + sh_ref[...]
    tm = h.shape[0]
    for c in range(SLABS):
        h_ref[pl.ds(c, tm, stride=SUBLANES), :] = h[:, c * LANES:(c + 1) * LANES]
    logits = _mm(h, wr_ref[...]) + br_ref[...]
    lane, i1, i2, g1, g2 = _top2(logits)
    onehot = jnp.where(lane == i1, 1.0, 0.0) + jnp.where(lane == i2, 1.0, 0.0)
    row = lax.broadcasted_iota(jnp.int32, (tm, tm), 0)
    col = lax.broadcasted_iota(jnp.int32, (tm, tm), 1)
    earlier = jnp.where(col < row, 1.0, 0.0).astype(BF16)
    before = jnp.dot(earlier, onehot.astype(BF16), preferred_element_type=F32) + carry_scr[0:1, :]
    r1 = jnp.sum(jnp.where(lane == i1, before, 0.0), axis=-1, keepdims=True)
    r2 = jnp.sum(jnp.where(lane == i2, before, 0.0), axis=-1, keepdims=True)
    p1 = i1 * float(stride) + r1
    p2 = i2 * float(stride) + r2
    route_ref[...] = (jnp.where(lane == 0.0, p1, 0.0) + jnp.where(lane == 1.0, p2, 0.0)
                      + jnp.where(lane == 2.0, g1, 0.0) + jnp.where(lane == 3.0, g2, 0.0))
    carry = carry_scr[...] + jnp.sum(onehot, axis=0, keepdims=True)
    carry_scr[...] = carry
    cnt_ref[...] = carry


def _route_call(x, g, sc, sh, wr, br, tm, stride):
    bm, r, d = x.shape
    nt = r // tm
    return pl.pallas_call(
        functools.partial(_route_kernel, stride=stride),
        grid=(bm, nt),
        in_specs=[_rows(tm, d), _const(g), _mod(sc, tm), _mod(sh, tm), _const(wr), _const(br)],
        out_specs=[pl.BlockSpec((tm * SUBLANES, LANES), lambda b, i: (b * nt + i, 0)),
                   _rows(tm, LANES),
                   pl.BlockSpec((SUBLANES, LANES), lambda b, i: (0, 0))],
        out_shape=[jax.ShapeDtypeStruct((bm * r * SUBLANES, LANES), F32),
                   jax.ShapeDtypeStruct((bm, r, LANES), F32),
                   jax.ShapeDtypeStruct((SUBLANES, LANES), F32)],
        scratch_shapes=[pltpu.VMEM((SUBLANES, LANES), F32)],
        compiler_params=_params("arbitrary", "arbitrary"),
        name="moe_route",
    )(x, g, sc, sh, wr, br)


def _token_tile(ref, idx):
    return ref.at[pl.ds(pl.multiple_of(idx * SUBLANES, SUBLANES), SUBLANES), :]


def _dispatch_kernel(cnt_ref, pos_ref, h_ref, xs_hbm, zero_scr, sem, zsem, *, tm, stride, tmx):
    i = pl.program_id(0)

    def issue(t, carry):
        src = _token_tile(h_ref, t)
        for k in range(2):
            pltpu.make_async_copy(src, _token_tile(xs_hbm, pos_ref[0, k * tm + t]), sem).start()
        return carry

    lax.fori_loop(0, tm, issue, 0)
    for _ in range(2):
        pltpu.make_async_copy(h_ref, xs_hbm.at[pl.ds(0, tm * SUBLANES), :], sem).wait()

    @pl.when(i == pl.num_programs(0) - 1)
    def _():
        zero_scr[...] = jnp.zeros_like(zero_scr)
        copies = []
        for e in range(N_EXPERTS):
            start = pl.multiple_of((e * stride + cnt_ref[e]) * SUBLANES, SUBLANES)
            copies.append(pltpu.make_async_copy(zero_scr, xs_hbm.at[pl.ds(start, tmx * SUBLANES), :], zsem))
            copies[-1].start()
        for cp in copies:
            cp.wait()


def _dispatch_call(cnt, pos, h2d, tm, stride, tmx):
    n_tiles = pos.shape[0]
    rows = N_EXPERTS * stride * SUBLANES
    return pl.pallas_call(
        functools.partial(_dispatch_kernel, tm=tm, stride=stride, tmx=tmx),
        grid_spec=pltpu.PrefetchScalarGridSpec(
            num_scalar_prefetch=1,
            grid=(n_tiles,),
            in_specs=[pl.BlockSpec((None, 1, 2 * tm), lambda i, cnt: (i, 0, 0), memory_space=pltpu.SMEM),
                      pl.BlockSpec((tm * SUBLANES, LANES), lambda i, cnt: (i, 0))],
            out_specs=pl.BlockSpec(memory_space=pl.ANY),
            scratch_shapes=[pltpu.VMEM((tmx * SUBLANES, LANES), F32),
                            pltpu.SemaphoreType.DMA(()), pltpu.SemaphoreType.DMA(())],
        ),
        out_shape=jax.ShapeDtypeStruct((rows, LANES), F32),
        compiler_params=_params("arbitrary"),
        name="moe_dispatch",
    )(cnt, pos, h2d)


def _expert_kernel(blk_ref, exp_ref, nused_ref, xs_ref, wg_ref, wu_ref, wd_ref, o_ref, x_scr, acc_scr,
                   *, tmx):
    j = pl.program_id(0)
    f = pl.program_id(1)

    @pl.when(j < nused_ref[0])
    def _():
        @pl.when(f == 0)
        def _():
            for c in range(SLABS):
                x_scr[:, c * LANES:(c + 1) * LANES] = (
                    xs_ref[pl.ds(c, tmx, stride=SUBLANES), :].astype(x_scr.dtype))
            acc_scr[...] = jnp.zeros_like(acc_scr)

        x = x_scr[...]
        he = _silu(_mm(x, wg_ref[...])) * _mm(x, wu_ref[...])
        acc_scr[...] += _mm(he, wd_ref[...])

        @pl.when(f == pl.num_programs(1) - 1)
        def _():
            for c in range(SLABS):
                o_ref[pl.ds(c, tmx, stride=SUBLANES), :] = acc_scr[:, c * LANES:(c + 1) * LANES]


def _expert_call(blk, exp, nused, xs, wg, wu, wd, tmx, tf):
    ne, d, ff = wg.shape
    nf = ff // tf

    def fidx(j, f, nused):
        return jnp.where(j < nused[0], f, nf - 1)

    row_spec = pl.BlockSpec((tmx * SUBLANES, LANES), lambda j, f, blk, exp, nu: (blk[j], 0))
    return pl.pallas_call(
        functools.partial(_expert_kernel, tmx=tmx),
        grid_spec=pltpu.PrefetchScalarGridSpec(
            num_scalar_prefetch=3,
            grid=(blk.shape[0], nf),
            in_specs=[row_spec,
                      pl.BlockSpec((None, d, tf), lambda j, f, blk, exp, nu: (exp[j], 0, fidx(j, f, nu))),
                      pl.BlockSpec((None, d, tf), lambda j, f, blk, exp, nu: (exp[j], 0, fidx(j, f, nu))),
                      pl.BlockSpec((None, tf, d), lambda j, f, blk, exp, nu: (exp[j], fidx(j, f, nu), 0))],
            out_specs=row_spec,
            scratch_shapes=[pltpu.VMEM((tmx, d), wg.dtype), pltpu.VMEM((tmx, d), F32)],
        ),
        out_shape=jax.ShapeDtypeStruct(xs.shape, F32),
        compiler_params=_params("arbitrary", "arbitrary"),
        name="moe_grouped_experts",
    )(blk, exp, nused, xs, wg, wu, wd)


def _combine_kernel(pos_ref, route_ref, x_ref, gn_ref, gate_ref, out_hbm, y_ref, buf, sem, *, tm):
    def issue(t, carry):
        for k in range(2):
            pltpu.make_async_copy(_token_tile(out_hbm, pos_ref[0, k * tm + t]),
                                  _token_tile(buf, k * tm + t), sem).start()
        return carry

    lax.fori_loop(0, tm, issue, 0)
    pltpu.make_async_copy(out_hbm.at[pl.ds(0, 2 * tm * SUBLANES), :], buf, sem).wait()
    route = route_ref[...]
    g1, g2 = route[:, 2:3], route[:, 3:4]
    parts = []
    for c in range(SLABS):
        parts.append(g1 * buf[pl.ds(c, tm, stride=SUBLANES), :]
                     + g2 * buf[pl.ds(tm * SUBLANES + c, tm, stride=SUBLANES), :])
    y = jnp.concatenate(parts, axis=1)
    y_ref[...] = x_ref[...] + gate_ref[...] * _rms(y, gn_ref[...])


def _combine_call(pos, route, x, gn, gate, out2d, tm):
    bm, r, d = x.shape
    nt = r // tm
    return pl.pallas_call(
        functools.partial(_combine_kernel, tm=tm),
        grid=(bm, nt),
        in_specs=[pl.BlockSpec((None, 1, 2 * tm), lambda b, i: (b * nt + i, 0, 0), memory_space=pltpu.SMEM),
                  _rows(tm, LANES), _rows(tm, d), _const(gn), _mod(gate, tm),
                  pl.BlockSpec(memory_space=pl.ANY)],
        out_specs=_rows(tm, d),
        out_shape=jax.ShapeDtypeStruct((bm, r, d), F32),
        scratch_shapes=[pltpu.VMEM((2 * tm * SUBLANES, LANES), F32), pltpu.SemaphoreType.DMA(())],
        compiler_params=_params("arbitrary", "arbitrary"),
        name="moe_combine",
    )(pos, route, x, gn, gate, out2d)


def _moe_sparse(x, g, sc, sh, wr, br, wg, wu, wd, gn, gate, tm):
    bm, r, d = x.shape
    m = bm * r
    tmx = MOE_TILE_ROWS
    stride = m + tmx
    n_tiles = 2 * m // tmx + N_EXPERTS
    h2d, route, cnt = _route_call(x, g, sc, sh, wr, br, tm, stride)
    cnt = cnt[0, :N_EXPERTS].astype(jnp.int32)
    pos = route[..., :2].astype(jnp.int32).reshape(m // tm, tm, 2).transpose(0, 2, 1).reshape(m // tm, 1, 2 * tm)
    xs = _dispatch_call(cnt, pos, h2d, tm, stride, tmx)
    per = (cnt + tmx - 1) // tmx
    ends = jnp.cumsum(per)
    j = jnp.minimum(jnp.arange(n_tiles, dtype=jnp.int32), ends[-1] - 1)
    exp = jnp.sum((j[:, None] >= ends[None, :]).astype(jnp.int32), axis=1)
    blk = exp * (stride // tmx) + j - (ends - per)[exp]
    out2d = _expert_call(blk, exp, ends[-1:], xs, wg, wu, wd, tmx, wg.shape[2] // 2)
    return _combine_call(pos, route, x, gn, gate, out2d, tm)


def _router_kernel(x_ref, g_ref, sc_ref, sh_ref, wr_ref, br_ref, h_ref, comb_ref):
    h = _rms(x_ref[...], g_ref[...]) * (1.0 + sc_ref[...]) + sh_ref[...]
    h_ref[...] = h.astype(h_ref.dtype)
    logits = _mm(h, wr_ref[...]) + br_ref[...]
    lane, i1, i2, g1, g2 = _top2(logits)
    comb_ref[...] = jnp.where(lane == i1, g1, 0.0) + jnp.where(lane == i2, g2, 0.0)


def _router_call(x, g, sc, sh, wr, br, h_dtype, tm):
    bm, r, d = x.shape
    return pl.pallas_call(
        _router_kernel,
        grid=(bm, r // tm),
        in_specs=[_rows(tm, d), _const(g), _mod(sc, tm), _mod(sh, tm), _const(wr), _const(br)],
        out_specs=[_rows(tm, d), _rows(tm, LANES)],
        out_shape=[jax.ShapeDtypeStruct((bm, r, d), h_dtype),
                   jax.ShapeDtypeStruct((bm, r, LANES), F32)],
        compiler_params=_params("arbitrary", "arbitrary"),
        name="moe_router",
    )(x, g, sc, sh, wr, br)


def _moe_kernel(h_ref, comb_ref, x_ref, wg_ref, wu_ref, wd_ref, gn_ref, gate_ref, o_ref, acc_scr):
    e = pl.program_id(2)
    f = pl.program_id(3)

    @pl.when((e == 0) & (f == 0))
    def _():
        acc_scr[...] = jnp.zeros_like(acc_scr)

    comb = comb_ref[...]
    lane = lax.broadcasted_iota(jnp.int32, comb.shape, 1)
    cw = jnp.sum(jnp.where(lane == e, comb, 0.0), axis=-1, keepdims=True)
    h = h_ref[...]
    he = _silu(_mm(h, wg_ref[...], False)) * _mm(h, wu_ref[...], False)
    acc_scr[...] += cw * _mm(he, wd_ref[...], False)

    @pl.when((e == pl.num_programs(2) - 1) & (f == pl.num_programs(3) - 1))
    def _():
        o_ref[...] = x_ref[...] + gate_ref[...] * _rms(acc_scr[...], gn_ref[...])


def _moe_call(h, comb, x, wg, wu, wd, gn, gate, tm, tf):
    bm, r, d = x.shape
    ne, _, ff = wg.shape
    return pl.pallas_call(
        _moe_kernel,
        grid=(bm, r // tm, ne, ff // tf),
        in_specs=[_rows(tm, d), _rows(tm, LANES), _rows(tm, d),
                  pl.BlockSpec((None, d, tf), lambda b, i, e, f: (e, 0, f)),
                  pl.BlockSpec((None, d, tf), lambda b, i, e, f: (e, 0, f)),
                  pl.BlockSpec((None, tf, d), lambda b, i, e, f: (e, f, 0)),
                  _const(gn), _mod(gate, tm)],
        out_specs=_rows(tm, d),
        out_shape=jax.ShapeDtypeStruct((bm, r, d), F32),
        scratch_shapes=[pltpu.VMEM((tm, d), F32)],
        compiler_params=_params("arbitrary", "arbitrary", "arbitrary", "arbitrary"),
        name="moe_experts",
    )(h, comb, x, wg, wu, wd, gn, gate)


def _cache_append_kernel(ck_hbm, cv_hbm, kn_hbm, vn_hbm, ok_hbm, ov_hbm, sems):
    n, cl = ck_hbm.shape[0], ck_hbm.shape[1]
    copies = []
    for j, (c_hbm, n_hbm, o_hbm) in enumerate(((ck_hbm, kn_hbm, ok_hbm), (cv_hbm, vn_hbm, ov_hbm))):
        for b in range(n):
            copies.append(pltpu.make_async_copy(c_hbm.at[b, pl.ds(1, cl - 1)],
                                                o_hbm.at[b, pl.ds(0, cl - 1)], sems.at[2 * j]))
        copies.append(pltpu.make_async_copy(n_hbm, o_hbm.at[:, cl - 1], sems.at[2 * j + 1]))
    for cp in copies:
        cp.start()
    for cp in copies:
        cp.wait()


def _cache_append_call(cache_k, cache_v, k_new, v_new):
    hbm = pl.BlockSpec(memory_space=pl.ANY)
    return pl.pallas_call(
        _cache_append_kernel,
        in_specs=[hbm] * 4,
        out_specs=[hbm] * 2,
        out_shape=[jax.ShapeDtypeStruct(cache_k.shape, cache_k.dtype),
                   jax.ShapeDtypeStruct(cache_v.shape, cache_v.dtype)],
        scratch_shapes=[pltpu.SemaphoreType.DMA((4,))],
        name="cache_append",
    )(cache_k, cache_v, k_new, v_new)


def _ssm_params(lam_re, lam_im, log_dt, b_re, b_im, c_re, c_im, wdtype):
    dt = jnp.exp(log_dt)[:, None]
    mag = jnp.exp(lam_re * dt)
    a_re, a_im = mag * jnp.cos(lam_im * dt), mag * jnp.sin(lam_im * dt)
    den = lam_re * lam_re + lam_im * lam_im
    f_re = ((a_re - 1.0) * lam_re + a_im * lam_im) / den
    f_im = (a_im * lam_re - (a_re - 1.0) * lam_im) / den
    bb_re = f_re[..., None] * b_re - f_im[..., None] * b_im
    bb_im = f_re[..., None] * b_im + f_im[..., None] * b_re
    eye = jnp.eye(GROUPS_PER_SLAB, dtype=F32)

    def in_map(bb):
        t = bb.reshape(SLABS, GROUPS_PER_SLAB, SSM_STATE, SSM_CH)
        w = jnp.einsum('sgpc,gh->sgchp', t, eye)
        return w.reshape(SLABS, LANES, SLAB_STATES).astype(wdtype)

    def out_map(cc):
        t = cc.reshape(SLABS, GROUPS_PER_SLAB, SSM_CH, SSM_STATE)
        w = jnp.einsum('sgcp,gh->sgphc', t, eye)
        return w.reshape(SLABS, SLAB_STATES, LANES).astype(wdtype)

    return a_re, a_im, in_map(bb_re), in_map(bb_im), out_map(c_re), out_map(c_im)


def _rope_tables(pos):
    half = ROT_DIM // 2
    inv = ROPE_THETA ** (-jnp.arange(half, dtype=F32) / half)
    ang = pos.astype(F32)[:, None] * inv[None, :]
    cos, sin = jnp.cos(ang), jnp.sin(ang)
    n = pos.shape[0]
    pad = jnp.zeros((n, HEAD_DIM - ROT_DIM), F32)
    zero = jnp.zeros((n, half), F32)
    cos_h = jnp.concatenate([cos, cos, pad + 1.0], axis=1)
    lo_h = jnp.concatenate([-sin, zero, pad], axis=1)
    hi_h = jnp.concatenate([zero, sin, pad], axis=1)
    rep = LANES // HEAD_DIM
    return jnp.tile(cos_h, (1, rep)), jnp.tile(lo_h, (1, rep)), jnp.tile(hi_h, (1, rep))


def _trunk(x, mods, pos, wts, prompt, ssm0=None, cache=None):
    bm, r, d = x.shape
    tm = 512 if prompt else r
    wd = BF16 if prompt else F32
    cast = lambda w: w.astype(wd)
    norm_g = wts['norm_g'].reshape(DEPTH, 4, 1, d)

    sh1, sc1, g1, sh2, sc2, g2 = mods[0]
    u = _inproj_call(x, norm_g[0, 0], sc1, sh1, cast(wts['ssm_w_in'][0]), tm)
    a_re, a_im, wbr, wbi, cr, ci = _ssm_params(
        wts['ssm_lam_re'][0], wts['ssm_lam_im'][0], wts['ssm_log_dt'][0], wts['ssm_b_re'][0],
        wts['ssm_b_im'][0], wts['ssm_c_re'][0], wts['ssm_c_im'][0], wd)
    dvec = wts['ssm_d'][0].reshape(1, d)
    if prompt:
        z, fr, fi = _ssm_prompt_call(u, wbr, wbi, a_re.reshape(STATE_TILES, LANES),
                                     a_im.reshape(STATE_TILES, LANES), cr, ci, dvec, 256)
    else:
        z, fr, fi = _ssm_sample_call(u[0], ssm0[0].reshape(r, SSM_FLAT), ssm0[1].reshape(r, SSM_FLAT),
                                     wbr, wbi, a_re.reshape(1, SSM_FLAT), a_im.reshape(1, SSM_FLAT),
                                     cr, ci, dvec)
        z = z[None]
    fr = fr.reshape(-1, SSM_GROUPS, SSM_STATE)[None]
    fi = fi.reshape(-1, SSM_GROUPS, SSM_STATE)[None]
    x = _s5out_call(z, x, cast(wts['ssm_w_glu'][0]), cast(wts['ssm_w_out'][0]), norm_g[0, 1], g1, tm)
    tf = wts['ffn_w_gate'].shape[2] // 2
    x = _ffn_call(x, norm_g[0, 2], sc2, sh2, cast(wts['ffn_w_gate'][0]), cast(wts['ffn_w_up'][0]),
                  cast(wts['ffn_w_down'][0]), norm_g[0, 3], g2, tm, tf)

    sh1, sc1, g1, sh2, sc2, g2 = mods[1]
    cos, slo, shi = _rope_tables(pos)
    qkv = _qkv_call(x, wts['kv_norm_g'].reshape(1, d), norm_g[1, 0], sc1, sh1, cast(wts['w_kv']),
                    cast(wts['attn_w_q'][0]), cos, slo, shi, tm, prompt)
    wo = cast(wts['attn_w_o'][0])
    wr = cast(jnp.zeros((d, LANES), F32).at[:, :N_EXPERTS].set(wts['moe_w_router'][0]))
    br = jnp.full((1, LANES), NEG_INF, F32).at[0, :N_EXPERTS].set(wts['moe_b_router'][0])
    wg, wu, wdn = (wts[n][0].astype(BF16) for n in ('moe_w_gate', 'moe_w_up', 'moe_w_down'))
    if prompt:
        k, v, kd, vd, qd = qkv
        os_, ls_ = [], []
        for g, (_, dil) in enumerate(DILATION_GROUPS):
            o, lse = _band_call(qd[g], kd[g], vd[g], min(4, r // dil // SPAN))
            os_.append(o)
            ls_.append(lse)
        x = _attn_out_call(os_, ls_, x, wo, norm_g[1, 1], g1, tm)
        x = _moe_sparse(x, norm_g[1, 2], sc2, sh2, wr, br, wg, wu, wdn, norm_g[1, 3], g2, tm)
    else:
        k, v, q = qkv
        o = _sample_attn_call(q[0].reshape(r, N_DIL_GROUPS, N_KV_HEADS, HEAD_DIM),
                              k[0].reshape(r, N_KV_HEADS, HEAD_DIM), v[0].reshape(r, N_KV_HEADS, HEAD_DIM),
                              cache[0], cache[1], 4)
        x = _proj_res_call(o.reshape(1, r, KV_DIM), x, wo, norm_g[1, 1], g1, tm)
        h, comb = _router_call(x, norm_g[1, 2], sc2, sh2, wr, br, wd, tm)
        x = _moe_call(h, comb, x, wg, wu, wdn, norm_g[1, 3], g2, tm, wg.shape[2] // 4)
    return x, fr, fi, k, v


def kernel(x_prompt, x_sample, state_ssm_re, state_ssm_im, cache_k, cache_v, c_prompt, c_sample, ada_w, ada_b, norm_g, ssm_w_in, ssm_lam_re, ssm_lam_im, ssm_log_dt, ssm_b_re, ssm_b_im, ssm_c_re, ssm_c_im, ssm_d, ssm_w_glu, ssm_w_out, kv_norm_g, w_kv, attn_w_q, attn_w_o, ffn_w_gate, ffn_w_up, ffn_w_down, moe_w_router, moe_b_router, moe_w_gate, moe_w_up, moe_w_down):
    wts = dict(norm_g=norm_g, ssm_w_in=ssm_w_in, ssm_lam_re=ssm_lam_re, ssm_lam_im=ssm_lam_im,
               ssm_log_dt=ssm_log_dt, ssm_b_re=ssm_b_re, ssm_b_im=ssm_b_im, ssm_c_re=ssm_c_re,
               ssm_c_im=ssm_c_im, ssm_d=ssm_d, ssm_w_glu=ssm_w_glu, ssm_w_out=ssm_w_out,
               kv_norm_g=kv_norm_g, w_kv=w_kv, attn_w_q=attn_w_q, attn_w_o=attn_w_o,
               ffn_w_gate=ffn_w_gate, ffn_w_up=ffn_w_up, ffn_w_down=ffn_w_down,
               moe_w_router=moe_w_router, moe_b_router=moe_b_router, moe_w_gate=moe_w_gate,
               moe_w_up=moe_w_up, moe_w_down=moe_w_down)
    nb, seq, d = x_prompt.shape
    ns = x_sample.shape[0]
    assert x_sample.shape[1] == 1 and ns % SUBLANES == 0

    pad = (-(ns + nb)) % SUBLANES
    c_all = jnp.concatenate([c_sample, c_prompt, jnp.zeros((pad, d), F32)], axis=0)
    mod_all = _ada_call(c_all, ada_w, ada_b)
    mods_s = [[mod_all[l, k, :ns][None] for k in range(6)] for l in range(DEPTH)]
    mods_p = [[mod_all[l, k, ns:ns + nb][:, None, :] for k in range(6)] for l in range(DEPTH)]

    pos_p = jnp.arange(seq, dtype=jnp.int32)
    pos_s = jnp.full((ns,), PAST_LEN, dtype=jnp.int32)
    y_p, sr_p, si_p, k_p, v_p = _trunk(x_prompt, mods_p, pos_p, wts, True)
    xs = x_sample.reshape(1, ns, d)
    y_s, sr_s, si_s, k_s, v_s = _trunk(xs, mods_s, pos_s, wts, False,
                                       ssm0=(state_ssm_re[0], state_ssm_im[0]), cache=(cache_k, cache_v))

    keep = min(MAX_WINDOW, seq)
    k_rows_p = k_p[:, seq - keep:].reshape(nb, keep, N_KV_HEADS, HEAD_DIM)
    v_rows_p = v_p[:, seq - keep:].reshape(nb, keep, N_KV_HEADS, HEAD_DIM)
    k_rows_s, v_rows_s = _cache_append_call(cache_k, cache_v, k_s[0].reshape(ns, N_KV_HEADS, HEAD_DIM),
                                            v_s[0].reshape(ns, N_KV_HEADS, HEAD_DIM))
    return (y_p, y_s.reshape(ns, 1, d), sr_p, si_p, sr_s, si_s, k_rows_p, v_rows_p, k_rows_s, v_rows_s)
```

```python
import functools
import math

import jax
import jax.numpy as jnp
from jax import lax
from jax.experimental import pallas as pl
from jax.experimental.pallas import tpu as pltpu

F32 = jnp.float32
BF16 = jnp.bfloat16
HIGHEST = lax.Precision.HIGHEST

D_MODEL = 1024
DEPTH = 2
PAST_LEN = 16384
SSM_CH = 16
SSM_GROUPS = D_MODEL // SSM_CH
SSM_STATE = 64
SSM_FLAT = SSM_GROUPS * SSM_STATE
HEAD_DIM = 64
N_KV_HEADS = D_MODEL // 128
KV_DIM = N_KV_HEADS * HEAD_DIM
DILATION_GROUPS = ((128, 1), (512, 4), (2048, 16))
N_DIL_GROUPS = len(DILATION_GROUPS)
SPAN = 128
MAX_WINDOW = 2048
ROT_DIM = HEAD_DIM // 4
ROPE_THETA = 500000.0
N_EXPERTS = 8
RMS_EPS = 1e-6
NEG_INF = -1e30

LANES = 128
SUBLANES = 8
SLABS = D_MODEL // LANES
GROUPS_PER_SLAB = LANES // SSM_CH
SLAB_STATES = GROUPS_PER_SLAB * SSM_STATE
STATE_TILES = SSM_FLAT // LANES
MOE_TILE_ROWS = 512
VMEM_LIMIT = 56 * 1024 * 1024


def _params(*sem):
    return pltpu.CompilerParams(dimension_semantics=sem, vmem_limit_bytes=VMEM_LIMIT)


def _mm(a, w, precise=True):
    if w.dtype == BF16 or not precise:
        return jnp.dot(a.astype(BF16), w.astype(BF16), preferred_element_type=F32)
    return jnp.dot(a.astype(F32), w, preferred_element_type=F32, precision=HIGHEST)


def _rms(x, g):
    return x * lax.rsqrt(jnp.mean(x * x, axis=-1, keepdims=True) + RMS_EPS) * g


def _sigmoid(x):
    return 1.0 / (1.0 + jnp.exp(-x))


def _silu(x):
    return x * _sigmoid(x)


def _gelu_tanh(x):
    return 0.5 * x * (1.0 + jnp.tanh(math.sqrt(2.0 / math.pi) * (x + 0.044715 * (x * x * x))))


def _rows(tm, n):
    return pl.BlockSpec((None, tm, n), lambda b, i, *_: (b, i, 0))


def _mod(arr, tm):
    if arr.shape[1] == 1:
        return pl.BlockSpec((None, 1, arr.shape[2]), lambda b, i, *_: (b, 0, 0))
    return pl.BlockSpec((None, tm, arr.shape[2]), lambda b, i, *_: (b, i, 0))


def _const(arr):
    nd = arr.ndim
    return pl.BlockSpec(arr.shape, lambda *_: (0,) * nd)


def _ada_kernel(c_ref, w_ref, b_ref, o_ref):
    o_ref[...] = _mm(_silu(c_ref[...]), w_ref[...]) + b_ref[...]


def _ada_call(c_all, ada_w, ada_b):
    rows = c_all.shape[0]
    b4 = ada_b.reshape(DEPTH, 6, 1, D_MODEL)
    return pl.pallas_call(
        _ada_kernel,
        grid=(DEPTH, 6),
        in_specs=[pl.BlockSpec((rows, D_MODEL), lambda l, k: (0, 0)),
                  pl.BlockSpec((None, D_MODEL, D_MODEL), lambda l, k: (l, 0, k)),
                  pl.BlockSpec((None, None, 1, D_MODEL), lambda l, k: (l, k, 0, 0))],
        out_specs=pl.BlockSpec((None, None, rows, D_MODEL), lambda l, k: (l, k, 0, 0)),
        out_shape=jax.ShapeDtypeStruct((DEPTH, 6, rows, D_MODEL), F32),
        compiler_params=_params("arbitrary", "arbitrary"),
        name="ada_modulation",
    )(c_all, ada_w, b4)


def _inproj_kernel(x_ref, g_ref, sc_ref, sh_ref, w_ref, u_ref):
    h = _rms(x_ref[...], g_ref[...]) * (1.0 + sc_ref[...]) + sh_ref[...]
    u_ref[...] = _mm(h, w_ref[...])


def _inproj_call(x, g, sc, sh, w, tm):
    bm, r, d = x.shape
    return pl.pallas_call(
        _inproj_kernel,
        grid=(bm, r // tm),
        in_specs=[_rows(tm, d), _const(g), _mod(sc, tm), _mod(sh, tm), _const(w)],
        out_specs=_rows(tm, w.shape[1]),
        out_shape=jax.ShapeDtypeStruct((bm, r, w.shape[1]), F32),
        compiler_params=_params("arbitrary", "arbitrary"),
        name="s5_in_proj",
    )(x, g, sc, sh, w)


def _ssm_prompt_kernel(u_ref, wbr_ref, wbi_ref, ar_ref, ai_ref, cr_ref, ci_ref, d_ref,
                       z_ref, fr_ref, fi_ref, br_scr, bi_scr, sr_scr, si_scr, *, t_chunk, nseq):
    tb_n = t_chunk // SUBLANES
    tiles = SLAB_STATES // LANES

    @pl.when(pl.program_id(0) == 0)
    def _():
        sr_scr[...] = jnp.zeros_like(sr_scr)
        si_scr[...] = jnp.zeros_like(si_scr)

    for q in range(nseq):
        ub = u_ref[q].astype(BF16)
        for s in range(SLABS):
            us = ub[:, s * LANES:(s + 1) * LANES]
            bre = jnp.dot(us, wbr_ref[s], preferred_element_type=F32)
            bim = jnp.dot(us, wbi_ref[s], preferred_element_type=F32)
            for jj in range(tiles):
                rows = slice((s * tiles + jj) * SUBLANES, (s * tiles + jj + 1) * SUBLANES)
                cols = slice(jj * LANES, (jj + 1) * LANES)
                br_scr[q, :, rows, :] = bre[:, cols].reshape(tb_n, SUBLANES, LANES)
                bi_scr[q, :, rows, :] = bim[:, cols].reshape(tb_n, SUBLANES, LANES)

    ar = ar_ref[...]
    ai = ai_ref[...]

    def body(tb, carry):
        carry = list(carry)
        for r in range(SUBLANES):
            step = pl.ds(r, STATE_TILES, stride=SUBLANES)
            for q in range(nseq):
                sr, si = carry[q]
                nsr = ar * sr - ai * si + br_scr[q, tb, step, :]
                nsi = ar * si + ai * sr + bi_scr[q, tb, step, :]
                carry[q] = (nsr, nsi)
                br_scr[q, tb, step, :] = nsr
                bi_scr[q, tb, step, :] = nsi
        return tuple(carry)

    final = lax.fori_loop(0, tb_n, body, tuple((sr_scr[q], si_scr[q]) for q in range(nseq)))
    for q in range(nseq):
        sr_scr[q], si_scr[q] = final[q]
        fr_ref[q], fi_ref[q] = final[q]

    for q in range(nseq):
        for s in range(SLABS):
            y = None
            for jj in range(tiles):
                rows = slice((s * tiles + jj) * SUBLANES, (s * tiles + jj + 1) * SUBLANES)
                cols = slice(jj * LANES, (jj + 1) * LANES)
                sre = br_scr[q, :, rows, :].reshape(t_chunk, LANES).astype(BF16)
                sim = bi_scr[q, :, rows, :].reshape(t_chunk, LANES).astype(BF16)
                part = (jnp.dot(sre, cr_ref[s, cols, :], preferred_element_type=F32)
                        - jnp.dot(sim, ci_ref[s, cols, :], preferred_element_type=F32))
                y = part if y is None else y + part
            sl = slice(s * LANES, (s + 1) * LANES)
            z_ref[q, :, sl] = _gelu_tanh(y + d_ref[:, sl] * u_ref[q, :, sl])


def _ssm_prompt_call(u, wbr, wbi, ar, ai, cr, ci, d, t_chunk):
    b, l, dm = u.shape
    kern = functools.partial(_ssm_prompt_kernel, t_chunk=t_chunk, nseq=b)
    seq_spec = pl.BlockSpec((b, t_chunk, dm), lambda c: (0, c, 0))
    state_spec = pl.BlockSpec((b, STATE_TILES, LANES), lambda c: (0, 0, 0))
    scr = pltpu.VMEM((b, t_chunk // SUBLANES, STATE_TILES * SUBLANES, LANES), F32)
    carry = pltpu.VMEM((b, STATE_TILES, LANES), F32)
    return pl.pallas_call(
        kern,
        grid=(l // t_chunk,),
        in_specs=[seq_spec, _const(wbr), _const(wbi), _const(ar), _const(ai),
                  _const(cr), _const(ci), _const(d)],
        out_specs=[seq_spec, state_spec, state_spec],
        out_shape=[jax.ShapeDtypeStruct((b, l, dm), F32),
                   jax.ShapeDtypeStruct((b, STATE_TILES, LANES), F32),
                   jax.ShapeDtypeStruct((b, STATE_TILES, LANES), F32)],
        scratch_shapes=[scr, scr, carry, carry],
        compiler_params=_params("arbitrary"),
        name="s5_scan_prompt",
    )(u, wbr, wbi, ar, ai, cr, ci, d)


def _ssm_sample_kernel(u_ref, x0r_ref, x0i_ref, wbr_ref, wbi_ref, ar_ref, ai_ref, cr_ref, ci_ref,
                       d_ref, z_ref, fr_ref, fi_ref):
    for s in range(SLABS):
        sl = slice(s * LANES, (s + 1) * LANES)
        st = slice(s * SLAB_STATES, (s + 1) * SLAB_STATES)
        us = u_ref[:, sl]
        a_r, a_i = ar_ref[:, st], ai_ref[:, st]
        x0r, x0i = x0r_ref[:, st], x0i_ref[:, st]
        sr = _mm(us, wbr_ref[s]) + a_r * x0r - a_i * x0i
        si = _mm(us, wbi_ref[s]) + a_r * x0i + a_i * x0r
        fr_ref[:, st] = sr
        fi_ref[:, st] = si
        y = _mm(sr, cr_ref[s]) - _mm(si, ci_ref[s])
        z_ref[:, sl] = _gelu_tanh(y + d_ref[:, sl] * us)


def _ssm_sample_call(u, x0r, x0i, wbr, wbi, ar, ai, cr, ci, d):
    n = u.shape[0]
    args = (u, x0r, x0i, wbr, wbi, ar, ai, cr, ci, d)
    return pl.pallas_call(
        _ssm_sample_kernel,
        grid=(1,),
        in_specs=[_const(a) for a in args],
        out_specs=[pl.BlockSpec((n, D_MODEL), lambda i: (0, 0)),
                   pl.BlockSpec((n, SSM_FLAT), lambda i: (0, 0)),
                   pl.BlockSpec((n, SSM_FLAT), lambda i: (0, 0))],
        out_shape=[jax.ShapeDtypeStruct((n, D_MODEL), F32),
                   jax.ShapeDtypeStruct((n, SSM_FLAT), F32),
                   jax.ShapeDtypeStruct((n, SSM_FLAT), F32)],
        compiler_params=_params("arbitrary"),
        name="s5_step_sample",
    )(*args)


def _s5out_kernel(z_ref, x_ref, wg_ref, wo_ref, gn_ref, gate_ref, o_ref):
    z = z_ref[...]
    gl = z * _sigmoid(_mm(z, wg_ref[...]))
    out = _mm(gl, wo_ref[...])
    o_ref[...] = x_ref[...] + gate_ref[...] * _rms(out, gn_ref[...])


def _s5out_call(z, x, wg, wo, gn, gate, tm):
    bm, r, d = x.shape
    return pl.pallas_call(
        _s5out_kernel,
        grid=(bm, r // tm),
        in_specs=[_rows(tm, d), _rows(tm, d), _const(wg), _const(wo), _const(gn), _mod(gate, tm)],
        out_specs=_rows(tm, d),
        out_shape=jax.ShapeDtypeStruct((bm, r, d), F32),
        compiler_params=_params("arbitrary", "arbitrary"),
        name="s5_glu_out_proj",
    )(z, x, wg, wo, gn, gate)


def _ffn_kernel(x_ref, g_ref, sc_ref, sh_ref, wg_ref, wu_ref, wd_ref, gn_ref, gate_ref, o_ref,
                h_scr, acc_scr):
    f = pl.program_id(2)

    @pl.when(f == 0)
    def _():
        h = _rms(x_ref[...], g_ref[...]) * (1.0 + sc_ref[...]) + sh_ref[...]
        h_scr[...] = h.astype(h_scr.dtype)
        acc_scr[...] = jnp.zeros_like(acc_scr)

    h = h_scr[...]
    he = _silu(_mm(h, wg_ref[...])) * _mm(h, wu_ref[...])
    acc_scr[...] += _mm(he, wd_ref[...])

    @pl.when(f == pl.num_programs(2) - 1)
    def _():
        o_ref[...] = x_ref[...] + gate_ref[...] * _rms(acc_scr[...], gn_ref[...])


def _ffn_call(x, g, sc, sh, wg, wu, wd, gn, gate, tm, tf):
    bm, r, d = x.shape
    ff = wg.shape[1]
    return pl.pallas_call(
        _ffn_kernel,
        grid=(bm, r // tm, ff // tf),
        in_specs=[_rows(tm, d), _const(g), _mod(sc, tm), _mod(sh, tm),
                  pl.BlockSpec((d, tf), lambda b, i, f: (0, f)),
                  pl.BlockSpec((d, tf), lambda b, i, f: (0, f)),
                  pl.BlockSpec((tf, d), lambda b, i, f: (f, 0)),
                  _const(gn), _mod(gate, tm)],
        out_specs=_rows(tm, d),
        out_shape=jax.ShapeDtypeStruct((bm, r, d), F32),
        scratch_shapes=[pltpu.VMEM((tm, d), wg.dtype), pltpu.VMEM((tm, d), F32)],
        compiler_params=_params("arbitrary", "arbitrary", "arbitrary"),
        name="dense_swiglu",
    )(x, g, sc, sh, wg, wu, wd, gn, gate)


def _rope(t, cos, sin_lo, sin_hi):
    half = ROT_DIM // 2
    outs = []
    for s in range(t.shape[1] // LANES):
        ts = t[:, s * LANES:(s + 1) * LANES]
        outs.append(ts * cos + pltpu.roll(ts, LANES - half, 1) * sin_lo + pltpu.roll(ts, half, 1) * sin_hi)
    return jnp.concatenate(outs, axis=1)


def _qkv_values(x_ref, gkv_ref, g_ref, sc_ref, sh_ref, wkv_ref, wq_ref, cos_ref, slo_ref, shi_ref):
    x = x_ref[...]
    xn = x * lax.rsqrt(jnp.mean(x * x, axis=-1, keepdims=True) + RMS_EPS)
    kv = _mm(xn * gkv_ref[...], wkv_ref[...])
    q = _mm((xn * g_ref[...]) * (1.0 + sc_ref[...]) + sh_ref[...], wq_ref[...])
    cos, slo, shi = cos_ref[...], slo_ref[...], shi_ref[...]
    k = _rope(kv[:, :KV_DIM], cos, slo, shi)
    v = kv[:, KV_DIM:]
    q = _rope(q, cos, slo, shi) * (HEAD_DIM ** -0.5)
    return k, v, q


def _qkv_sample_kernel(*refs):
    k_ref, v_ref, q_ref = refs[10:]
    k, v, q = _qkv_values(*refs[:10])
    k_ref[...] = k
    v_ref[...] = v
    q_ref[...] = q


def _store_residues(dst_ref, val, scr, dil):
    if dil == 1:
        dst_ref[0] = val.astype(dst_ref.dtype)
        return
    slabs = val.shape[1] // LANES
    for s in range(slabs):
        scr[s] = val[:, s * LANES:(s + 1) * LANES]
    n = val.shape[0] // dil
    for r in range(dil):
        rows = [scr[s, pl.ds(r, n, stride=dil), :] for s in range(slabs)]
        dst_ref[r] = jnp.concatenate(rows, axis=1).astype(dst_ref.dtype)


def _qkv_prompt_kernel(*refs):
    k_ref, v_ref = refs[10:12]
    kd_refs = refs[12:12 + N_DIL_GROUPS]
    vd_refs = refs[12 + N_DIL_GROUPS:12 + 2 * N_DIL_GROUPS]
    qd_refs = refs[12 + 2 * N_DIL_GROUPS:12 + 3 * N_DIL_GROUPS]
    scr = refs[12 + 3 * N_DIL_GROUPS]
    k, v, q = _qkv_values(*refs[:10])
    k_ref[...] = k
    v_ref[...] = v
    for g, (_, dil) in enumerate(DILATION_GROUPS):
        _store_residues(kd_refs[g], k, scr, dil)
        _store_residues(vd_refs[g], v, scr, dil)
        _store_residues(qd_refs[g], q[:, g * KV_DIM:(g + 1) * KV_DIM], scr, dil)


def _qkv_call(x, gkv, g, sc, sh, wkv, wq, cos, slo, shi, tm, prompt):
    bm, r, d = x.shape
    nq = wq.shape[1]
    tab = pl.BlockSpec((tm, LANES), lambda b, i: (i, 0))
    in_specs = [_rows(tm, d), _const(gkv), _const(g), _mod(sc, tm), _mod(sh, tm),
                _const(wkv), _const(wq), tab, tab, tab]
    out_specs = [_rows(tm, KV_DIM), _rows(tm, KV_DIM)]
    out_shape = [jax.ShapeDtypeStruct((bm, r, KV_DIM), F32), jax.ShapeDtypeStruct((bm, r, KV_DIM), F32)]
    if not prompt:
        return pl.pallas_call(
            _qkv_sample_kernel,
            grid=(bm, r // tm),
            in_specs=in_specs,
            out_specs=out_specs + [_rows(tm, nq)],
            out_shape=out_shape + [jax.ShapeDtypeStruct((bm, r, nq), F32)],
            compiler_params=_params("arbitrary", "arbitrary"),
            name="qkv_proj_rope_sample",
        )(x, gkv, g, sc, sh, wkv, wq, cos, slo, shi)
    for _ in range(3):
        for _, dil in DILATION_GROUPS:
            out_specs.append(pl.BlockSpec((None, dil, tm // dil, KV_DIM), lambda b, i: (b, 0, i, 0)))
            out_shape.append(jax.ShapeDtypeStruct((bm, dil, r // dil, KV_DIM), BF16))
    outs = pl.pallas_call(
        _qkv_prompt_kernel,
        grid=(bm, r // tm),
        in_specs=in_specs,
        out_specs=out_specs,
        out_shape=out_shape,
        scratch_shapes=[pltpu.VMEM((KV_DIM // LANES, tm, LANES), F32)],
        compiler_params=_params("arbitrary", "arbitrary"),
        name="qkv_proj_rope_prompt",
    )(x, gkv, g, sc, sh, wkv, wq, cos, slo, shi)
    n = N_DIL_GROUPS
    return outs[0], outs[1], outs[2:2 + n], outs[2 + n:2 + 2 * n], outs[2 + 2 * n:2 + 3 * n]


def _band_kernel(q_ref, kp_ref, kc_ref, vp_ref, vc_ref, o_ref, l_ref, *, nb):
    qi = lax.broadcasted_iota(jnp.int32, (SPAN, 2 * SPAN), 0)
    kj = lax.broadcasted_iota(jnp.int32, (SPAN, 2 * SPAN), 1)
    band = (kj >= qi) & (kj <= qi + SPAN)
    band_first = band & (kj >= jnp.where(pl.program_id(2) == 0, SPAN, 0))
    lane = lax.broadcasted_iota(jnp.int32, (SPAN, LANES), 1)
    head0 = lane < HEAD_DIM
    for i in range(nb):
        cur = slice(i * SPAN, (i + 1) * SPAN)
        prev = slice((i - 1) * SPAN, i * SPAN)
        for hp in range(KV_DIM // LANES):
            cols = slice(hp * LANES, (hp + 1) * LANES)
            q2 = q_ref[cur, cols]
            if i == 0:
                k2 = jnp.concatenate([kp_ref[:, cols], kc_ref[cur, cols]], axis=0)
                v2 = jnp.concatenate([vp_ref[:, cols], vc_ref[cur, cols]], axis=0)
                mask = band_first
            else:
                k2 = jnp.concatenate([kc_ref[prev, cols], kc_ref[cur, cols]], axis=0)
                v2 = jnp.concatenate([vc_ref[prev, cols], vc_ref[cur, cols]], axis=0)
                mask = band
            o_pair = None
            l_pair = None
            for hh in range(2):
                sel = head0 if hh == 0 else jnp.logical_not(head0)
                qm = jnp.where(sel, q2, jnp.zeros_like(q2))
                s = lax.dot_general(qm, k2, (((1,), (1,)), ((), ())), preferred_element_type=F32)
                s = jnp.where(mask, s, NEG_INF)
                m = jnp.max(s, axis=-1, keepdims=True)
                p = jnp.exp(s - m)
                den = jnp.sum(p, axis=-1, keepdims=True)
                o = jnp.dot(p.astype(BF16), v2, preferred_element_type=F32) / den
                lse = jnp.broadcast_to(m + jnp.log(den), (SPAN, LANES))
                o_pair = o if o_pair is None else jnp.where(head0, o_pair, o)
                l_pair = lse if l_pair is None else jnp.where(head0, l_pair, lse)
            o_ref[cur, cols] = o_pair
            l_ref[cur, cols] = l_pair


def _band_call(q, k, v, nb):
    b, dil, m, _ = k.shape
    tq = nb * SPAN
    cur = pl.BlockSpec((None, None, tq, KV_DIM), lambda bb, r, n: (bb, r, n, 0))
    prev = pl.BlockSpec((None, None, SPAN, KV_DIM), lambda bb, r, n: (bb, r, jnp.maximum(n * nb - 1, 0), 0))
    return pl.pallas_call(
        functools.partial(_band_kernel, nb=nb),
        grid=(b, dil, m // tq),
        in_specs=[cur, prev, cur, prev, cur],
        out_specs=[cur, cur],
        out_shape=[jax.ShapeDtypeStruct((b, dil, m, KV_DIM), F32),
                   jax.ShapeDtypeStruct((b, dil, m, KV_DIM), F32)],
        compiler_params=_params("arbitrary", "arbitrary", "arbitrary"),
        name=f"band_attention_dil{dil}",
    )(q, k, k, v, v)


def _sample_attn_kernel(q_ref, kn_ref, vn_ref, kc_ref, vc_ref, o_ref, ko_ref, vo_ref):
    cl = kc_ref.shape[-1]
    is_last = lax.broadcasted_iota(jnp.int32, (HEAD_DIM, cl), 1) == cl - 1
    for h in range(N_KV_HEADS):
        kh, vh = kc_ref[0, h], vc_ref[0, h]
        kn, vn = kn_ref[0, h], vn_ref[0, h]
        ko_ref[0, h] = jnp.where(is_last, kn, pltpu.roll(kh, cl - 1, 1))
        vo_ref[0, h] = jnp.where(is_last, vn, pltpu.roll(vh, cl - 1, 1))
        outs, lses = [], []
        for g, (win, dil) in enumerate(DILATION_GROUPS):
            q = q_ref[0, g, h]
            s = jnp.sum(kh[:, cl - win:] * q, axis=0, keepdims=True)
            if dil > 1:
                lane = lax.broadcasted_iota(jnp.int32, (1, win), 1)
                s = jnp.where((lane & (dil - 1)) == 0, s, NEG_INF)
            s_new = jnp.sum(kn * q, axis=0, keepdims=True)
            m = jnp.maximum(jnp.max(s, axis=-1, keepdims=True), s_new)
            p = jnp.exp(s - m)
            p_new = jnp.exp(s_new - m)
            den = jnp.sum(p, axis=-1, keepdims=True) + p_new
            outs.append((jnp.sum(vh[:, cl - win:] * p, axis=-1, keepdims=True) + p_new * vn) / den)
            lses.append(m + jnp.log(den))
        mx = jnp.maximum(jnp.maximum(lses[0], lses[1]), lses[2])
        es = [jnp.exp(l - mx) for l in lses]
        o_ref[0, h] = (es[0] * outs[0] + es[1] * outs[1] + es[2] * outs[2]) / (es[0] + es[1] + es[2])


def _sample_attn_call(q, k_new, v_new, cache_k, cache_v):
    n, nh, hd, cl = cache_k.shape
    col = pl.BlockSpec((1, nh, hd, 1), lambda i: (i, 0, 0, 0))
    cache = pl.BlockSpec((1, nh, hd, cl), lambda i: (i, 0, 0, 0))
    return pl.pallas_call(
        _sample_attn_kernel,
        grid=(n,),
        in_specs=[pl.BlockSpec((1, N_DIL_GROUPS, nh, hd, 1), lambda i: (i, 0, 0, 0, 0)), col, col,
                  cache, cache],
        out_specs=[col, cache, cache],
        out_shape=[jax.ShapeDtypeStruct((n, nh, hd, 1), F32),
                   jax.ShapeDtypeStruct(cache_k.shape, cache_k.dtype),
                   jax.ShapeDtypeStruct(cache_v.shape, cache_v.dtype)],
        compiler_params=_params("arbitrary"),
        name="window_attention_cache_append",
    )(q, k_new, v_new, cache_k, cache_v)


def _attn_out_kernel(o1_ref, o2_ref, o3_ref, l1_ref, l2_ref, l3_ref, x_ref, wo_ref, gn_ref, gate_ref,
                     y_ref, *scrs):
    vals = []
    for ref, scr, (_, dil) in zip((o1_ref, o2_ref, o3_ref, l1_ref, l2_ref, l3_ref), scrs,
                                  DILATION_GROUPS + DILATION_GROUPS):
        if dil == 1:
            vals.append(ref[0])
            continue
        n = ref.shape[1]
        slabs = ref.shape[2] // LANES
        for r in range(dil):
            for s in range(slabs):
                scr[s, pl.ds(r, n, stride=dil), :] = ref[r, :, s * LANES:(s + 1) * LANES]
        vals.append(jnp.concatenate([scr[s] for s in range(slabs)], axis=1))
    o1, o2, o3, l1, l2, l3 = vals
    mx = jnp.maximum(jnp.maximum(l1, l2), l3)
    e1, e2, e3 = jnp.exp(l1 - mx), jnp.exp(l2 - mx), jnp.exp(l3 - mx)
    o = (e1 * o1 + e2 * o2 + e3 * o3) / (e1 + e2 + e3)
    out = _mm(o, wo_ref[...])
    y_ref[...] = x_ref[...] + gate_ref[...] * _rms(out, gn_ref[...])


def _attn_out_call(os_, ls_, x, wo, gn, gate, tm):
    bm, r, d = x.shape
    res = [pl.BlockSpec((None, dil, tm // dil, KV_DIM), lambda b, i: (b, 0, i, 0))
           for _, dil in DILATION_GROUPS]
    return pl.pallas_call(
        _attn_out_kernel,
        grid=(bm, r // tm),
        in_specs=res + res + [_rows(tm, d), _const(wo), _const(gn), _mod(gate, tm)],
        out_specs=_rows(tm, d),
        out_shape=jax.ShapeDtypeStruct((bm, r, d), F32),
        scratch_shapes=[pltpu.VMEM((KV_DIM // LANES, tm, LANES), F32)] * (2 * N_DIL_GROUPS),
        compiler_params=_params("arbitrary", "arbitrary"),
        name="attn_merge_out_proj",
    )(*os_, *ls_, x, wo, gn, gate)


def _proj_res_kernel(o_ref, x_ref, wo_ref, gn_ref, gate_ref, y_ref):
    out = _mm(o_ref[...], wo_ref[...])
    y_ref[...] = x_ref[...] + gate_ref[...] * _rms(out, gn_ref[...])


def _proj_res_call(o, x, wo, gn, gate, tm):
    bm, r, d = x.shape
    return pl.pallas_call(
        _proj_res_kernel,
        grid=(bm, r // tm),
        in_specs=[_rows(tm, o.shape[2]), _rows(tm, d), _const(wo), _const(gn), _mod(gate, tm)],
        out_specs=_rows(tm, d),
        out_shape=jax.ShapeDtypeStruct((bm, r, d), F32),
        compiler_params=_params("arbitrary", "arbitrary"),
        name="attn_out_proj",
    )(o, x, wo, gn, gate)


def _top2(logits):
    lane = lax.broadcasted_iota(jnp.int32, logits.shape, 1).astype(F32)
    m1 = jnp.max(logits, axis=-1, keepdims=True)
    i1 = jnp.min(jnp.where(logits == m1, lane, float(LANES)), axis=-1, keepdims=True)
    rest = jnp.where(lane == i1, -jnp.inf, logits)
    m2 = jnp.max(rest, axis=-1, keepdims=True)
    i2 = jnp.min(jnp.where(rest == m2, lane, float(LANES)), axis=-1, keepdims=True)
    e = jnp.exp(m2 - m1)
    return lane, i1, i2, 1.0 / (1.0 + e), e / (1.0 + e)


def _route_kernel(x_ref, g_ref, sc_ref, sh_ref, wr_ref, br_ref, h_ref, route_ref, cnt_ref, carry_scr,
                  *, stride):
    @pl.when((pl.program_id(0) == 0) & (pl.program_id(1) == 0))
    def _():
        carry_scr[...] = jnp.zeros_like(carry_scr)

    h = _rms(x_ref[...], g_ref[...]) * (1.0 + sc_ref[...]) + sh_ref[...]
    tm = h.shape[0]
    for c in range(SLABS):
        h_ref[pl.ds(c, tm, stride=SUBLANES), :] = h[:, c * LANES:(c + 1) * LANES]
    logits = _mm(h, wr_ref[...]) + br_ref[...]
    lane, i1, i2, g1, g2 = _top2(logits)
    onehot = jnp.where(lane == i1, 1.0, 0.0) + jnp.where(lane == i2, 1.0, 0.0)
    row = lax.broadcasted_iota(jnp.int32, (tm, tm), 0)
    col = lax.broadcasted_iota(jnp.int32, (tm, tm), 1)
    earlier = jnp.where(col < row, 1.0, 0.0).astype(BF16)
    before = jnp.dot(earlier, onehot.astype(BF16), preferred_element_type=F32) + carry_scr[0:1, :]
    r1 = jnp.sum(jnp.where(lane == i1, before, 0.0), axis=-1, keepdims=True)
    r2 = jnp.sum(jnp.where(lane == i2, before, 0.0), axis=-1, keepdims=True)
    p1 = i1 * float(stride) + r1
    p2 = i2 * float(stride) + r2
    route_ref[...] = (jnp.where(lane == 0.0, p1, 0.0) + jnp.where(lane == 1.0, p2, 0.0)
                      + jnp.where(lane == 2.0, g1, 0.0) + jnp.where(lane == 3.0, g2, 0.0))
    carry = carry_scr[...] + jnp.sum(onehot, axis=0, keepdims=True)
    carry_scr[...] = carry
    cnt_ref[...] = carry


def _route_call(x, g, sc, sh, wr, br, tm, stride):
    bm, r, d = x.shape
    nt = r // tm
    return pl.pallas_call(
        functools.partial(_route_kernel, stride=stride),
        grid=(bm, nt),
        in_specs=[_rows(tm, d), _const(g), _mod(sc, tm), _mod(sh, tm), _const(wr), _const(br)],
        out_specs=[pl.BlockSpec((tm * SUBLANES, LANES), lambda b, i: (b * nt + i, 0)),
                   _rows(tm, LANES),
                   pl.BlockSpec((SUBLANES, LANES), lambda b, i: (0, 0))],
        out_shape=[jax.ShapeDtypeStruct((bm * r * SUBLANES, LANES), F32),
                   jax.ShapeDtypeStruct((bm, r, LANES), F32),
                   jax.ShapeDtypeStruct((SUBLANES, LANES), F32)],
        scratch_shapes=[pltpu.VMEM((SUBLANES, LANES), F32)],
        compiler_params=_params("arbitrary", "arbitrary"),
        name="moe_route",
    )(x, g, sc, sh, wr, br)


def _token_tile(ref, idx):
    return ref.at[pl.ds(pl.multiple_of(idx * SUBLANES, SUBLANES), SUBLANES), :]


def _dispatch_kernel(cnt_ref, pos_ref, h_ref, xs_hbm, zero_scr, sem, zsem, *, tm, stride, tmx):
    i = pl.program_id(0)

    def issue(t, carry):
        src = _token_tile(h_ref, t)
        for k in range(2):
            pltpu.make_async_copy(src, _token_tile(xs_hbm, pos_ref[0, k * tm + t]), sem).start()
        return carry

    lax.fori_loop(0, tm, issue, 0)
    for _ in range(2):
        pltpu.make_async_copy(h_ref, xs_hbm.at[pl.ds(0, tm * SUBLANES), :], sem).wait()

    @pl.when(i == pl.num_programs(0) - 1)
    def _():
        zero_scr[...] = jnp.zeros_like(zero_scr)
        copies = []
        for e in range(N_EXPERTS):
            start = pl.multiple_of((e * stride + cnt_ref[e]) * SUBLANES, SUBLANES)
            copies.append(pltpu.make_async_copy(zero_scr, xs_hbm.at[pl.ds(start, tmx * SUBLANES), :], zsem))
            copies[-1].start()
        for cp in copies:
            cp.wait()


def _dispatch_call(cnt, pos, h2d, tm, stride, tmx):
    n_tiles = pos.shape[0]
    rows = N_EXPERTS * stride * SUBLANES
    return pl.pallas_call(
        functools.partial(_dispatch_kernel, tm=tm, stride=stride, tmx=tmx),
        grid_spec=pltpu.PrefetchScalarGridSpec(
            num_scalar_prefetch=1,
            grid=(n_tiles,),
            in_specs=[pl.BlockSpec((None, 1, 2 * tm), lambda i, cnt: (i, 0, 0), memory_space=pltpu.SMEM),
                      pl.BlockSpec((tm * SUBLANES, LANES), lambda i, cnt: (i, 0))],
            out_specs=pl.BlockSpec(memory_space=pl.ANY),
            scratch_shapes=[pltpu.VMEM((tmx * SUBLANES, LANES), F32),
                            pltpu.SemaphoreType.DMA(()), pltpu.SemaphoreType.DMA(())],
        ),
        out_shape=jax.ShapeDtypeStruct((rows, LANES), F32),
        compiler_params=_params("arbitrary"),
        name="moe_dispatch",
    )(cnt, pos, h2d)


def _expert_kernel(blk_ref, exp_ref, nused_ref, xs_ref, wg_ref, wu_ref, wd_ref, o_ref, x_scr, acc_scr,
                   *, tmx):
    j = pl.program_id(0)
    f = pl.program_id(1)

    @pl.when(j < nused_ref[0])
    def _():
        @pl.when(f == 0)
        def _():
            for c in range(SLABS):
                x_scr[:, c * LANES:(c + 1) * LANES] = (
                    xs_ref[pl.ds(c, tmx, stride=SUBLANES), :].astype(x_scr.dtype))
            acc_scr[...] = jnp.zeros_like(acc_scr)

        x = x_scr[...]
        he = _silu(_mm(x, wg_ref[...])) * _mm(x, wu_ref[...])
        acc_scr[...] += _mm(he, wd_ref[...])

        @pl.when(f == pl.num_programs(1) - 1)
        def _():
            for c in range(SLABS):
                o_ref[pl.ds(c, tmx, stride=SUBLANES), :] = acc_scr[:, c * LANES:(c + 1) * LANES]


def _expert_call(blk, exp, nused, xs, wg, wu, wd, tmx, tf):
    ne, d, ff = wg.shape
    nf = ff // tf

    def fidx(j, f, nused):
        return jnp.where(j < nused[0], f, nf - 1)

    row_spec = pl.BlockSpec((tmx * SUBLANES, LANES), lambda j, f, blk, exp, nu: (blk[j], 0))
    return pl.pallas_call(
        functools.partial(_expert_kernel, tmx=tmx),
        grid_spec=pltpu.PrefetchScalarGridSpec(
            num_scalar_prefetch=3,
            grid=(blk.shape[0], nf),
            in_specs=[row_spec,
                      pl.BlockSpec((None, d, tf), lambda j, f, blk, exp, nu: (exp[j], 0, fidx(j, f, nu))),
                      pl.BlockSpec((None, d, tf), lambda j, f, blk, exp, nu: (exp[j], 0, fidx(j, f, nu))),
                      pl.BlockSpec((None, tf, d), lambda j, f, blk, exp, nu: (exp[j], fidx(j, f, nu), 0))],
            out_specs=row_spec,
            scratch_shapes=[pltpu.VMEM((tmx, d), wg.dtype), pltpu.VMEM((tmx, d), F32)],
        ),
        out_shape=jax.ShapeDtypeStruct(xs.shape, F32),
        compiler_params=_params("arbitrary", "arbitrary"),
        name="moe_grouped_experts",
    )(blk, exp, nused, xs, wg, wu, wd)


def _combine_kernel(pos_ref, route_ref, x_ref, gn_ref, gate_ref, out_hbm, y_ref, buf, sem, *, tm):
    def issue(t, carry):
        for k in range(2):
            pltpu.make_async_copy(_token_tile(out_hbm, pos_ref[0, k * tm + t]),
                                  _token_tile(buf, k * tm + t), sem).start()
        return carry

    lax.fori_loop(0, tm, issue, 0)
    pltpu.make_async_copy(out_hbm.at[pl.ds(0, 2 * tm * SUBLANES), :], buf, sem).wait()
    route = route_ref[...]
    g1, g2 = route[:, 2:3], route[:, 3:4]
    parts = []
    for c in range(SLABS):
        parts.append(g1 * buf[pl.ds(c, tm, stride=SUBLANES), :]
                     + g2 * buf[pl.ds(tm * SUBLANES + c, tm, stride=SUBLANES), :])
    y = jnp.concatenate(parts, axis=1)
    y_ref[...] = x_ref[...] + gate_ref[...] * _rms(y, gn_ref[...])


def _combine_call(pos, route, x, gn, gate, out2d, tm):
    bm, r, d = x.shape
    nt = r // tm
    return pl.pallas_call(
        functools.partial(_combine_kernel, tm=tm),
        grid=(bm, nt),
        in_specs=[pl.BlockSpec((None, 1, 2 * tm), lambda b, i: (b * nt + i, 0, 0), memory_space=pltpu.SMEM),
                  _rows(tm, LANES), _rows(tm, d), _const(gn), _mod(gate, tm),
                  pl.BlockSpec(memory_space=pl.ANY)],
        out_specs=_rows(tm, d),
        out_shape=jax.ShapeDtypeStruct((bm, r, d), F32),
        scratch_shapes=[pltpu.VMEM((2 * tm * SUBLANES, LANES), F32), pltpu.SemaphoreType.DMA(())],
        compiler_params=_params("arbitrary", "arbitrary"),
        name="moe_combine",
    )(pos, route, x, gn, gate, out2d)


def _moe_sparse(x, g, sc, sh, wr, br, wg, wu, wd, gn, gate, tm):
    bm, r, d = x.shape
    m = bm * r
    tmx = MOE_TILE_ROWS
    stride = m + tmx
    n_tiles = 2 * m // tmx + N_EXPERTS
    h2d, route, cnt = _route_call(x, g, sc, sh, wr, br, tm, stride)
    cnt = cnt[0, :N_EXPERTS].astype(jnp.int32)
    pos = route[..., :2].astype(jnp.int32).reshape(m // tm, tm, 2).transpose(0, 2, 1).reshape(m // tm, 1, 2 * tm)
    xs = _dispatch_call(cnt, pos, h2d, tm, stride, tmx)
    per = (cnt + tmx - 1) // tmx
    ends = jnp.cumsum(per)
    j = jnp.minimum(jnp.arange(n_tiles, dtype=jnp.int32), ends[-1] - 1)
    exp = jnp.sum((j[:, None] >= ends[None, :]).astype(jnp.int32), axis=1)
    blk = exp * (stride // tmx) + j - (ends - per)[exp]
    out2d = _expert_call(blk, exp, ends[-1:], xs, wg, wu, wd, tmx, wg.shape[2] // 2)
    return _combine_call(pos, route, x, gn, gate, out2d, tm)


def _router_kernel(x_ref, g_ref, sc_ref, sh_ref, wr_ref, br_ref, h_ref, comb_ref):
    h = _rms(x_ref[...], g_ref[...]) * (1.0 + sc_ref[...]) + sh_ref[...]
    h_ref[...] = h.astype(h_ref.dtype)
    logits = _mm(h, wr_ref[...]) + br_ref[...]
    lane, i1, i2, g1, g2 = _top2(logits)
    comb_ref[...] = jnp.where(lane == i1, g1, 0.0) + jnp.where(lane == i2, g2, 0.0)


def _router_call(x, g, sc, sh, wr, br, h_dtype, tm):
    bm, r, d = x.shape
    return pl.pallas_call(
        _router_kernel,
        grid=(bm, r // tm),
        in_specs=[_rows(tm, d), _const(g), _mod(sc, tm), _mod(sh, tm), _const(wr), _const(br)],
        out_specs=[_rows(tm, d), _rows(tm, LANES)],
        out_shape=[jax.ShapeDtypeStruct((bm, r, d), h_dtype),
                   jax.ShapeDtypeStruct((bm, r, LANES), F32)],
        compiler_params=_params("arbitrary", "arbitrary"),
        name="moe_router",
    )(x, g, sc, sh, wr, br)


def _moe_kernel(h_ref, comb_ref, x_ref, wg_ref, wu_ref, wd_ref, gn_ref, gate_ref, o_ref, acc_scr):
    e = pl.program_id(2)
    f = pl.program_id(3)

    @pl.when((e == 0) & (f == 0))
    def _():
        acc_scr[...] = jnp.zeros_like(acc_scr)

    comb = comb_ref[...]
    lane = lax.broadcasted_iota(jnp.int32, comb.shape, 1)
    cw = jnp.sum(jnp.where(lane == e, comb, 0.0), axis=-1, keepdims=True)
    h = h_ref[...]
    he = _silu(_mm(h, wg_ref[...], False)) * _mm(h, wu_ref[...], False)
    acc_scr[...] += cw * _mm(he, wd_ref[...], False)

    @pl.when((e == pl.num_programs(2) - 1) & (f == pl.num_programs(3) - 1))
    def _():
        o_ref[...] = x_ref[...] + gate_ref[...] * _rms(acc_scr[...], gn_ref[...])


def _moe_call(h, comb, x, wg, wu, wd, gn, gate, tm, tf):
    bm, r, d = x.shape
    ne, _, ff = wg.shape
    return pl.pallas_call(
        _moe_kernel,
        grid=(bm, r // tm, ne, ff // tf),
        in_specs=[_rows(tm, d), _rows(tm, LANES), _rows(tm, d),
                  pl.BlockSpec((None, d, tf), lambda b, i, e, f: (e, 0, f)),
                  pl.BlockSpec((None, d, tf), lambda b, i, e, f: (e, 0, f)),
                  pl.BlockSpec((None, tf, d), lambda b, i, e, f: (e, f, 0)),
                  _const(gn), _mod(gate, tm)],
        out_specs=_rows(tm, d),
        out_shape=jax.ShapeDtypeStruct((bm, r, d), F32),
        scratch_shapes=[pltpu.VMEM((tm, d), F32)],
        compiler_params=_params("arbitrary", "arbitrary", "arbitrary", "arbitrary"),
        name="moe_experts",
    )(h, comb, x, wg, wu, wd, gn, gate)


def _ssm_params(lam_re, lam_im, log_dt, b_re, b_im, c_re, c_im, wdtype):
    dt = jnp.exp(log_dt)[:, None]
    mag = jnp.exp(lam_re * dt)
    a_re, a_im = mag * jnp.cos(lam_im * dt), mag * jnp.sin(lam_im * dt)
    den = lam_re * lam_re + lam_im * lam_im
    f_re = ((a_re - 1.0) * lam_re + a_im * lam_im) / den
    f_im = (a_im * lam_re - (a_re - 1.0) * lam_im) / den
    bb_re = f_re[..., None] * b_re - f_im[..., None] * b_im
    bb_im = f_re[..., None] * b_im + f_im[..., None] * b_re
    eye = jnp.eye(GROUPS_PER_SLAB, dtype=F32)

    def in_map(bb):
        t = bb.reshape(SLABS, GROUPS_PER_SLAB, SSM_STATE, SSM_CH)
        w = jnp.einsum('sgpc,gh->sgchp', t, eye)
        return w.reshape(SLABS, LANES, SLAB_STATES).astype(wdtype)

    def out_map(cc):
        t = cc.reshape(SLABS, GROUPS_PER_SLAB, SSM_CH, SSM_STATE)
        w = jnp.einsum('sgcp,gh->sgphc', t, eye)
        return w.reshape(SLABS, SLAB_STATES, LANES).astype(wdtype)

    return a_re, a_im, in_map(bb_re), in_map(bb_im), out_map(c_re), out_map(c_im)


def _rope_tables(pos):
    half = ROT_DIM // 2
    inv = ROPE_THETA ** (-jnp.arange(half, dtype=F32) / half)
    ang = pos.astype(F32)[:, None] * inv[None, :]
    cos, sin = jnp.cos(ang), jnp.sin(ang)
    n = pos.shape[0]
    pad = jnp.zeros((n, HEAD_DIM - ROT_DIM), F32)
    zero = jnp.zeros((n, half), F32)
    cos_h = jnp.concatenate([cos, cos, pad + 1.0], axis=1)
    lo_h = jnp.concatenate([-sin, zero, pad], axis=1)
    hi_h = jnp.concatenate([zero, sin, pad], axis=1)
    rep = LANES // HEAD_DIM
    return jnp.tile(cos_h, (1, rep)), jnp.tile(lo_h, (1, rep)), jnp.tile(hi_h, (1, rep))


def _trunk(x, mods, pos, wts, prompt, ssm0=None, cache=None):
    bm, r, d = x.shape
    tm = 512 if prompt else r
    wd = BF16 if prompt else F32
    cast = lambda w: w.astype(wd)
    norm_g = wts['norm_g'].reshape(DEPTH, 4, 1, d)

    sh1, sc1, g1, sh2, sc2, g2 = mods[0]
    u = _inproj_call(x, norm_g[0, 0], sc1, sh1, cast(wts['ssm_w_in'][0]), tm)
    a_re, a_im, wbr, wbi, cr, ci = _ssm_params(
        wts['ssm_lam_re'][0], wts['ssm_lam_im'][0], wts['ssm_log_dt'][0], wts['ssm_b_re'][0],
        wts['ssm_b_im'][0], wts['ssm_c_re'][0], wts['ssm_c_im'][0], wd)
    dvec = wts['ssm_d'][0].reshape(1, d)
    if prompt:
        z, fr, fi = _ssm_prompt_call(u, wbr, wbi, a_re.reshape(STATE_TILES, LANES),
                                     a_im.reshape(STATE_TILES, LANES), cr, ci, dvec, 256)
    else:
        z, fr, fi = _ssm_sample_call(u[0], ssm0[0].reshape(r, SSM_FLAT), ssm0[1].reshape(r, SSM_FLAT),
                                     wbr, wbi, a_re.reshape(1, SSM_FLAT), a_im.reshape(1, SSM_FLAT),
                                     cr, ci, dvec)
        z = z[None]
    fr = fr.reshape(-1, SSM_GROUPS, SSM_STATE)[None]
    fi = fi.reshape(-1, SSM_GROUPS, SSM_STATE)[None]
    x = _s5out_call(z, x, cast(wts['ssm_w_glu'][0]), cast(wts['ssm_w_out'][0]), norm_g[0, 1], g1, tm)
    tf = wts['ffn_w_gate'].shape[2] // 2
    x = _ffn_call(x, norm_g[0, 2], sc2, sh2, cast(wts['ffn_w_gate'][0]), cast(wts['ffn_w_up'][0]),
                  cast(wts['ffn_w_down'][0]), norm_g[0, 3], g2, tm, tf)

    sh1, sc1, g1, sh2, sc2, g2 = mods[1]
    cos, slo, shi = _rope_tables(pos)
    qkv = _qkv_call(x, wts['kv_norm_g'].reshape(1, d), norm_g[1, 0], sc1, sh1, cast(wts['w_kv']),
                    cast(wts['attn_w_q'][0]), cos, slo, shi, tm, prompt)
    wo = cast(wts['attn_w_o'][0])
    wr = cast(jnp.zeros((d, LANES), F32).at[:, :N_EXPERTS].set(wts['moe_w_router'][0]))
    br = jnp.full((1, LANES), NEG_INF, F32).at[0, :N_EXPERTS].set(wts['moe_b_router'][0])
    wg, wu, wdn = (wts[n][0].astype(BF16) for n in ('moe_w_gate', 'moe_w_up', 'moe_w_down'))
    if prompt:
        k, v, kd, vd, qd = qkv
        os_, ls_ = [], []
        for g, (_, dil) in enumerate(DILATION_GROUPS):
            o, lse = _band_call(qd[g], kd[g], vd[g], min(4, r // dil // SPAN))
            os_.append(o)
            ls_.append(lse)
        x = _attn_out_call(os_, ls_, x, wo, norm_g[1, 1], g1, tm)
        x = _moe_sparse(x, norm_g[1, 2], sc2, sh2, wr, br, wg, wu, wdn, norm_g[1, 3], g2, tm)
    else:
        k, v, q = qkv
        o, k, v = _sample_attn_call(q[0].reshape(r, N_DIL_GROUPS, N_KV_HEADS, HEAD_DIM, 1),
                                    k[0].reshape(r, N_KV_HEADS, HEAD_DIM, 1),
                                    v[0].reshape(r, N_KV_HEADS, HEAD_DIM, 1),
                                    cache[0].transpose(0, 2, 3, 1), cache[1].transpose(0, 2, 3, 1))
        k, v = k.transpose(0, 3, 1, 2), v.transpose(0, 3, 1, 2)
        x = _proj_res_call(o.reshape(1, r, KV_DIM), x, wo, norm_g[1, 1], g1, tm)
        h, comb = _router_call(x, norm_g[1, 2], sc2, sh2, wr, br, wd, tm)
        x = _moe_call(h, comb, x, wg, wu, wdn, norm_g[1, 3], g2, tm, wg.shape[2] // 4)
    return x, fr, fi, k, v


def kernel(x_prompt, x_sample, state_ssm_re, state_ssm_im, cache_k, cache_v, c_prompt, c_sample, ada_w, ada_b, norm_g, ssm_w_in, ssm_lam_re, ssm_lam_im, ssm_log_dt, ssm_b_re, ssm_b_im, ssm_c_re, ssm_c_im, ssm_d, ssm_w_glu, ssm_w_out, kv_norm_g, w_kv, attn_w_q, attn_w_o, ffn_w_gate, ffn_w_up, ffn_w_down, moe_w_router, moe_b_router, moe_w_gate, moe_w_up, moe_w_down):
    wts = dict(norm_g=norm_g, ssm_w_in=ssm_w_in, ssm_lam_re=ssm_lam_re, ssm_lam_im=ssm_lam_im,
               ssm_log_dt=ssm_log_dt, ssm_b_re=ssm_b_re, ssm_b_im=ssm_b_im, ssm_c_re=ssm_c_re,
               ssm_c_im=ssm_c_im, ssm_d=ssm_d, ssm_w_glu=ssm_w_glu, ssm_w_out=ssm_w_out,
               kv_norm_g=kv_norm_g, w_kv=w_kv, attn_w_q=attn_w_q, attn_w_o=attn_w_o,
               ffn_w_gate=ffn_w_gate, ffn_w_up=ffn_w_up, ffn_w_down=ffn_w_down,
               moe_w_router=moe_w_router, moe_b_router=moe_b_router, moe_w_gate=moe_w_gate,
               moe_w_up=moe_w_up, moe_w_down=moe_w_down)
    nb, seq, d = x_prompt.shape
    ns = x_sample.shape[0]
    assert x_sample.shape[1] == 1 and ns % SUBLANES == 0

    pad = (-(ns + nb)) % SUBLANES
    c_all = jnp.concatenate([c_sample, c_prompt, jnp.zeros((pad, d), F32)], axis=0)
    mod_all = _ada_call(c_all, ada_w, ada_b)
    mods_s = [[mod_all[l, k, :ns][None] for k in range(6)] for l in range(DEPTH)]
    mods_p = [[mod_all[l, k, ns:ns + nb][:, None, :] for k in range(6)] for l in range(DEPTH)]

    pos_p = jnp.arange(seq, dtype=jnp.int32)
    pos_s = jnp.full((ns,), PAST_LEN, dtype=jnp.int32)
    y_p, sr_p, si_p, k_p, v_p = _trunk(x_prompt, mods_p, pos_p, wts, True)
    xs = x_sample.reshape(1, ns, d)
    y_s, sr_s, si_s, k_s, v_s = _trunk(xs, mods_s, pos_s, wts, False,
                                       ssm0=(state_ssm_re[0], state_ssm_im[0]), cache=(cache_k, cache_v))

    keep = min(MAX_WINDOW, seq)
    k_rows_p = k_p[:, seq - keep:].reshape(nb, keep, N_KV_HEADS, HEAD_DIM)
    v_rows_p = v_p[:, seq - keep:].reshape(nb, keep, N_KV_HEADS, HEAD_DIM)
    return (y_p, y_s.reshape(ns, 1, d), sr_p, si_p, sr_s, si_s, k_rows_p, v_rows_p, k_s, v_s)
```

```python
import functools
import math

import jax
import jax.numpy as jnp
from jax import lax
from jax.experimental import pallas as pl
from jax.experimental.pallas import tpu as pltpu

F32 = jnp.float32
BF16 = jnp.bfloat16
HIGHEST = lax.Precision.HIGHEST

D_MODEL = 1024
DEPTH = 2
PAST_LEN = 16384
SSM_CH = 16
SSM_GROUPS = D_MODEL // SSM_CH
SSM_STATE = 64
SSM_FLAT = SSM_GROUPS * SSM_STATE
HEAD_DIM = 64
N_KV_HEADS = D_MODEL // 128
KV_DIM = N_KV_HEADS * HEAD_DIM
DILATION_GROUPS = ((128, 1), (512, 4), (2048, 16))
N_DIL_GROUPS = len(DILATION_GROUPS)
SPAN = 128
MAX_WINDOW = 2048
ROT_DIM = HEAD_DIM // 4
ROPE_THETA = 500000.0
N_EXPERTS = 8
RMS_EPS = 1e-6
NEG_INF = -1e30

LANES = 128
SUBLANES = 8
SLABS = D_MODEL // LANES
GROUPS_PER_SLAB = LANES // SSM_CH
SLAB_STATES = GROUPS_PER_SLAB * SSM_STATE
STATE_TILES = SSM_FLAT // LANES
SCAN_STEP_PITCH = 2
SCAN_TILE_PITCH = 15
SCAN_SLAB_ROWS = 480
SCAN_CHUNK = 128
MOE_TILE_ROWS = 512
VMEM_LIMIT = 56 * 1024 * 1024


def _params(*sem):
    return pltpu.CompilerParams(dimension_semantics=sem, vmem_limit_bytes=VMEM_LIMIT)


def _mm(a, w, precise=True):
    if w.dtype == BF16 or not precise:
        return jnp.dot(a.astype(BF16), w.astype(BF16), preferred_element_type=F32)
    return jnp.dot(a.astype(F32), w, preferred_element_type=F32, precision=HIGHEST)


def _rms(x, g):
    return x * lax.rsqrt(jnp.mean(x * x, axis=-1, keepdims=True) + RMS_EPS) * g


def _sigmoid(x):
    return 1.0 / (1.0 + jnp.exp(-x))


def _silu(x):
    return x * _sigmoid(x)


def _gelu_tanh(x):
    return 0.5 * x * (1.0 + jnp.tanh(math.sqrt(2.0 / math.pi) * (x + 0.044715 * (x * x * x))))


def _rows(tm, n):
    return pl.BlockSpec((None, tm, n), lambda b, i, *_: (b, i, 0))


def _mod(arr, tm):
    if arr.shape[1] == 1:
        return pl.BlockSpec((None, 1, arr.shape[2]), lambda b, i, *_: (b, 0, 0))
    return pl.BlockSpec((None, tm, arr.shape[2]), lambda b, i, *_: (b, i, 0))


def _const(arr):
    nd = arr.ndim
    return pl.BlockSpec(arr.shape, lambda *_: (0,) * nd)


def _ada_kernel(c_ref, w_ref, b_ref, o_ref):
    o_ref[...] = _mm(_silu(c_ref[...]), w_ref[...]) + b_ref[...]


def _ada_call(c_all, ada_w, ada_b):
    rows = c_all.shape[0]
    b4 = ada_b.reshape(DEPTH, 6, 1, D_MODEL)
    return pl.pallas_call(
        _ada_kernel,
        grid=(DEPTH, 6),
        in_specs=[pl.BlockSpec((rows, D_MODEL), lambda l, k: (0, 0)),
                  pl.BlockSpec((None, D_MODEL, D_MODEL), lambda l, k: (l, 0, k)),
                  pl.BlockSpec((None, None, 1, D_MODEL), lambda l, k: (l, k, 0, 0))],
        out_specs=pl.BlockSpec((None, None, rows, D_MODEL), lambda l, k: (l, k, 0, 0)),
        out_shape=jax.ShapeDtypeStruct((DEPTH, 6, rows, D_MODEL), F32),
        compiler_params=_params("arbitrary", "arbitrary"),
        name="ada_modulation",
    )(c_all, ada_w, b4)


def _inproj_kernel(x_ref, g_ref, sc_ref, sh_ref, w_ref, u_ref):
    h = _rms(x_ref[...], g_ref[...]) * (1.0 + sc_ref[...]) + sh_ref[...]
    u_ref[...] = _mm(h, w_ref[...])


def _inproj_call(x, g, sc, sh, w, tm):
    bm, r, d = x.shape
    return pl.pallas_call(
        _inproj_kernel,
        grid=(bm, r // tm),
        in_specs=[_rows(tm, d), _const(g), _mod(sc, tm), _mod(sh, tm), _const(w)],
        out_specs=_rows(tm, w.shape[1]),
        out_shape=jax.ShapeDtypeStruct((bm, r, w.shape[1]), F32),
        compiler_params=_params("arbitrary", "arbitrary"),
        name="s5_in_proj",
    )(x, g, sc, sh, w)


def _ssm_prompt_kernel(u_ref, wbr_ref, wbi_ref, ar_ref, ai_ref, cr_ref, ci_ref, d_ref,
                       z_ref, fr_ref, fi_ref, br_scr, bi_scr, sr_scr, si_scr, *, t_chunk, nseq):
    tb_n = t_chunk // SUBLANES
    tiles = SLAB_STATES // LANES

    def tile_rows(j):
        return pl.ds(SCAN_TILE_PITCH * j, SUBLANES, stride=SCAN_STEP_PITCH)

    @pl.when(pl.program_id(0) == 0)
    def _():
        sr_scr[...] = jnp.zeros_like(sr_scr)
        si_scr[...] = jnp.zeros_like(si_scr)

    ub = u_ref[...].reshape(nseq * t_chunk, D_MODEL).astype(BF16)
    for s in range(SLABS):
        us = ub[:, s * LANES:(s + 1) * LANES]
        bre = jnp.dot(us, wbr_ref[s], preferred_element_type=F32)
        bim = jnp.dot(us, wbi_ref[s], preferred_element_type=F32)
        for jj in range(tiles):
            cols = slice(jj * LANES, (jj + 1) * LANES)
            for q in range(nseq):
                for tb in range(tb_n):
                    rows = slice((q * tb_n + tb) * SUBLANES, (q * tb_n + tb + 1) * SUBLANES)
                    br_scr[q, tb, tile_rows(s * tiles + jj), :] = bre[rows, cols]
                    bi_scr[q, tb, tile_rows(s * tiles + jj), :] = bim[rows, cols]

    ar = ar_ref[...]
    ai = ai_ref[...]

    def body(tb, carry):
        carry = list(carry)
        for r in range(SUBLANES):
            step = pl.ds(SCAN_STEP_PITCH * r, STATE_TILES, stride=SCAN_TILE_PITCH)
            for q in range(nseq):
                sr, si = carry[q]
                nsr = ar * sr - ai * si + br_scr[q, tb, step, :]
                nsi = ar * si + ai * sr + bi_scr[q, tb, step, :]
                carry[q] = (nsr, nsi)
                br_scr[q, tb, step, :] = nsr
                bi_scr[q, tb, step, :] = nsi
        return tuple(carry)

    final = lax.fori_loop(0, tb_n, body, tuple((sr_scr[q], si_scr[q]) for q in range(nseq)))
    for q in range(nseq):
        sr_scr[q], si_scr[q] = final[q]
        fr_ref[q], fi_ref[q] = final[q]

    def states(scr, j):
        return jnp.concatenate([scr[q, tb, tile_rows(j), :] for q in range(nseq) for tb in range(tb_n)],
                               axis=0).astype(BF16)

    for s in range(SLABS):
        y = None
        for jj in range(tiles):
            cols = slice(jj * LANES, (jj + 1) * LANES)
            part = (jnp.dot(states(br_scr, s * tiles + jj), cr_ref[s, cols, :], preferred_element_type=F32)
                    - jnp.dot(states(bi_scr, s * tiles + jj), ci_ref[s, cols, :], preferred_element_type=F32))
            y = part if y is None else y + part
        sl = slice(s * LANES, (s + 1) * LANES)
        for q in range(nseq):
            yq = y[q * t_chunk:(q + 1) * t_chunk]
            z_ref[q, :, sl] = _gelu_tanh(yq + d_ref[:, sl] * u_ref[q, :, sl])


def _ssm_prompt_call(u, wbr, wbi, ar, ai, cr, ci, d, t_chunk):
    b, l, dm = u.shape
    kern = functools.partial(_ssm_prompt_kernel, t_chunk=t_chunk, nseq=b)
    seq_spec = pl.BlockSpec((b, t_chunk, dm), lambda c: (0, c, 0))
    state_spec = pl.BlockSpec((b, STATE_TILES, LANES), lambda c: (0, 0, 0))
    scr = pltpu.VMEM((b, t_chunk // SUBLANES, SCAN_SLAB_ROWS, LANES), F32)
    carry = pltpu.VMEM((b, STATE_TILES, LANES), F32)
    return pl.pallas_call(
        kern,
        grid=(l // t_chunk,),
        in_specs=[seq_spec, _const(wbr), _const(wbi), _const(ar), _const(ai),
                  _const(cr), _const(ci), _const(d)],
        out_specs=[seq_spec, state_spec, state_spec],
        out_shape=[jax.ShapeDtypeStruct((b, l, dm), F32),
                   jax.ShapeDtypeStruct((b, STATE_TILES, LANES), F32),
                   jax.ShapeDtypeStruct((b, STATE_TILES, LANES), F32)],
        scratch_shapes=[scr, scr, carry, carry],
        compiler_params=_params("arbitrary"),
        name="s5_scan_prompt",
    )(u, wbr, wbi, ar, ai, cr, ci, d)


def _ssm_sample_kernel(u_ref, x0r_ref, x0i_ref, wbr_ref, wbi_ref, ar_ref, ai_ref, cr_ref, ci_ref,
                       d_ref, z_ref, fr_ref, fi_ref):
    for s in range(SLABS):
        sl = slice(s * LANES, (s + 1) * LANES)
        st = slice(s * SLAB_STATES, (s + 1) * SLAB_STATES)
        us = u_ref[:, sl]
        a_r, a_i = ar_ref[:, st], ai_ref[:, st]
        x0r, x0i = x0r_ref[:, st], x0i_ref[:, st]
        sr = _mm(us, wbr_ref[s]) + a_r * x0r - a_i * x0i
        si = _mm(us, wbi_ref[s]) + a_r * x0i + a_i * x0r
        fr_ref[:, st] = sr
        fi_ref[:, st] = si
        y = _mm(sr, cr_ref[s]) - _mm(si, ci_ref[s])
        z_ref[:, sl] = _gelu_tanh(y + d_ref[:, sl] * us)


def _ssm_sample_call(u, x0r, x0i, wbr, wbi, ar, ai, cr, ci, d):
    n = u.shape[0]
    args = (u, x0r, x0i, wbr, wbi, ar, ai, cr, ci, d)
    return pl.pallas_call(
        _ssm_sample_kernel,
        grid=(1,),
        in_specs=[_const(a) for a in args],
        out_specs=[pl.BlockSpec((n, D_MODEL), lambda i: (0, 0)),
                   pl.BlockSpec((n, SSM_FLAT), lambda i: (0, 0)),
                   pl.BlockSpec((n, SSM_FLAT), lambda i: (0, 0))],
        out_shape=[jax.ShapeDtypeStruct((n, D_MODEL), F32),
                   jax.ShapeDtypeStruct((n, SSM_FLAT), F32),
                   jax.ShapeDtypeStruct((n, SSM_FLAT), F32)],
        compiler_params=_params("arbitrary"),
        name="s5_step_sample",
    )(*args)


def _s5out_kernel(z_ref, x_ref, wg_ref, wo_ref, gn_ref, gate_ref, o_ref):
    z = z_ref[...]
    gl = z * _sigmoid(_mm(z, wg_ref[...]))
    out = _mm(gl, wo_ref[...])
    o_ref[...] = x_ref[...] + gate_ref[...] * _rms(out, gn_ref[...])


def _s5out_call(z, x, wg, wo, gn, gate, tm):
    bm, r, d = x.shape
    return pl.pallas_call(
        _s5out_kernel,
        grid=(bm, r // tm),
        in_specs=[_rows(tm, d), _rows(tm, d), _const(wg), _const(wo), _const(gn), _mod(gate, tm)],
        out_specs=_rows(tm, d),
        out_shape=jax.ShapeDtypeStruct((bm, r, d), F32),
        compiler_params=_params("arbitrary", "arbitrary"),
        name="s5_glu_out_proj",
    )(z, x, wg, wo, gn, gate)


def _ffn_kernel(x_ref, g_ref, sc_ref, sh_ref, wg_ref, wu_ref, wd_ref, gn_ref, gate_ref, o_ref,
                h_scr, acc_scr):
    f = pl.program_id(2)

    @pl.when(f == 0)
    def _():
        h = _rms(x_ref[...], g_ref[...]) * (1.0 + sc_ref[...]) + sh_ref[...]
        h_scr[...] = h.astype(h_scr.dtype)
        acc_scr[...] = jnp.zeros_like(acc_scr)

    h = h_scr[...]
    he = _silu(_mm(h, wg_ref[...])) * _mm(h, wu_ref[...])
    acc_scr[...] += _mm(he, wd_ref[...])

    @pl.when(f == pl.num_programs(2) - 1)
    def _():
        o_ref[...] = x_ref[...] + gate_ref[...] * _rms(acc_scr[...], gn_ref[...])


def _ffn_call(x, g, sc, sh, wg, wu, wd, gn, gate, tm, tf):
    bm, r, d = x.shape
    ff = wg.shape[1]
    return pl.pallas_call(
        _ffn_kernel,
        grid=(bm, r // tm, ff // tf),
        in_specs=[_rows(tm, d), _const(g), _mod(sc, tm), _mod(sh, tm),
                  pl.BlockSpec((d, tf), lambda b, i, f: (0, f)),
                  pl.BlockSpec((d, tf), lambda b, i, f: (0, f)),
                  pl.BlockSpec((tf, d), lambda b, i, f: (f, 0)),
                  _const(gn), _mod(gate, tm)],
        out_specs=_rows(tm, d),
        out_shape=jax.ShapeDtypeStruct((bm, r, d), F32),
        scratch_shapes=[pltpu.VMEM((tm, d), wg.dtype), pltpu.VMEM((tm, d), F32)],
        compiler_params=_params("arbitrary", "arbitrary", "arbitrary"),
        name="dense_swiglu",
    )(x, g, sc, sh, wg, wu, wd, gn, gate)


def _rope(t, cos, sin_lo, sin_hi):
    half = ROT_DIM // 2
    outs = []
    for s in range(t.shape[1] // LANES):
        ts = t[:, s * LANES:(s + 1) * LANES]
        outs.append(ts * cos + pltpu.roll(ts, LANES - half, 1) * sin_lo + pltpu.roll(ts, half, 1) * sin_hi)
    return jnp.concatenate(outs, axis=1)


def _qkv_values(x_ref, gkv_ref, g_ref, sc_ref, sh_ref, wkv_ref, wq_ref, cos_ref, slo_ref, shi_ref):
    x = x_ref[...]
    xn = x * lax.rsqrt(jnp.mean(x * x, axis=-1, keepdims=True) + RMS_EPS)
    kv = _mm(xn * gkv_ref[...], wkv_ref[...])
    q = _mm((xn * g_ref[...]) * (1.0 + sc_ref[...]) + sh_ref[...], wq_ref[...])
    cos, slo, shi = cos_ref[...], slo_ref[...], shi_ref[...]
    k = _rope(kv[:, :KV_DIM], cos, slo, shi)
    v = kv[:, KV_DIM:]
    q = _rope(q, cos, slo, shi) * (HEAD_DIM ** -0.5)
    return k, v, q


def _qkv_sample_kernel(*refs):
    k_ref, v_ref, q_ref = refs[10:]
    k, v, q = _qkv_values(*refs[:10])
    k_ref[...] = k
    v_ref[...] = v
    q_ref[...] = q


def _store_residues(dst_ref, val, scr, dil):
    if dil == 1:
        dst_ref[0] = val.astype(dst_ref.dtype)
        return
    slabs = val.shape[1] // LANES
    for s in range(slabs):
        scr[s] = val[:, s * LANES:(s + 1) * LANES]
    n = val.shape[0] // dil
    for r in range(dil):
        rows = [scr[s, pl.ds(r, n, stride=dil), :] for s in range(slabs)]
        dst_ref[r] = jnp.concatenate(rows, axis=1).astype(dst_ref.dtype)


def _qkv_prompt_kernel(*refs):
    k_ref, v_ref = refs[10:12]
    kd_refs = refs[12:12 + N_DIL_GROUPS]
    vd_refs = refs[12 + N_DIL_GROUPS:12 + 2 * N_DIL_GROUPS]
    qd_refs = refs[12 + 2 * N_DIL_GROUPS:12 + 3 * N_DIL_GROUPS]
    scr = refs[12 + 3 * N_DIL_GROUPS]
    k, v, q = _qkv_values(*refs[:10])
    k_ref[...] = k
    v_ref[...] = v
    for g, (_, dil) in enumerate(DILATION_GROUPS):
        _store_residues(kd_refs[g], k, scr, dil)
        _store_residues(vd_refs[g], v, scr, dil)
        _store_residues(qd_refs[g], q[:, g * KV_DIM:(g + 1) * KV_DIM], scr, dil)


def _qkv_call(x, gkv, g, sc, sh, wkv, wq, cos, slo, shi, tm, prompt):
    bm, r, d = x.shape
    nq = wq.shape[1]
    tab = pl.BlockSpec((tm, LANES), lambda b, i: (i, 0))
    in_specs = [_rows(tm, d), _const(gkv), _const(g), _mod(sc, tm), _mod(sh, tm),
                _const(wkv), _const(wq), tab, tab, tab]
    out_specs = [_rows(tm, KV_DIM), _rows(tm, KV_DIM)]
    out_shape = [jax.ShapeDtypeStruct((bm, r, KV_DIM), F32), jax.ShapeDtypeStruct((bm, r, KV_DIM), F32)]
    if not prompt:
        return pl.pallas_call(
            _qkv_sample_kernel,
            grid=(bm, r // tm),
            in_specs=in_specs,
            out_specs=out_specs + [_rows(tm, nq)],
            out_shape=out_shape + [jax.ShapeDtypeStruct((bm, r, nq), F32)],
            compiler_params=_params("arbitrary", "arbitrary"),
            name="qkv_proj_rope_sample",
        )(x, gkv, g, sc, sh, wkv, wq, cos, slo, shi)
    for _ in range(3):
        for _, dil in DILATION_GROUPS:
            out_specs.append(pl.BlockSpec((None, dil, tm // dil, KV_DIM), lambda b, i: (b, 0, i, 0)))
            out_shape.append(jax.ShapeDtypeStruct((bm, dil, r // dil, KV_DIM), BF16))
    outs = pl.pallas_call(
        _qkv_prompt_kernel,
        grid=(bm, r // tm),
        in_specs=in_specs,
        out_specs=out_specs,
        out_shape=out_shape,
        scratch_shapes=[pltpu.VMEM((KV_DIM // LANES, tm, LANES), F32)],
        compiler_params=_params("arbitrary", "arbitrary"),
        name="qkv_proj_rope_prompt",
    )(x, gkv, g, sc, sh, wkv, wq, cos, slo, shi)
    n = N_DIL_GROUPS
    return outs[0], outs[1], outs[2:2 + n], outs[2 + n:2 + 2 * n], outs[2 + 2 * n:2 + 3 * n]


def _band_kernel(q_ref, kp_ref, kc_ref, vp_ref, vc_ref, o_ref, l_ref, *, nb):
    qi = lax.broadcasted_iota(jnp.int32, (SPAN, 2 * SPAN), 0)
    kj = lax.broadcasted_iota(jnp.int32, (SPAN, 2 * SPAN), 1)
    band = (kj >= qi) & (kj <= qi + SPAN)
    band_first = band & (kj >= jnp.where(pl.program_id(2) == 0, SPAN, 0))
    lane = lax.broadcasted_iota(jnp.int32, (SPAN, LANES), 1)
    head0 = lane < HEAD_DIM
    for i in range(nb):
        cur = slice(i * SPAN, (i + 1) * SPAN)
        prev = slice((i - 1) * SPAN, i * SPAN)
        for hp in range(KV_DIM // LANES):
            cols = slice(hp * LANES, (hp + 1) * LANES)
            q2 = q_ref[cur, cols]
            if i == 0:
                k2 = jnp.concatenate([kp_ref[:, cols], kc_ref[cur, cols]], axis=0)
                v2 = jnp.concatenate([vp_ref[:, cols], vc_ref[cur, cols]], axis=0)
                mask = band_first
            else:
                k2 = jnp.concatenate([kc_ref[prev, cols], kc_ref[cur, cols]], axis=0)
                v2 = jnp.concatenate([vc_ref[prev, cols], vc_ref[cur, cols]], axis=0)
                mask = band
            o_pair = None
            l_pair = None
            for hh in range(2):
                sel = head0 if hh == 0 else jnp.logical_not(head0)
                qm = jnp.where(sel, q2, jnp.zeros_like(q2))
                s = lax.dot_general(qm, k2, (((1,), (1,)), ((), ())), preferred_element_type=F32)
                s = jnp.where(mask, s, NEG_INF)
                m = jnp.max(s, axis=-1, keepdims=True)
                p = jnp.exp(s - m)
                den = jnp.sum(p, axis=-1, keepdims=True)
                o = jnp.dot(p.astype(BF16), v2, preferred_element_type=F32) / den
                lse = jnp.broadcast_to(m + jnp.log(den), (SPAN, LANES))
                o_pair = o if o_pair is None else jnp.where(head0, o_pair, o)
                l_pair = lse if l_pair is None else jnp.where(head0, l_pair, lse)
            o_ref[cur, cols] = o_pair
            l_ref[cur, cols] = l_pair


def _band_call(q, k, v, nb):
    b, dil, m, _ = k.shape
    tq = nb * SPAN
    cur = pl.BlockSpec((None, None, tq, KV_DIM), lambda bb, r, n: (bb, r, n, 0))
    prev = pl.BlockSpec((None, None, SPAN, KV_DIM), lambda bb, r, n: (bb, r, jnp.maximum(n * nb - 1, 0), 0))
    return pl.pallas_call(
        functools.partial(_band_kernel, nb=nb),
        grid=(b, dil, m // tq),
        in_specs=[cur, prev, cur, prev, cur],
        out_specs=[cur, cur],
        out_shape=[jax.ShapeDtypeStruct((b, dil, m, KV_DIM), F32),
                   jax.ShapeDtypeStruct((b, dil, m, KV_DIM), F32)],
        compiler_params=_params("arbitrary", "arbitrary", "arbitrary"),
        name=f"band_attention_dil{dil}",
    )(q, k, k, v, v)


def _transpose_small(x):
    c = x.shape[1]
    eye = jnp.where(lax.broadcasted_iota(jnp.int32, (c, c), 0) == lax.broadcasted_iota(jnp.int32, (c, c), 1),
                    1.0, 0.0).astype(F32)
    return lax.dot_general(eye, x, (((1,), (1,)), ((), ())), preferred_element_type=F32, precision=HIGHEST)


def _sample_attn_kernel(q_ref, kn_ref, vn_ref, kc_ref, vc_ref, o_ref, ko_ref, vo_ref):
    cl = kc_ref.shape[-1]
    is_last = lax.broadcasted_iota(jnp.int32, (HEAD_DIM, cl), 1) == cl - 1
    kn_rows, vn_rows = kn_ref[0], vn_ref[0]
    kn_cols, vn_cols = _transpose_small(kn_rows), _transpose_small(vn_rows)
    for h in range(N_KV_HEADS):
        ko_ref[0, h] = jnp.where(is_last, kn_cols[:, h:h + 1], pltpu.roll(kc_ref[0, h], cl - 1, 1))
        vo_ref[0, h] = jnp.where(is_last, vn_cols[:, h:h + 1], pltpu.roll(vc_ref[0, h], cl - 1, 1))
    outs, lses = [], []
    for g, (win, dil) in enumerate(DILATION_GROUPS):
        q_rows = q_ref[0, g]
        q_cols = _transpose_small(q_rows)
        s = jnp.concatenate(
            [jnp.sum(kc_ref[0, h, :, cl - win:] * q_cols[:, h:h + 1], axis=0, keepdims=True)
             for h in range(N_KV_HEADS)], axis=0)
        if dil > 1:
            lane = lax.broadcasted_iota(jnp.int32, s.shape, 1)
            s = jnp.where((lane & (dil - 1)) == 0, s, NEG_INF)
        s_new = jnp.sum(kn_rows * q_rows, axis=-1, keepdims=True)
        m = jnp.maximum(jnp.max(s, axis=-1, keepdims=True), s_new)
        p = jnp.exp(s - m)
        p_new = jnp.exp(s_new - m)
        den = jnp.sum(p, axis=-1, keepdims=True) + p_new
        pn = p / den
        o_cols = jnp.concatenate(
            [jnp.sum(vc_ref[0, h, :, cl - win:] * pn[h:h + 1, :], axis=-1, keepdims=True)
             for h in range(N_KV_HEADS)], axis=1)
        outs.append(_transpose_small(o_cols) + (p_new / den) * vn_rows)
        lses.append(m + jnp.log(den))
    mx = jnp.maximum(jnp.maximum(lses[0], lses[1]), lses[2])
    es = [jnp.exp(l - mx) for l in lses]
    o_ref[0] = (es[0] * outs[0] + es[1] * outs[1] + es[2] * outs[2]) / (es[0] + es[1] + es[2])


def _sample_attn_call(q, k_new, v_new, cache_k, cache_v):
    n, nh, hd, cl = cache_k.shape
    col = pl.BlockSpec((1, nh, hd), lambda i: (i, 0, 0))
    cache = pl.BlockSpec((1, nh, hd, cl), lambda i: (i, 0, 0, 0))
    return pl.pallas_call(
        _sample_attn_kernel,
        grid=(n,),
        in_specs=[pl.BlockSpec((1, N_DIL_GROUPS, nh, hd), lambda i: (i, 0, 0, 0)), col, col,
                  cache, cache],
        out_specs=[col, cache, cache],
        out_shape=[jax.ShapeDtypeStruct((n, nh, hd), F32),
                   jax.ShapeDtypeStruct(cache_k.shape, cache_k.dtype),
                   jax.ShapeDtypeStruct(cache_v.shape, cache_v.dtype)],
        compiler_params=_params("arbitrary"),
        name="window_attention_cache_append",
    )(q, k_new, v_new, cache_k, cache_v)


def _attn_out_kernel(o1_ref, o2_ref, o3_ref, l1_ref, l2_ref, l3_ref, x_ref, wo_ref, gn_ref, gate_ref,
                     y_ref, *scrs):
    vals = []
    for ref, scr, (_, dil) in zip((o1_ref, o2_ref, o3_ref, l1_ref, l2_ref, l3_ref), scrs,
                                  DILATION_GROUPS + DILATION_GROUPS):
        if dil == 1:
            vals.append(ref[0])
            continue
        n = ref.shape[1]
        slabs = ref.shape[2] // LANES
        for r in range(dil):
            for s in range(slabs):
                scr[s, pl.ds(r, n, stride=dil), :] = ref[r, :, s * LANES:(s + 1) * LANES]
        vals.append(jnp.concatenate([scr[s] for s in range(slabs)], axis=1))
    o1, o2, o3, l1, l2, l3 = vals
    mx = jnp.maximum(jnp.maximum(l1, l2), l3)
    e1, e2, e3 = jnp.exp(l1 - mx), jnp.exp(l2 - mx), jnp.exp(l3 - mx)
    o = (e1 * o1 + e2 * o2 + e3 * o3) / (e1 + e2 + e3)
    out = _mm(o, wo_ref[...])
    y_ref[...] = x_ref[...] + gate_ref[...] * _rms(out, gn_ref[...])


def _attn_out_call(os_, ls_, x, wo, gn, gate, tm):
    bm, r, d = x.shape
    res = [pl.BlockSpec((None, dil, tm // dil, KV_DIM), lambda b, i: (b, 0, i, 0))
           for _, dil in DILATION_GROUPS]
    return pl.pallas_call(
        _attn_out_kernel,
        grid=(bm, r // tm),
        in_specs=res + res + [_rows(tm, d), _const(wo), _const(gn), _mod(gate, tm)],
        out_specs=_rows(tm, d),
        out_shape=jax.ShapeDtypeStruct((bm, r, d), F32),
        scratch_shapes=[pltpu.VMEM((KV_DIM // LANES, tm, LANES), F32)] * (2 * N_DIL_GROUPS),
        compiler_params=_params("arbitrary", "arbitrary"),
        name="attn_merge_out_proj",
    )(*os_, *ls_, x, wo, gn, gate)


def _proj_res_kernel(o_ref, x_ref, wo_ref, gn_ref, gate_ref, y_ref):
    out = _mm(o_ref[...], wo_ref[...])
    y_ref[...] = x_ref[...] + gate_ref[...] * _rms(out, gn_ref[...])


def _proj_res_call(o, x, wo, gn, gate, tm):
    bm, r, d = x.shape
    return pl.pallas_call(
        _proj_res_kernel,
        grid=(bm, r // tm),
        in_specs=[_rows(tm, o.shape[2]), _rows(tm, d), _const(wo), _const(gn), _mod(gate, tm)],
        out_specs=_rows(tm, d),
        out_shape=jax.ShapeDtypeStruct((bm, r, d), F32),
        compiler_params=_params("arbitrary", "arbitrary"),
        name="attn_out_proj",
    )(o, x, wo, gn, gate)


def _top2(logits):
    lane = lax.broadcasted_iota(jnp.int32, logits.shape, 1).astype(F32)
    m1 = jnp.max(logits, axis=-1, keepdims=True)
    i1 = jnp.min(jnp.where(logits == m1, lane, float(LANES)), axis=-1, keepdims=True)
    rest = jnp.where(lane == i1, -jnp.inf, logits)
    m2 = jnp.max(rest, axis=-1, keepdims=True)
    i2 = jnp.min(jnp.where(rest == m2, lane, float(LANES)), axis=-1, keepdims=True)
    e = jnp.exp(m2 - m1)
    return lane, i1, i2, 1.0 / (1.0 + e), e / (1.0 + e)


def _route_kernel(x_ref, g_ref, sc_ref, sh_ref, wr_ref, br_ref, h_ref, route_ref, cnt_ref, carry_scr,
                  *, stride):
    @pl.when((pl.program_id(0) == 0) & (pl.program_id(1) == 0))
    def _():
        carry_scr[...] = jnp.zeros_like(carry_scr)

    h = _rms(x_ref[...], g_ref[...]) * (1.0 + sc_ref[...]) + sh_ref[...]
    tm = h.shape[0]
    for c in range(SLABS):
        h_ref[pl.ds(c, tm, stride=SUBLANES), :] = h[:, c * LANES:(c + 1) * LANES]
    logits = _mm(h, wr_ref[...]) + br_ref[...]
    lane, i1, i2, g1, g2 = _top2(logits)
    onehot = jnp.where(lane == i1, 1.0, 0.0) + jnp.where(lane == i2, 1.0, 0.0)
    row = lax.broadcasted_iota(jnp.int32, (tm, tm), 0)
    col = lax.broadcasted_iota(jnp.int32, (tm, tm), 1)
    earlier = jnp.where(col < row, 1.0, 0.0).astype(BF16)
    before = jnp.dot(earlier, onehot.astype(BF16), preferred_element_type=F32) + carry_scr[0:1, :]
    r1 = jnp.sum(jnp.where(lane == i1, before, 0.0), axis=-1, keepdims=True)
    r2 = jnp.sum(jnp.where(lane == i2, before, 0.0), axis=-1, keepdims=True)
    p1 = i1 * float(stride) + r1
    p2 = i2 * float(stride) + r2
    route_ref[...] = (jnp.where(lane == 0.0, p1, 0.0) + jnp.where(lane == 1.0, p2, 0.0)
                      + jnp.where(lane == 2.0, g1, 0.0) + jnp.where(lane == 3.0, g2, 0.0))
    carry = carry_scr[...] + jnp.sum(onehot, axis=0, keepdims=True)
    carry_scr[...] = carry
    cnt_ref[...] = carry


def _route_call(x, g, sc, sh, wr, br, tm, stride):
    bm, r, d = x.shape
    nt = r // tm
    return pl.pallas_call(
        functools.partial(_route_kernel, stride=stride),
        grid=(bm, nt),
        in_specs=[_rows(tm, d), _const(g), _mod(sc, tm), _mod(sh, tm), _const(wr), _const(br)],
        out_specs=[pl.BlockSpec((tm * SUBLANES, LANES), lambda b, i: (b * nt + i, 0)),
                   _rows(tm, LANES),
                   pl.BlockSpec((SUBLANES, LANES), lambda b, i: (0, 0))],
        out_shape=[jax.ShapeDtypeStruct((bm * r * SUBLANES, LANES), F32),
                   jax.ShapeDtypeStruct((bm, r, LANES), F32),
                   jax.ShapeDtypeStruct((SUBLANES, LANES), F32)],
        scratch_shapes=[pltpu.VMEM((SUBLANES, LANES), F32)],
        compiler_params=_params("arbitrary", "arbitrary"),
        name="moe_route",
    )(x, g, sc, sh, wr, br)


def _token_tile(ref, idx):
    return ref.at[pl.ds(pl.multiple_of(idx * SUBLANES, SUBLANES), SUBLANES), :]


def _dispatch_kernel(cnt_ref, pos_ref, h_ref, xs_hbm, zero_scr, sem, zsem, *, tm, stride, tmx):
    i = pl.program_id(0)

    def issue(t, carry):
        src = _token_tile(h_ref, t)
        for k in range(2):
            pltpu.make_async_copy(src, _token_tile(xs_hbm, pos_ref[0, k * tm + t]), sem).start(priority=k)
        return carry

    lax.fori_loop(0, tm, issue, 0)
    for _ in range(2):
        pltpu.make_async_copy(h_ref, xs_hbm.at[pl.ds(0, tm * SUBLANES), :], sem).wait()

    @pl.when(i == pl.num_programs(0) - 1)
    def _():
        zero_scr[...] = jnp.zeros_like(zero_scr)
        copies = []
        for e in range(N_EXPERTS):
            start = pl.multiple_of((e * stride + cnt_ref[e]) * SUBLANES, SUBLANES)
            copies.append(pltpu.make_async_copy(zero_scr, xs_hbm.at[pl.ds(start, tmx * SUBLANES), :], zsem))
            copies[-1].start()
        for cp in copies:
            cp.wait()


def _dispatch_call(cnt, pos, h2d, tm, stride, tmx):
    n_tiles = pos.shape[0]
    rows = N_EXPERTS * stride * SUBLANES
    return pl.pallas_call(
        functools.partial(_dispatch_kernel, tm=tm, stride=stride, tmx=tmx),
        grid_spec=pltpu.PrefetchScalarGridSpec(
            num_scalar_prefetch=1,
            grid=(n_tiles,),
            in_specs=[pl.BlockSpec((None, 1, 2 * tm), lambda i, cnt: (i, 0, 0), memory_space=pltpu.SMEM),
                      pl.BlockSpec((tm * SUBLANES, LANES), lambda i, cnt: (i, 0))],
            out_specs=pl.BlockSpec(memory_space=pl.ANY),
            scratch_shapes=[pltpu.VMEM((tmx * SUBLANES, LANES), F32),
                            pltpu.SemaphoreType.DMA(()), pltpu.SemaphoreType.DMA(())],
        ),
        out_shape=jax.ShapeDtypeStruct((rows, LANES), F32),
        compiler_params=_params("arbitrary"),
        name="moe_dispatch",
    )(cnt, pos, h2d)


def _expert_kernel(blk_ref, exp_ref, nused_ref, xs_ref, wg_ref, wu_ref, wd_ref, o_ref, x_scr, acc_scr,
                   *, tmx):
    j = pl.program_id(0)
    f = pl.program_id(1)

    @pl.when(j < nused_ref[0])
    def _():
        @pl.when(f == 0)
        def _():
            for c in range(SLABS):
                x_scr[:, c * LANES:(c + 1) * LANES] = (
                    xs_ref[pl.ds(c, tmx, stride=SUBLANES), :].astype(x_scr.dtype))
            acc_scr[...] = jnp.zeros_like(acc_scr)

        x = x_scr[...]
        he = _silu(_mm(x, wg_ref[...])) * _mm(x, wu_ref[...])
        acc_scr[...] += _mm(he, wd_ref[...])

        @pl.when(f == pl.num_programs(1) - 1)
        def _():
            for c in range(SLABS):
                o_ref[pl.ds(c, tmx, stride=SUBLANES), :] = acc_scr[:, c * LANES:(c + 1) * LANES]


def _expert_call(blk, exp, nused, xs, wg, wu, wd, tmx, tf):
    ne, d, ff = wg.shape
    nf = ff // tf

    def fidx(j, f, nused):
        return jnp.where(j < nused[0], f, nf - 1)

    row_spec = pl.BlockSpec((tmx * SUBLANES, LANES), lambda j, f, blk, exp, nu: (blk[j], 0))
    return pl.pallas_call(
        functools.partial(_expert_kernel, tmx=tmx),
        grid_spec=pltpu.PrefetchScalarGridSpec(
            num_scalar_prefetch=3,
            grid=(blk.shape[0], nf),
            in_specs=[row_spec,
                      pl.BlockSpec((None, d, tf), lambda j, f, blk, exp, nu: (exp[j], 0, fidx(j, f, nu))),
                      pl.BlockSpec((None, d, tf), lambda j, f, blk, exp, nu: (exp[j], 0, fidx(j, f, nu))),
                      pl.BlockSpec((None, tf, d), lambda j, f, blk, exp, nu: (exp[j], fidx(j, f, nu), 0))],
            out_specs=row_spec,
            scratch_shapes=[pltpu.VMEM((tmx, d), wg.dtype), pltpu.VMEM((tmx, d), F32)],
        ),
        out_shape=jax.ShapeDtypeStruct(xs.shape, F32),
        compiler_params=_params("arbitrary", "arbitrary"),
        name="moe_grouped_experts",
    )(blk, exp, nused, xs, wg, wu, wd)


def _combine_kernel(pos_ref, route_ref, x_ref, gn_ref, gate_ref, out_hbm, y_ref, buf, sem, *, tm):
    def issue(t, carry):
        for k in range(2):
            pltpu.make_async_copy(_token_tile(out_hbm, pos_ref[0, k * tm + t]),
                                  _token_tile(buf, k * tm + t), sem).start(priority=k)
        return carry

    lax.fori_loop(0, tm, issue, 0)
    pltpu.make_async_copy(out_hbm.at[pl.ds(0, 2 * tm * SUBLANES), :], buf, sem).wait()
    route = route_ref[...]
    g1, g2 = route[:, 2:3], route[:, 3:4]
    parts = []
    for c in range(SLABS):
        parts.append(g1 * buf[pl.ds(c, tm, stride=SUBLANES), :]
                     + g2 * buf[pl.ds(tm * SUBLANES + c, tm, stride=SUBLANES), :])
    y = jnp.concatenate(parts, axis=1)
    y_ref[...] = x_ref[...] + gate_ref[...] * _rms(y, gn_ref[...])


def _combine_call(pos, route, x, gn, gate, out2d, tm):
    bm, r, d = x.shape
    nt = r // tm
    return pl.pallas_call(
        functools.partial(_combine_kernel, tm=tm),
        grid=(bm, nt),
        in_specs=[pl.BlockSpec((None, 1, 2 * tm), lambda b, i: (b * nt + i, 0, 0), memory_space=pltpu.SMEM),
                  _rows(tm, LANES), _rows(tm, d), _const(gn), _mod(gate, tm),
                  pl.BlockSpec(memory_space=pl.ANY)],
        out_specs=_rows(tm, d),
        out_shape=jax.ShapeDtypeStruct((bm, r, d), F32),
        scratch_shapes=[pltpu.VMEM((2 * tm * SUBLANES, LANES), F32), pltpu.SemaphoreType.DMA(())],
        compiler_params=_params("arbitrary", "arbitrary"),
        name="moe_combine",
    )(pos, route, x, gn, gate, out2d)


def _moe_sparse(x, g, sc, sh, wr, br, wg, wu, wd, gn, gate, tm):
    bm, r, d = x.shape
    m = bm * r
    tmx = MOE_TILE_ROWS
    stride = m + tmx
    n_tiles = 2 * m // tmx + N_EXPERTS
    h2d, route, cnt = _route_call(x, g, sc, sh, wr, br, tm, stride)
    cnt = cnt[0, :N_EXPERTS].astype(jnp.int32)
    pos = route[..., :2].astype(jnp.int32).reshape(m // tm, tm, 2).transpose(0, 2, 1).reshape(m // tm, 1, 2 * tm)
    xs = _dispatch_call(cnt, pos, h2d, tm, stride, tmx)
    per = (cnt + tmx - 1) // tmx
    ends = jnp.cumsum(per)
    j = jnp.minimum(jnp.arange(n_tiles, dtype=jnp.int32), ends[-1] - 1)
    exp = jnp.sum((j[:, None] >= ends[None, :]).astype(jnp.int32), axis=1)
    blk = exp * (stride // tmx) + j - (ends - per)[exp]
    out2d = _expert_call(blk, exp, ends[-1:], xs, wg, wu, wd, tmx, wg.shape[2] // 2)
    return _combine_call(pos, route, x, gn, gate, out2d, tm)


def _router_kernel(x_ref, g_ref, sc_ref, sh_ref, wr_ref, br_ref, h_ref, comb_ref):
    h = _rms(x_ref[...], g_ref[...]) * (1.0 + sc_ref[...]) + sh_ref[...]
    h_ref[...] = h.astype(h_ref.dtype)
    logits = _mm(h, wr_ref[...]) + br_ref[...]
    lane, i1, i2, g1, g2 = _top2(logits)
    comb_ref[...] = jnp.where(lane == i1, g1, 0.0) + jnp.where(lane == i2, g2, 0.0)


def _router_call(x, g, sc, sh, wr, br, h_dtype, tm):
    bm, r, d = x.shape
    return pl.pallas_call(
        _router_kernel,
        grid=(bm, r // tm),
        in_specs=[_rows(tm, d), _const(g), _mod(sc, tm), _mod(sh, tm), _const(wr), _const(br)],
        out_specs=[_rows(tm, d), _rows(tm, LANES)],
        out_shape=[jax.ShapeDtypeStruct((bm, r, d), h_dtype),
                   jax.ShapeDtypeStruct((bm, r, LANES), F32)],
        compiler_params=_params("arbitrary", "arbitrary"),
        name="moe_router",
    )(x, g, sc, sh, wr, br)


def _moe_kernel(h_ref, comb_ref, x_ref, wg_ref, wu_ref, wd_ref, gn_ref, gate_ref, o_ref, acc_scr):
    e = pl.program_id(2)
    f = pl.program_id(3)

    @pl.when((e == 0) & (f == 0))
    def _():
        acc_scr[...] = jnp.zeros_like(acc_scr)

    comb = comb_ref[...]
    lane = lax.broadcasted_iota(jnp.int32, comb.shape, 1)
    cw = jnp.sum(jnp.where(lane == e, comb, 0.0), axis=-1, keepdims=True)
    h = h_ref[...]
    he = _silu(_mm(h, wg_ref[...], False)) * _mm(h, wu_ref[...], False)
    acc_scr[...] += cw * _mm(he, wd_ref[...], False)

    @pl.when((e == pl.num_programs(2) - 1) & (f == pl.num_programs(3) - 1))
    def _():
        o_ref[...] = x_ref[...] + gate_ref[...] * _rms(acc_scr[...], gn_ref[...])


def _moe_call(h, comb, x, wg, wu, wd, gn, gate, tm, tf):
    bm, r, d = x.shape
    ne, _, ff = wg.shape
    return pl.pallas_call(
        _moe_kernel,
        grid=(bm, r // tm, ne, ff // tf),
        in_specs=[_rows(tm, d), _rows(tm, LANES), _rows(tm, d),
                  pl.BlockSpec((None, d, tf), lambda b, i, e, f: (e, 0, f)),
                  pl.BlockSpec((None, d, tf), lambda b, i, e, f: (e, 0, f)),
                  pl.BlockSpec((None, tf, d), lambda b, i, e, f: (e, f, 0)),
                  _const(gn), _mod(gate, tm)],
        out_specs=_rows(tm, d),
        out_shape=jax.ShapeDtypeStruct((bm, r, d), F32),
        scratch_shapes=[pltpu.VMEM((tm, d), F32)],
        compiler_params=_params("arbitrary", "arbitrary", "arbitrary", "arbitrary"),
        name="moe_experts",
    )(h, comb, x, wg, wu, wd, gn, gate)


def _ssm_params(lam_re, lam_im, log_dt, b_re, b_im, c_re, c_im, wdtype):
    dt = jnp.exp(log_dt)[:, None]
    mag = jnp.exp(lam_re * dt)
    a_re, a_im = mag * jnp.cos(lam_im * dt), mag * jnp.sin(lam_im * dt)
    den = lam_re * lam_re + lam_im * lam_im
    f_re = ((a_re - 1.0) * lam_re + a_im * lam_im) / den
    f_im = (a_im * lam_re - (a_re - 1.0) * lam_im) / den
    bb_re = f_re[..., None] * b_re - f_im[..., None] * b_im
    bb_im = f_re[..., None] * b_im + f_im[..., None] * b_re
    eye = jnp.eye(GROUPS_PER_SLAB, dtype=F32)

    def in_map(bb):
        t = bb.reshape(SLABS, GROUPS_PER_SLAB, SSM_STATE, SSM_CH)
        w = jnp.einsum('sgpc,gh->sgchp', t, eye)
        return w.reshape(SLABS, LANES, SLAB_STATES).astype(wdtype)

    def out_map(cc):
        t = cc.reshape(SLABS, GROUPS_PER_SLAB, SSM_CH, SSM_STATE)
        w = jnp.einsum('sgcp,gh->sgphc', t, eye)
        return w.reshape(SLABS, SLAB_STATES, LANES).astype(wdtype)

    return a_re, a_im, in_map(bb_re), in_map(bb_im), out_map(c_re), out_map(c_im)


def _rope_tables(pos):
    half = ROT_DIM // 2
    inv = ROPE_THETA ** (-jnp.arange(half, dtype=F32) / half)
    ang = pos.astype(F32)[:, None] * inv[None, :]
    cos, sin = jnp.cos(ang), jnp.sin(ang)
    n = pos.shape[0]
    pad = jnp.zeros((n, HEAD_DIM - ROT_DIM), F32)
    zero = jnp.zeros((n, half), F32)
    cos_h = jnp.concatenate([cos, cos, pad + 1.0], axis=1)
    lo_h = jnp.concatenate([-sin, zero, pad], axis=1)
    hi_h = jnp.concatenate([zero, sin, pad], axis=1)
    rep = LANES // HEAD_DIM
    return jnp.tile(cos_h, (1, rep)), jnp.tile(lo_h, (1, rep)), jnp.tile(hi_h, (1, rep))


def _trunk(x, mods, pos, wts, prompt, ssm0=None, cache=None):
    bm, r, d = x.shape
    tm = 512 if prompt else r
    wd = BF16 if prompt else F32
    cast = lambda w: w.astype(wd)
    norm_g = wts['norm_g'].reshape(DEPTH, 4, 1, d)

    sh1, sc1, g1, sh2, sc2, g2 = mods[0]
    u = _inproj_call(x, norm_g[0, 0], sc1, sh1, cast(wts['ssm_w_in'][0]), tm)
    a_re, a_im, wbr, wbi, cr, ci = _ssm_params(
        wts['ssm_lam_re'][0], wts['ssm_lam_im'][0], wts['ssm_log_dt'][0], wts['ssm_b_re'][0],
        wts['ssm_b_im'][0], wts['ssm_c_re'][0], wts['ssm_c_im'][0], wd)
    dvec = wts['ssm_d'][0].reshape(1, d)
    if prompt:
        z, fr, fi = _ssm_prompt_call(u, wbr, wbi, a_re.reshape(STATE_TILES, LANES),
                                     a_im.reshape(STATE_TILES, LANES), cr, ci, dvec, SCAN_CHUNK)
    else:
        z, fr, fi = _ssm_sample_call(u[0], ssm0[0].reshape(r, SSM_FLAT), ssm0[1].reshape(r, SSM_FLAT),
                                     wbr, wbi, a_re.reshape(1, SSM_FLAT), a_im.reshape(1, SSM_FLAT),
                                     cr, ci, dvec)
        z = z[None]
    fr = fr.reshape(-1, SSM_GROUPS, SSM_STATE)[None]
    fi = fi.reshape(-1, SSM_GROUPS, SSM_STATE)[None]
    x = _s5out_call(z, x, cast(wts['ssm_w_glu'][0]), cast(wts['ssm_w_out'][0]), norm_g[0, 1], g1, tm)
    tf = wts['ffn_w_gate'].shape[2] // 2
    x = _ffn_call(x, norm_g[0, 2], sc2, sh2, cast(wts['ffn_w_gate'][0]), cast(wts['ffn_w_up'][0]),
                  cast(wts['ffn_w_down'][0]), norm_g[0, 3], g2, tm, tf)

    sh1, sc1, g1, sh2, sc2, g2 = mods[1]
    cos, slo, shi = _rope_tables(pos)
    qkv = _qkv_call(x, wts['kv_norm_g'].reshape(1, d), norm_g[1, 0], sc1, sh1, cast(wts['w_kv']),
                    cast(wts['attn_w_q'][0]), cos, slo, shi, tm, prompt)
    wo = cast(wts['attn_w_o'][0])
    wr = cast(jnp.zeros((d, LANES), F32).at[:, :N_EXPERTS].set(wts['moe_w_router'][0]))
    br = jnp.full((1, LANES), NEG_INF, F32).at[0, :N_EXPERTS].set(wts['moe_b_router'][0])
    wg, wu, wdn = (wts[n][0].astype(BF16) for n in ('moe_w_gate', 'moe_w_up', 'moe_w_down'))
    if prompt:
        k, v, kd, vd, qd = qkv
        os_, ls_ = [], []
        for g, (_, dil) in enumerate(DILATION_GROUPS):
            o, lse = _band_call(qd[g], kd[g], vd[g], min(4, r // dil // SPAN))
            os_.append(o)
            ls_.append(lse)
        x = _attn_out_call(os_, ls_, x, wo, norm_g[1, 1], g1, tm)
        x = _moe_sparse(x, norm_g[1, 2], sc2, sh2, wr, br, wg, wu, wdn, norm_g[1, 3], g2, tm)
    else:
        k, v, q = qkv
        o, k, v = _sample_attn_call(q[0].reshape(r, N_DIL_GROUPS, N_KV_HEADS, HEAD_DIM),
                                    k[0].reshape(r, N_KV_HEADS, HEAD_DIM),
                                    v[0].reshape(r, N_KV_HEADS, HEAD_DIM),
                                    cache[0].transpose(0, 2, 3, 1), cache[1].transpose(0, 2, 3, 1))
        k, v = k.transpose(0, 3, 1, 2), v.transpose(0, 3, 1, 2)
        x = _proj_res_call(o.reshape(1, r, KV_DIM), x, wo, norm_g[1, 1], g1, tm)
        h, comb = _router_call(x, norm_g[1, 2], sc2, sh2, wr, br, wd, tm)
        x = _moe_call(h, comb, x, wg, wu, wdn, norm_g[1, 3], g2, tm, wg.shape[2] // 4)
    return x, fr, fi, k, v


def kernel(x_prompt, x_sample, state_ssm_re, state_ssm_im, cache_k, cache_v, c_prompt, c_sample, ada_w, ada_b, norm_g, ssm_w_in, ssm_lam_re, ssm_lam_im, ssm_log_dt, ssm_b_re, ssm_b_im, ssm_c_re, ssm_c_im, ssm_d, ssm_w_glu, ssm_w_out, kv_norm_g, w_kv, attn_w_q, attn_w_o, ffn_w_gate, ffn_w_up, ffn_w_down, moe_w_router, moe_b_router, moe_w_gate, moe_w_up, moe_w_down):
    wts = dict(norm_g=norm_g, ssm_w_in=ssm_w_in, ssm_lam_re=ssm_lam_re, ssm_lam_im=ssm_lam_im,
               ssm_log_dt=ssm_log_dt, ssm_b_re=ssm_b_re, ssm_b_im=ssm_b_im, ssm_c_re=ssm_c_re,
               ssm_c_im=ssm_c_im, ssm_d=ssm_d, ssm_w_glu=ssm_w_glu, ssm_w_out=ssm_w_out,
               kv_norm_g=kv_norm_g, w_kv=w_kv, attn_w_q=attn_w_q, attn_w_o=attn_w_o,
               ffn_w_gate=ffn_w_gate, ffn_w_up=ffn_w_up, ffn_w_down=ffn_w_down,
               moe_w_router=moe_w_router, moe_b_router=moe_b_router, moe_w_gate=moe_w_gate,
               moe_w_up=moe_w_up, moe_w_down=moe_w_down)
    nb, seq, d = x_prompt.shape
    ns = x_sample.shape[0]
    assert x_sample.shape[1] == 1 and ns % SUBLANES == 0

    pad = (-(ns + nb)) % SUBLANES
    c_all = jnp.concatenate([c_sample, c_prompt, jnp.zeros((pad, d), F32)], axis=0)
    mod_all = _ada_call(c_all, ada_w, ada_b)
    mods_s = [[mod_all[l, k, :ns][None] for k in range(6)] for l in range(DEPTH)]
    mods_p = [[mod_all[l, k, ns:ns + nb][:, None, :] for k in range(6)] for l in range(DEPTH)]

    pos_p = jnp.arange(seq, dtype=jnp.int32)
    pos_s = jnp.full((ns,), PAST_LEN, dtype=jnp.int32)
    y_p, sr_p, si_p, k_p, v_p = _trunk(x_prompt, mods_p, pos_p, wts, True)
    xs = x_sample.reshape(1, ns, d)
    y_s, sr_s, si_s, k_s, v_s = _trunk(xs, mods_s, pos_s, wts, False,
                                       ssm0=(state_ssm_re[0], state_ssm_im[0]), cache=(cache_k, cache_v))

    keep = min(MAX_WINDOW, seq)
    k_rows_p = k_p[:, seq - keep:].reshape(nb, keep, N_KV_HEADS, HEAD_DIM)
    v_rows_p = v_p[:, seq - keep:].reshape(nb, keep, N_KV_HEADS, HEAD_DIM)
    return (y_p, y_s.reshape(ns, 1, d), sr_p, si_p, sr_s, si_s, k_rows_p, v_rows_p, k_s, v_s)
```

```python
import functools
import math

import jax
import jax.numpy as jnp
from jax import lax
from jax.experimental import pallas as pl
from jax.experimental.pallas import tpu as pltpu

F32 = jnp.float32
BF16 = jnp.bfloat16
HIGHEST = lax.Precision.HIGHEST

D_MODEL = 1024
DEPTH = 2
PAST_LEN = 16384
SSM_CH = 16
SSM_GROUPS = D_MODEL // SSM_CH
SSM_STATE = 64
SSM_FLAT = SSM_GROUPS * SSM_STATE
HEAD_DIM = 64
N_KV_HEADS = D_MODEL // 128
KV_DIM = N_KV_HEADS * HEAD_DIM
DILATION_GROUPS = ((128, 1), (512, 4), (2048, 16))
N_DIL_GROUPS = len(DILATION_GROUPS)
SPAN = 128
MAX_WINDOW = 2048
ROT_DIM = HEAD_DIM // 4
ROPE_THETA = 500000.0
N_EXPERTS = 8
RMS_EPS = 1e-6
NEG_INF = -1e30

LANES = 128
SUBLANES = 8
SLABS = D_MODEL // LANES
GROUPS_PER_SLAB = LANES // SSM_CH
SLAB_STATES = GROUPS_PER_SLAB * SSM_STATE
STATE_TILES = SSM_FLAT // LANES
SCAN_STEP_PITCH = 2
SCAN_TILE_PITCH = 15
SCAN_SLAB_ROWS = 480
SCAN_CHUNK = 128
MOE_TILE_ROWS = 512
VMEM_LIMIT = 56 * 1024 * 1024


def _params(*sem):
    return pltpu.CompilerParams(dimension_semantics=sem, vmem_limit_bytes=VMEM_LIMIT)


def _mm(a, w, precise=True):
    if w.dtype == BF16 or not precise:
        return jnp.dot(a.astype(BF16), w.astype(BF16), preferred_element_type=F32)
    return jnp.dot(a.astype(F32), w, preferred_element_type=F32, precision=HIGHEST)


def _rms(x, g):
    return x * lax.rsqrt(jnp.mean(x * x, axis=-1, keepdims=True) + RMS_EPS) * g


def _sigmoid(x):
    return 1.0 / (1.0 + jnp.exp(-x))


def _silu(x):
    return x * _sigmoid(x)


def _gelu_tanh(x):
    return 0.5 * x * (1.0 + jnp.tanh(math.sqrt(2.0 / math.pi) * (x + 0.044715 * (x * x * x))))


def _rows(tm, n):
    return pl.BlockSpec((None, tm, n), lambda b, i, *_: (b, i, 0))


def _mod(arr, tm):
    if arr.shape[1] == 1:
        return pl.BlockSpec((None, 1, arr.shape[2]), lambda b, i, *_: (b, 0, 0))
    return pl.BlockSpec((None, tm, arr.shape[2]), lambda b, i, *_: (b, i, 0))


def _const(arr):
    nd = arr.ndim
    return pl.BlockSpec(arr.shape, lambda *_: (0,) * nd)


def _ada_kernel(c_ref, w_ref, b_ref, o_ref):
    o_ref[...] = _mm(_silu(c_ref[...]), w_ref[...]) + b_ref[...]


def _ada_call(c_all, ada_w, ada_b):
    rows = c_all.shape[0]
    b4 = ada_b.reshape(DEPTH, 6, 1, D_MODEL)
    return pl.pallas_call(
        _ada_kernel,
        grid=(DEPTH, 6),
        in_specs=[pl.BlockSpec((rows, D_MODEL), lambda l, k: (0, 0)),
                  pl.BlockSpec((None, D_MODEL, D_MODEL), lambda l, k: (l, 0, k)),
                  pl.BlockSpec((None, None, 1, D_MODEL), lambda l, k: (l, k, 0, 0))],
        out_specs=pl.BlockSpec((None, None, rows, D_MODEL), lambda l, k: (l, k, 0, 0)),
        out_shape=jax.ShapeDtypeStruct((DEPTH, 6, rows, D_MODEL), F32),
        compiler_params=_params("arbitrary", "arbitrary"),
        name="ada_modulation",
    )(c_all, ada_w, b4)


def _inproj_kernel(x_ref, g_ref, sc_ref, sh_ref, w_ref, u_ref):
    h = _rms(x_ref[...], g_ref[...]) * (1.0 + sc_ref[...]) + sh_ref[...]
    u_ref[...] = _mm(h, w_ref[...])


def _inproj_call(x, g, sc, sh, w, tm):
    bm, r, d = x.shape
    return pl.pallas_call(
        _inproj_kernel,
        grid=(bm, r // tm),
        in_specs=[_rows(tm, d), _const(g), _mod(sc, tm), _mod(sh, tm), _const(w)],
        out_specs=_rows(tm, w.shape[1]),
        out_shape=jax.ShapeDtypeStruct((bm, r, w.shape[1]), F32),
        compiler_params=_params("arbitrary", "arbitrary"),
        name="s5_in_proj",
    )(x, g, sc, sh, w)


def _ssm_prompt_kernel(u_ref, wbr_ref, wbi_ref, ar_ref, ai_ref, cr_ref, ci_ref, d_ref,
                       z_ref, fr_ref, fi_ref, br_scr, bi_scr, sr_scr, si_scr, *, t_chunk, nseq):
    tb_n = t_chunk // SUBLANES
    tiles = SLAB_STATES // LANES

    def tile_rows(j):
        return pl.ds(SCAN_TILE_PITCH * j, SUBLANES, stride=SCAN_STEP_PITCH)

    @pl.when(pl.program_id(0) == 0)
    def _():
        sr_scr[...] = jnp.zeros_like(sr_scr)
        si_scr[...] = jnp.zeros_like(si_scr)

    ub = u_ref[...].reshape(nseq * t_chunk, D_MODEL).astype(BF16)
    for s in range(SLABS):
        us = ub[:, s * LANES:(s + 1) * LANES]
        bre = jnp.dot(us, wbr_ref[s], preferred_element_type=F32)
        bim = jnp.dot(us, wbi_ref[s], preferred_element_type=F32)
        for jj in range(tiles):
            cols = slice(jj * LANES, (jj + 1) * LANES)
            for q in range(nseq):
                for tb in range(tb_n):
                    rows = slice((q * tb_n + tb) * SUBLANES, (q * tb_n + tb + 1) * SUBLANES)
                    br_scr[q, tb, tile_rows(s * tiles + jj), :] = bre[rows, cols]
                    bi_scr[q, tb, tile_rows(s * tiles + jj), :] = bim[rows, cols]

    ar = ar_ref[...]
    ai = ai_ref[...]

    def body(tb, carry):
        carry = list(carry)
        for r in range(SUBLANES):
            step = pl.ds(SCAN_STEP_PITCH * r, STATE_TILES, stride=SCAN_TILE_PITCH)
            for q in range(nseq):
                sr, si = carry[q]
                nsr = ar * sr - ai * si + br_scr[q, tb, step, :]
                nsi = ar * si + ai * sr + bi_scr[q, tb, step, :]
                carry[q] = (nsr, nsi)
                br_scr[q, tb, step, :] = nsr
                bi_scr[q, tb, step, :] = nsi
        return tuple(carry)

    final = lax.fori_loop(0, tb_n, body, tuple((sr_scr[q], si_scr[q]) for q in range(nseq)))
    for q in range(nseq):
        sr_scr[q], si_scr[q] = final[q]
        fr_ref[q], fi_ref[q] = final[q]

    def states(scr, j):
        return jnp.concatenate([scr[q, tb, tile_rows(j), :] for q in range(nseq) for tb in range(tb_n)],
                               axis=0).astype(BF16)

    for s in range(SLABS):
        y = None
        for jj in range(tiles):
            cols = slice(jj * LANES, (jj + 1) * LANES)
            part = (jnp.dot(states(br_scr, s * tiles + jj), cr_ref[s, cols, :], preferred_element_type=F32)
                    - jnp.dot(states(bi_scr, s * tiles + jj), ci_ref[s, cols, :], preferred_element_type=F32))
            y = part if y is None else y + part
        sl = slice(s * LANES, (s + 1) * LANES)
        for q in range(nseq):
            yq = y[q * t_chunk:(q + 1) * t_chunk]
            z_ref[q, :, sl] = _gelu_tanh(yq + d_ref[:, sl] * u_ref[q, :, sl])


def _ssm_prompt_call(u, wbr, wbi, ar, ai, cr, ci, d, t_chunk):
    b, l, dm = u.shape
    kern = functools.partial(_ssm_prompt_kernel, t_chunk=t_chunk, nseq=b)
    seq_spec = pl.BlockSpec((b, t_chunk, dm), lambda c: (0, c, 0))
    state_spec = pl.BlockSpec((b, STATE_TILES, LANES), lambda c: (0, 0, 0))
    scr = pltpu.VMEM((b, t_chunk // SUBLANES, SCAN_SLAB_ROWS, LANES), F32)
    carry = pltpu.VMEM((b, STATE_TILES, LANES), F32)
    return pl.pallas_call(
        kern,
        grid=(l // t_chunk,),
        in_specs=[seq_spec, _const(wbr), _const(wbi), _const(ar), _const(ai),
                  _const(cr), _const(ci), _const(d)],
        out_specs=[seq_spec, state_spec, state_spec],
        out_shape=[jax.ShapeDtypeStruct((b, l, dm), F32),
                   jax.ShapeDtypeStruct((b, STATE_TILES, LANES), F32),
                   jax.ShapeDtypeStruct((b, STATE_TILES, LANES), F32)],
        scratch_shapes=[scr, scr, carry, carry],
        compiler_params=_params("arbitrary"),
        name="s5_scan_prompt",
    )(u, wbr, wbi, ar, ai, cr, ci, d)


def _ssm_sample_kernel(u_ref, x0r_ref, x0i_ref, wbr_ref, wbi_ref, ar_ref, ai_ref, cr_ref, ci_ref,
                       d_ref, z_ref, fr_ref, fi_ref):
    for s in range(SLABS):
        sl = slice(s * LANES, (s + 1) * LANES)
        st = slice(s * SLAB_STATES, (s + 1) * SLAB_STATES)
        us = u_ref[:, sl]
        a_r, a_i = ar_ref[:, st], ai_ref[:, st]
        x0r, x0i = x0r_ref[:, st], x0i_ref[:, st]
        sr = _mm(us, wbr_ref[s]) + a_r * x0r - a_i * x0i
        si = _mm(us, wbi_ref[s]) + a_r * x0i + a_i * x0r
        fr_ref[:, st] = sr
        fi_ref[:, st] = si
        y = _mm(sr, cr_ref[s]) - _mm(si, ci_ref[s])
        z_ref[:, sl] = _gelu_tanh(y + d_ref[:, sl] * us)


def _ssm_sample_call(u, x0r, x0i, wbr, wbi, ar, ai, cr, ci, d):
    n = u.shape[0]
    args = (u, x0r, x0i, wbr, wbi, ar, ai, cr, ci, d)
    return pl.pallas_call(
        _ssm_sample_kernel,
        grid=(1,),
        in_specs=[_const(a) for a in args],
        out_specs=[pl.BlockSpec((n, D_MODEL), lambda i: (0, 0)),
                   pl.BlockSpec((n, SSM_FLAT), lambda i: (0, 0)),
                   pl.BlockSpec((n, SSM_FLAT), lambda i: (0, 0))],
        out_shape=[jax.ShapeDtypeStruct((n, D_MODEL), F32),
                   jax.ShapeDtypeStruct((n, SSM_FLAT), F32),
                   jax.ShapeDtypeStruct((n, SSM_FLAT), F32)],
        compiler_params=_params("arbitrary"),
        name="s5_step_sample",
    )(*args)


def _s5out_kernel(z_ref, x_ref, wg_ref, wo_ref, gn_ref, gate_ref, o_ref):
    z = z_ref[...]
    gl = z * _sigmoid(_mm(z, wg_ref[...]))
    out = _mm(gl, wo_ref[...])
    o_ref[...] = x_ref[...] + gate_ref[...] * _rms(out, gn_ref[...])


def _s5out_call(z, x, wg, wo, gn, gate, tm):
    bm, r, d = x.shape
    return pl.pallas_call(
        _s5out_kernel,
        grid=(bm, r // tm),
        in_specs=[_rows(tm, d), _rows(tm, d), _const(wg), _const(wo), _const(gn), _mod(gate, tm)],
        out_specs=_rows(tm, d),
        out_shape=jax.ShapeDtypeStruct((bm, r, d), F32),
        compiler_params=_params("arbitrary", "arbitrary"),
        name="s5_glu_out_proj",
    )(z, x, wg, wo, gn, gate)


def _ffn_kernel(x_ref, g_ref, sc_ref, sh_ref, wg_ref, wu_ref, wd_ref, gn_ref, gate_ref, o_ref,
                h_scr, acc_scr):
    f = pl.program_id(2)

    @pl.when(f == 0)
    def _():
        h = _rms(x_ref[...], g_ref[...]) * (1.0 + sc_ref[...]) + sh_ref[...]
        h_scr[...] = h.astype(h_scr.dtype)
        acc_scr[...] = jnp.zeros_like(acc_scr)

    h = h_scr[...]
    he = _silu(_mm(h, wg_ref[...])) * _mm(h, wu_ref[...])
    acc_scr[...] += _mm(he, wd_ref[...])

    @pl.when(f == pl.num_programs(2) - 1)
    def _():
        o_ref[...] = x_ref[...] + gate_ref[...] * _rms(acc_scr[...], gn_ref[...])


def _ffn_call(x, g, sc, sh, wg, wu, wd, gn, gate, tm, tf):
    bm, r, d = x.shape
    ff = wg.shape[1]
    return pl.pallas_call(
        _ffn_kernel,
        grid=(bm, r // tm, ff // tf),
        in_specs=[_rows(tm, d), _const(g), _mod(sc, tm), _mod(sh, tm),
                  pl.BlockSpec((d, tf), lambda b, i, f: (0, f)),
                  pl.BlockSpec((d, tf), lambda b, i, f: (0, f)),
                  pl.BlockSpec((tf, d), lambda b, i, f: (f, 0)),
                  _const(gn), _mod(gate, tm)],
        out_specs=_rows(tm, d),
        out_shape=jax.ShapeDtypeStruct((bm, r, d), F32),
        scratch_shapes=[pltpu.VMEM((tm, d), wg.dtype), pltpu.VMEM((tm, d), F32)],
        compiler_params=_params("arbitrary", "arbitrary", "arbitrary"),
        name="dense_swiglu",
    )(x, g, sc, sh, wg, wu, wd, gn, gate)


def _rope(t, cos, sin_lo, sin_hi):
    half = ROT_DIM // 2
    outs = []
    for s in range(t.shape[1] // LANES):
        ts = t[:, s * LANES:(s + 1) * LANES]
        outs.append(ts * cos + pltpu.roll(ts, LANES - half, 1) * sin_lo + pltpu.roll(ts, half, 1) * sin_hi)
    return jnp.concatenate(outs, axis=1)


def _qkv_values(x_ref, gkv_ref, g_ref, sc_ref, sh_ref, wkv_ref, wq_ref, cos_ref, slo_ref, shi_ref):
    x = x_ref[...]
    xn = x * lax.rsqrt(jnp.mean(x * x, axis=-1, keepdims=True) + RMS_EPS)
    kv = _mm(xn * gkv_ref[...], wkv_ref[...])
    q = _mm((xn * g_ref[...]) * (1.0 + sc_ref[...]) + sh_ref[...], wq_ref[...])
    cos, slo, shi = cos_ref[...], slo_ref[...], shi_ref[...]
    k = _rope(kv[:, :KV_DIM], cos, slo, shi)
    v = kv[:, KV_DIM:]
    q = _rope(q, cos, slo, shi) * (HEAD_DIM ** -0.5)
    return k, v, q


def _qkv_sample_kernel(*refs):
    k_ref, v_ref, q_ref = refs[10:]
    k, v, q = _qkv_values(*refs[:10])
    k_ref[...] = k
    v_ref[...] = v
    q_ref[...] = q


def _store_residues(dst_ref, val, scr, dil):
    if dil == 1:
        dst_ref[0] = val.astype(dst_ref.dtype)
        return
    slabs = val.shape[1] // LANES
    for s in range(slabs):
        scr[s] = val[:, s * LANES:(s + 1) * LANES]
    n = val.shape[0] // dil
    for r in range(dil):
        rows = [scr[s, pl.ds(r, n, stride=dil), :] for s in range(slabs)]
        dst_ref[r] = jnp.concatenate(rows, axis=1).astype(dst_ref.dtype)


def _qkv_prompt_kernel(*refs, first_kept):
    k_ref, v_ref = refs[10:12]
    kd_refs = refs[12:12 + N_DIL_GROUPS]
    vd_refs = refs[12 + N_DIL_GROUPS:12 + 2 * N_DIL_GROUPS]
    qd_refs = refs[12 + 2 * N_DIL_GROUPS:12 + 3 * N_DIL_GROUPS]
    scr = refs[12 + 3 * N_DIL_GROUPS]
    k, v, q = _qkv_values(*refs[:10])
    @pl.when(pl.program_id(1) >= first_kept)
    def _():
        k_ref[...] = k
        v_ref[...] = v

    for g, (_, dil) in enumerate(DILATION_GROUPS):
        _store_residues(kd_refs[g], k, scr, dil)
        _store_residues(vd_refs[g], v, scr, dil)
        _store_residues(qd_refs[g], q[:, g * KV_DIM:(g + 1) * KV_DIM], scr, dil)


def _qkv_call(x, gkv, g, sc, sh, wkv, wq, cos, slo, shi, tm, prompt):
    bm, r, d = x.shape
    nq = wq.shape[1]
    tab = pl.BlockSpec((tm, LANES), lambda b, i: (i, 0))
    in_specs = [_rows(tm, d), _const(gkv), _const(g), _mod(sc, tm), _mod(sh, tm),
                _const(wkv), _const(wq), tab, tab, tab]
    keep = min(MAX_WINDOW, r) if prompt else r
    first_kept = (r - keep) // tm
    kept = pl.BlockSpec((None, tm, KV_DIM), lambda b, i: (b, jnp.maximum(i - first_kept, 0), 0))
    out_specs = [kept, kept]
    out_shape = [jax.ShapeDtypeStruct((bm, keep, KV_DIM), F32), jax.ShapeDtypeStruct((bm, keep, KV_DIM), F32)]
    if not prompt:
        return pl.pallas_call(
            _qkv_sample_kernel,
            grid=(bm, r // tm),
            in_specs=in_specs,
            out_specs=out_specs + [_rows(tm, nq)],
            out_shape=out_shape + [jax.ShapeDtypeStruct((bm, r, nq), F32)],
            compiler_params=_params("arbitrary", "arbitrary"),
            name="qkv_proj_rope_sample",
        )(x, gkv, g, sc, sh, wkv, wq, cos, slo, shi)
    for _ in range(3):
        for _, dil in DILATION_GROUPS:
            out_specs.append(pl.BlockSpec((None, dil, tm // dil, KV_DIM), lambda b, i: (b, 0, i, 0)))
            out_shape.append(jax.ShapeDtypeStruct((bm, dil, r // dil, KV_DIM), BF16))
    outs = pl.pallas_call(
        functools.partial(_qkv_prompt_kernel, first_kept=first_kept),
        grid=(bm, r // tm),
        in_specs=in_specs,
        out_specs=out_specs,
        out_shape=out_shape,
        scratch_shapes=[pltpu.VMEM((KV_DIM // LANES, tm, LANES), F32)],
        compiler_params=_params("arbitrary", "arbitrary"),
        name="qkv_proj_rope_prompt",
    )(x, gkv, g, sc, sh, wkv, wq, cos, slo, shi)
    n = N_DIL_GROUPS
    return outs[0], outs[1], outs[2:2 + n], outs[2 + n:2 + 2 * n], outs[2 + 2 * n:2 + 3 * n]


def _band_kernel(q_ref, kp_ref, kc_ref, vp_ref, vc_ref, o_ref, l_ref, *, nb):
    qi = lax.broadcasted_iota(jnp.int32, (SPAN, 2 * SPAN), 0)
    kj = lax.broadcasted_iota(jnp.int32, (SPAN, 2 * SPAN), 1)
    band = (kj >= qi) & (kj <= qi + SPAN)
    band_first = band & (kj >= jnp.where(pl.program_id(2) == 0, SPAN, 0))
    lane = lax.broadcasted_iota(jnp.int32, (SPAN, LANES), 1)
    head0 = lane < HEAD_DIM
    for i in range(nb):
        cur = slice(i * SPAN, (i + 1) * SPAN)
        prev = slice((i - 1) * SPAN, i * SPAN)
        for hp in range(KV_DIM // LANES):
            cols = slice(hp * LANES, (hp + 1) * LANES)
            q2 = q_ref[cur, cols]
            if i == 0:
                k2 = jnp.concatenate([kp_ref[:, cols], kc_ref[cur, cols]], axis=0)
                v2 = jnp.concatenate([vp_ref[:, cols], vc_ref[cur, cols]], axis=0)
                mask = band_first
            else:
                k2 = jnp.concatenate([kc_ref[prev, cols], kc_ref[cur, cols]], axis=0)
                v2 = jnp.concatenate([vc_ref[prev, cols], vc_ref[cur, cols]], axis=0)
                mask = band
            o_pair = None
            l_pair = None
            for hh in range(2):
                sel = head0 if hh == 0 else jnp.logical_not(head0)
                qm = jnp.where(sel, q2, jnp.zeros_like(q2))
                s = lax.dot_general(qm, k2, (((1,), (1,)), ((), ())), preferred_element_type=F32)
                s = jnp.where(mask, s, NEG_INF)
                m = jnp.max(s, axis=-1, keepdims=True)
                p = jnp.exp(s - m)
                den = jnp.sum(p, axis=-1, keepdims=True)
                o = jnp.dot(p.astype(BF16), v2, preferred_element_type=F32) / den
                lse = jnp.broadcast_to(m + jnp.log(den), (SPAN, LANES))
                o_pair = o if o_pair is None else jnp.where(head0, o_pair, o)
                l_pair = lse if l_pair is None else jnp.where(head0, l_pair, lse)
            o_ref[cur, cols] = o_pair.astype(o_ref.dtype)
            l_ref[cur, cols] = l_pair


def _band_call(q, k, v, nb):
    b, dil, m, _ = k.shape
    tq = nb * SPAN
    cur = pl.BlockSpec((None, None, tq, KV_DIM), lambda bb, r, n: (bb, r, n, 0))
    prev = pl.BlockSpec((None, None, SPAN, KV_DIM), lambda bb, r, n: (bb, r, jnp.maximum(n * nb - 1, 0), 0))
    return pl.pallas_call(
        functools.partial(_band_kernel, nb=nb),
        grid=(b, dil, m // tq),
        in_specs=[cur, prev, cur, prev, cur],
        out_specs=[cur, cur],
        out_shape=[jax.ShapeDtypeStruct((b, dil, m, KV_DIM), BF16),
                   jax.ShapeDtypeStruct((b, dil, m, KV_DIM), F32)],
        compiler_params=_params("arbitrary", "arbitrary", "arbitrary"),
        name=f"band_attention_dil{dil}",
    )(q, k, k, v, v)


def _transpose_small(x):
    c = x.shape[1]
    eye = jnp.where(lax.broadcasted_iota(jnp.int32, (c, c), 0) == lax.broadcasted_iota(jnp.int32, (c, c), 1),
                    1.0, 0.0).astype(F32)
    return lax.dot_general(eye, x, (((1,), (1,)), ((), ())), preferred_element_type=F32, precision=HIGHEST)


def _sample_attn_kernel(q_ref, kn_ref, vn_ref, kc_ref, vc_ref, o_ref, ko_ref, vo_ref):
    cl = kc_ref.shape[-1]
    is_last = lax.broadcasted_iota(jnp.int32, (HEAD_DIM, cl), 1) == cl - 1
    kn_rows, vn_rows = kn_ref[0], vn_ref[0]
    kn_cols, vn_cols = _transpose_small(kn_rows), _transpose_small(vn_rows)
    for h in range(N_KV_HEADS):
        ko_ref[0, h] = jnp.where(is_last, kn_cols[:, h:h + 1], pltpu.roll(kc_ref[0, h], cl - 1, 1))
        vo_ref[0, h] = jnp.where(is_last, vn_cols[:, h:h + 1], pltpu.roll(vc_ref[0, h], cl - 1, 1))
    outs, lses = [], []
    for g, (win, dil) in enumerate(DILATION_GROUPS):
        q_rows = q_ref[0, g]
        q_cols = _transpose_small(q_rows)
        s = jnp.concatenate(
            [jnp.sum(kc_ref[0, h, :, cl - win:] * q_cols[:, h:h + 1], axis=0, keepdims=True)
             for h in range(N_KV_HEADS)], axis=0)
        if dil > 1:
            lane = lax.broadcasted_iota(jnp.int32, s.shape, 1)
            s = jnp.where((lane & (dil - 1)) == 0, s, NEG_INF)
        s_new = jnp.sum(kn_rows * q_rows, axis=-1, keepdims=True)
        m = jnp.maximum(jnp.max(s, axis=-1, keepdims=True), s_new)
        p = jnp.exp(s - m)
        p_new = jnp.exp(s_new - m)
        den = jnp.sum(p, axis=-1, keepdims=True) + p_new
        pn = p / den
        o_cols = jnp.concatenate(
            [jnp.sum(vc_ref[0, h, :, cl - win:] * pn[h:h + 1, :], axis=-1, keepdims=True)
             for h in range(N_KV_HEADS)], axis=1)
        outs.append(_transpose_small(o_cols) + (p_new / den) * vn_rows)
        lses.append(m + jnp.log(den))
    mx = jnp.maximum(jnp.maximum(lses[0], lses[1]), lses[2])
    es = [jnp.exp(l - mx) for l in lses]
    o_ref[0] = (es[0] * outs[0] + es[1] * outs[1] + es[2] * outs[2]) / (es[0] + es[1] + es[2])


def _sample_attn_call(q, k_new, v_new, cache_k, cache_v):
    n, nh, hd, cl = cache_k.shape
    col = pl.BlockSpec((1, nh, hd), lambda i: (i, 0, 0))
    cache = pl.BlockSpec((1, nh, hd, cl), lambda i: (i, 0, 0, 0))
    return pl.pallas_call(
        _sample_attn_kernel,
        grid=(n,),
        in_specs=[pl.BlockSpec((1, N_DIL_GROUPS, nh, hd), lambda i: (i, 0, 0, 0)), col, col,
                  cache, cache],
        out_specs=[col, cache, cache],
        out_shape=[jax.ShapeDtypeStruct((n, nh, hd), F32),
                   jax.ShapeDtypeStruct(cache_k.shape, cache_k.dtype),
                   jax.ShapeDtypeStruct(cache_v.shape, cache_v.dtype)],
        compiler_params=_params("arbitrary"),
        name="window_attention_cache_append",
    )(q, k_new, v_new, cache_k, cache_v)


def _attn_out_kernel(o1_ref, o2_ref, o3_ref, l1_ref, l2_ref, l3_ref, x_ref, wo_ref, gn_ref, gate_ref,
                     y_ref, *scrs):
    vals = []
    for ref, scr, (_, dil) in zip((o1_ref, o2_ref, o3_ref, l1_ref, l2_ref, l3_ref), scrs,
                                  DILATION_GROUPS + DILATION_GROUPS):
        if dil == 1:
            vals.append(ref[0].astype(F32))
            continue
        n = ref.shape[1]
        slabs = ref.shape[2] // LANES
        for r in range(dil):
            for s in range(slabs):
                scr[s, pl.ds(r, n, stride=dil), :] = ref[r, :, s * LANES:(s + 1) * LANES].astype(F32)
        vals.append(jnp.concatenate([scr[s] for s in range(slabs)], axis=1))
    o1, o2, o3, l1, l2, l3 = vals
    mx = jnp.maximum(jnp.maximum(l1, l2), l3)
    e1, e2, e3 = jnp.exp(l1 - mx), jnp.exp(l2 - mx), jnp.exp(l3 - mx)
    o = (e1 * o1 + e2 * o2 + e3 * o3) / (e1 + e2 + e3)
    out = _mm(o, wo_ref[...])
    y_ref[...] = x_ref[...] + gate_ref[...] * _rms(out, gn_ref[...])


def _attn_out_call(os_, ls_, x, wo, gn, gate, tm):
    bm, r, d = x.shape
    res = [pl.BlockSpec((None, dil, tm // dil, KV_DIM), lambda b, i: (b, 0, i, 0))
           for _, dil in DILATION_GROUPS]
    return pl.pallas_call(
        _attn_out_kernel,
        grid=(bm, r // tm),
        in_specs=res + res + [_rows(tm, d), _const(wo), _const(gn), _mod(gate, tm)],
        out_specs=_rows(tm, d),
        out_shape=jax.ShapeDtypeStruct((bm, r, d), F32),
        scratch_shapes=[pltpu.VMEM((KV_DIM // LANES, tm, LANES), F32)] * (2 * N_DIL_GROUPS),
        compiler_params=_params("arbitrary", "arbitrary"),
        name="attn_merge_out_proj",
    )(*os_, *ls_, x, wo, gn, gate)


def _proj_res_kernel(o_ref, x_ref, wo_ref, gn_ref, gate_ref, y_ref):
    out = _mm(o_ref[...], wo_ref[...])
    y_ref[...] = x_ref[...] + gate_ref[...] * _rms(out, gn_ref[...])


def _proj_res_call(o, x, wo, gn, gate, tm):
    bm, r, d = x.shape
    return pl.pallas_call(
        _proj_res_kernel,
        grid=(bm, r // tm),
        in_specs=[_rows(tm, o.shape[2]), _rows(tm, d), _const(wo), _const(gn), _mod(gate, tm)],
        out_specs=_rows(tm, d),
        out_shape=jax.ShapeDtypeStruct((bm, r, d), F32),
        compiler_params=_params("arbitrary", "arbitrary"),
        name="attn_out_proj",
    )(o, x, wo, gn, gate)


def _top2(logits):
    lane = lax.broadcasted_iota(jnp.int32, logits.shape, 1).astype(F32)
    m1 = jnp.max(logits, axis=-1, keepdims=True)
    i1 = jnp.min(jnp.where(logits == m1, lane, float(LANES)), axis=-1, keepdims=True)
    rest = jnp.where(lane == i1, -jnp.inf, logits)
    m2 = jnp.max(rest, axis=-1, keepdims=True)
    i2 = jnp.min(jnp.where(rest == m2, lane, float(LANES)), axis=-1, keepdims=True)
    e = jnp.exp(m2 - m1)
    return lane, i1, i2, 1.0 / (1.0 + e), e / (1.0 + e)


def _route_kernel(x_ref, g_ref, sc_ref, sh_ref, wr_ref, br_ref, h_ref, route_ref, cnt_ref, carry_scr,
                  *, stride):
    @pl.when((pl.program_id(0) == 0) & (pl.program_id(1) == 0))
    def _():
        carry_scr[...] = jnp.zeros_like(carry_scr)

    h = _rms(x_ref[...], g_ref[...]) * (1.0 + sc_ref[...]) + sh_ref[...]
    tm = h.shape[0]
    for c in range(SLABS):
        h_ref[pl.ds(c, tm, stride=SUBLANES), :] = h[:, c * LANES:(c + 1) * LANES]
    logits = _mm(h, wr_ref[...]) + br_ref[...]
    lane, i1, i2, g1, g2 = _top2(logits)
    onehot = jnp.where(lane == i1, 1.0, 0.0) + jnp.where(lane == i2, 1.0, 0.0)
    row = lax.broadcasted_iota(jnp.int32, (tm, tm), 0)
    col = lax.broadcasted_iota(jnp.int32, (tm, tm), 1)
    earlier = jnp.where(col < row, 1.0, 0.0).astype(BF16)
    before = jnp.dot(earlier, onehot.astype(BF16), preferred_element_type=F32) + carry_scr[0:1, :]
    r1 = jnp.sum(jnp.where(lane == i1, before, 0.0), axis=-1, keepdims=True)
    r2 = jnp.sum(jnp.where(lane == i2, before, 0.0), axis=-1, keepdims=True)
    p1 = i1 * float(stride) + r1
    p2 = i2 * float(stride) + r2
    route_ref[...] = (jnp.where(lane == 0.0, p1, 0.0) + jnp.where(lane == 1.0, p2, 0.0)
                      + jnp.where(lane == 2.0, g1, 0.0) + jnp.where(lane == 3.0, g2, 0.0))
    carry = carry_scr[...] + jnp.sum(onehot, axis=0, keepdims=True)
    carry_scr[...] = carry
    cnt_ref[...] = carry


def _route_call(x, g, sc, sh, wr, br, tm, stride):
    bm, r, d = x.shape
    nt = r // tm
    return pl.pallas_call(
        functools.partial(_route_kernel, stride=stride),
        grid=(bm, nt),
        in_specs=[_rows(tm, d), _const(g), _mod(sc, tm), _mod(sh, tm), _const(wr), _const(br)],
        out_specs=[pl.BlockSpec((tm * SUBLANES, LANES), lambda b, i: (b * nt + i, 0)),
                   _rows(tm, LANES),
                   pl.BlockSpec((SUBLANES, LANES), lambda b, i: (0, 0))],
        out_shape=[jax.ShapeDtypeStruct((bm * r * SUBLANES, LANES), F32),
                   jax.ShapeDtypeStruct((bm, r, LANES), F32),
                   jax.ShapeDtypeStruct((SUBLANES, LANES), F32)],
        scratch_shapes=[pltpu.VMEM((SUBLANES, LANES), F32)],
        compiler_params=_params("arbitrary", "arbitrary"),
        name="moe_route",
    )(x, g, sc, sh, wr, br)


def _token_tile(ref, idx):
    return ref.at[pl.ds(pl.multiple_of(idx * SUBLANES, SUBLANES), SUBLANES), :]


def _dispatch_kernel(cnt_ref, pos_ref, h_ref, xs_hbm, zero_scr, sem, zsem, *, tm, stride, tmx):
    i = pl.program_id(0)

    def issue(t, carry):
        src = _token_tile(h_ref, t)
        for k in range(2):
            pltpu.make_async_copy(src, _token_tile(xs_hbm, pos_ref[0, k * tm + t]), sem).start(priority=k)
        return carry

    lax.fori_loop(0, tm, issue, 0)
    for _ in range(2):
        pltpu.make_async_copy(h_ref, xs_hbm.at[pl.ds(0, tm * SUBLANES), :], sem).wait()

    @pl.when(i == pl.num_programs(0) - 1)
    def _():
        zero_scr[...] = jnp.zeros_like(zero_scr)
        copies = []
        for e in range(N_EXPERTS):
            start = pl.multiple_of((e * stride + cnt_ref[e]) * SUBLANES, SUBLANES)
            copies.append(pltpu.make_async_copy(zero_scr, xs_hbm.at[pl.ds(start, tmx * SUBLANES), :], zsem))
            copies[-1].start()
        for cp in copies:
            cp.wait()


def _dispatch_call(cnt, pos, h2d, tm, stride, tmx):
    n_tiles = pos.shape[0]
    rows = N_EXPERTS * stride * SUBLANES
    return pl.pallas_call(
        functools.partial(_dispatch_kernel, tm=tm, stride=stride, tmx=tmx),
        grid_spec=pltpu.PrefetchScalarGridSpec(
            num_scalar_prefetch=1,
            grid=(n_tiles,),
            in_specs=[pl.BlockSpec((None, 1, 2 * tm), lambda i, cnt: (i, 0, 0), memory_space=pltpu.SMEM),
                      pl.BlockSpec((tm * SUBLANES, LANES), lambda i, cnt: (i, 0))],
            out_specs=pl.BlockSpec(memory_space=pl.ANY),
            scratch_shapes=[pltpu.VMEM((tmx * SUBLANES, LANES), F32),
                            pltpu.SemaphoreType.DMA(()), pltpu.SemaphoreType.DMA(())],
        ),
        out_shape=jax.ShapeDtypeStruct((rows, LANES), F32),
        compiler_params=_params("arbitrary"),
        name="moe_dispatch",
    )(cnt, pos, h2d)


def _expert_kernel(blk_ref, exp_ref, nused_ref, xs_ref, wg_ref, wu_ref, wd_ref, o_ref, x_scr, acc_scr,
                   *, tmx):
    j = pl.program_id(0)
    f = pl.program_id(1)

    @pl.when(j < nused_ref[0])
    def _():
        @pl.when(f == 0)
        def _():
            for c in range(SLABS):
                x_scr[:, c * LANES:(c + 1) * LANES] = (
                    xs_ref[pl.ds(c, tmx, stride=SUBLANES), :].astype(x_scr.dtype))
            acc_scr[...] = jnp.zeros_like(acc_scr)

        x = x_scr[...]
        he = _silu(_mm(x, wg_ref[...])) * _mm(x, wu_ref[...])
        acc_scr[...] += _mm(he, wd_ref[...])

        @pl.when(f == pl.num_programs(1) - 1)
        def _():
            for c in range(SLABS):
                o_ref[pl.ds(c, tmx, stride=SUBLANES), :] = acc_scr[:, c * LANES:(c + 1) * LANES]


def _expert_call(blk, exp, nused, xs, wg, wu, wd, tmx, tf):
    ne, d, ff = wg.shape
    nf = ff // tf

    def fidx(j, f, nused):
        return jnp.where(j < nused[0], f, nf - 1)

    row_spec = pl.BlockSpec((tmx * SUBLANES, LANES), lambda j, f, blk, exp, nu: (blk[j], 0))
    return pl.pallas_call(
        functools.partial(_expert_kernel, tmx=tmx),
        grid_spec=pltpu.PrefetchScalarGridSpec(
            num_scalar_prefetch=3,
            grid=(blk.shape[0], nf),
            in_specs=[row_spec,
                      pl.BlockSpec((None, d, tf), lambda j, f, blk, exp, nu: (exp[j], 0, fidx(j, f, nu))),
                      pl.BlockSpec((None, d, tf), lambda j, f, blk, exp, nu: (exp[j], 0, fidx(j, f, nu))),
                      pl.BlockSpec((None, tf, d), lambda j, f, blk, exp, nu: (exp[j], fidx(j, f, nu), 0))],
            out_specs=row_spec,
            scratch_shapes=[pltpu.VMEM((tmx, d), wg.dtype), pltpu.VMEM((tmx, d), F32)],
        ),
        out_shape=jax.ShapeDtypeStruct(xs.shape, F32),
        compiler_params=_params("arbitrary", "arbitrary"),
        name="moe_grouped_experts",
    )(blk, exp, nused, xs, wg, wu, wd)


def _combine_kernel(pos_ref, route_ref, x_ref, gn_ref, gate_ref, out_hbm, y_ref, buf, sem, *, tm):
    def issue(t, carry):
        for k in range(2):
            pltpu.make_async_copy(_token_tile(out_hbm, pos_ref[0, k * tm + t]),
                                  _token_tile(buf, k * tm + t), sem).start(priority=k)
        return carry

    lax.fori_loop(0, tm, issue, 0)
    pltpu.make_async_copy(out_hbm.at[pl.ds(0, 2 * tm * SUBLANES), :], buf, sem).wait()
    route = route_ref[...]
    g1, g2 = route[:, 2:3], route[:, 3:4]
    parts = []
    for c in range(SLABS):
        parts.append(g1 * buf[pl.ds(c, tm, stride=SUBLANES), :]
                     + g2 * buf[pl.ds(tm * SUBLANES + c, tm, stride=SUBLANES), :])
    y = jnp.concatenate(parts, axis=1)
    y_ref[...] = x_ref[...] + gate_ref[...] * _rms(y, gn_ref[...])


def _combine_call(pos, route, x, gn, gate, out2d, tm):
    bm, r, d = x.shape
    nt = r // tm
    return pl.pallas_call(
        functools.partial(_combine_kernel, tm=tm),
        grid=(bm, nt),
        in_specs=[pl.BlockSpec((None, 1, 2 * tm), lambda b, i: (b * nt + i, 0, 0), memory_space=pltpu.SMEM),
                  _rows(tm, LANES), _rows(tm, d), _const(gn), _mod(gate, tm),
                  pl.BlockSpec(memory_space=pl.ANY)],
        out_specs=_rows(tm, d),
        out_shape=jax.ShapeDtypeStruct((bm, r, d), F32),
        scratch_shapes=[pltpu.VMEM((2 * tm * SUBLANES, LANES), F32), pltpu.SemaphoreType.DMA(())],
        compiler_params=_params("arbitrary", "arbitrary"),
        name="moe_combine",
    )(pos, route, x, gn, gate, out2d)


def _moe_sparse(x, g, sc, sh, wr, br, wg, wu, wd, gn, gate, tm):
    bm, r, d = x.shape
    m = bm * r
    tmx = MOE_TILE_ROWS
    stride = m + tmx
    n_tiles = 2 * m // tmx + N_EXPERTS
    h2d, route, cnt = _route_call(x, g, sc, sh, wr, br, tm, stride)
    cnt = cnt[0, :N_EXPERTS].astype(jnp.int32)
    pos = route[..., :2].astype(jnp.int32).reshape(m // tm, tm, 2).transpose(0, 2, 1).reshape(m // tm, 1, 2 * tm)
    xs = _dispatch_call(cnt, pos, h2d, tm, stride, tmx)
    per = (cnt + tmx - 1) // tmx
    ends = jnp.cumsum(per)
    j = jnp.minimum(jnp.arange(n_tiles, dtype=jnp.int32), ends[-1] - 1)
    exp = jnp.sum((j[:, None] >= ends[None, :]).astype(jnp.int32), axis=1)
    blk = exp * (stride // tmx) + j - (ends - per)[exp]
    out2d = _expert_call(blk, exp, ends[-1:], xs, wg, wu, wd, tmx, wg.shape[2] // 2)
    return _combine_call(pos, route, x, gn, gate, out2d, tm)


def _router_kernel(x_ref, g_ref, sc_ref, sh_ref, wr_ref, br_ref, h_ref, comb_ref):
    h = _rms(x_ref[...], g_ref[...]) * (1.0 + sc_ref[...]) + sh_ref[...]
    h_ref[...] = h.astype(h_ref.dtype)
    logits = _mm(h, wr_ref[...]) + br_ref[...]
    lane, i1, i2, g1, g2 = _top2(logits)
    comb_ref[...] = jnp.where(lane == i1, g1, 0.0) + jnp.where(lane == i2, g2, 0.0)


def _router_call(x, g, sc, sh, wr, br, h_dtype, tm):
    bm, r, d = x.shape
    return pl.pallas_call(
        _router_kernel,
        grid=(bm, r // tm),
        in_specs=[_rows(tm, d), _const(g), _mod(sc, tm), _mod(sh, tm), _const(wr), _const(br)],
        out_specs=[_rows(tm, d), _rows(tm, LANES)],
        out_shape=[jax.ShapeDtypeStruct((bm, r, d), h_dtype),
                   jax.ShapeDtypeStruct((bm, r, LANES), F32)],
        compiler_params=_params("arbitrary", "arbitrary"),
        name="moe_router",
    )(x, g, sc, sh, wr, br)


def _moe_kernel(h_ref, comb_ref, x_ref, wg_ref, wu_ref, wd_ref, gn_ref, gate_ref,
                o_ref, wgb_ref, wub_ref, wdb_ref, acc_scr):
    e = pl.program_id(2)
    f = pl.program_id(3)

    @pl.when((e == 0) & (f == 0))
    def _():
        acc_scr[...] = jnp.zeros_like(acc_scr)

    comb = comb_ref[...]
    lane = lax.broadcasted_iota(jnp.int32, comb.shape, 1)
    cw = jnp.sum(jnp.where(lane == e, comb, 0.0), axis=-1, keepdims=True)
    wg, wu, wd = wg_ref[...].astype(BF16), wu_ref[...].astype(BF16), wd_ref[...].astype(BF16)
    wgb_ref[...] = wg
    wub_ref[...] = wu
    wdb_ref[...] = wd
    h = h_ref[...]
    he = _silu(_mm(h, wg)) * _mm(h, wu)
    acc_scr[...] += cw * _mm(he, wd)

    @pl.when((e == pl.num_programs(2) - 1) & (f == pl.num_programs(3) - 1))
    def _():
        o_ref[...] = x_ref[...] + gate_ref[...] * _rms(acc_scr[...], gn_ref[...])


def _moe_call(h, comb, x, wg, wu, wd, gn, gate, tm, tf):
    bm, r, d = x.shape
    ne, _, ff = wg.shape
    assert bm == 1 and r == tm
    up = pl.BlockSpec((None, d, tf), lambda b, i, e, f: (e, 0, f))
    down = pl.BlockSpec((None, tf, d), lambda b, i, e, f: (e, f, 0))
    return pl.pallas_call(
        _moe_kernel,
        grid=(bm, r // tm, ne, ff // tf),
        in_specs=[_rows(tm, d), _rows(tm, LANES), _rows(tm, d), up, up, down, _const(gn), _mod(gate, tm)],
        out_specs=[_rows(tm, d), up, up, down],
        out_shape=[jax.ShapeDtypeStruct((bm, r, d), F32), jax.ShapeDtypeStruct(wg.shape, BF16),
                   jax.ShapeDtypeStruct(wu.shape, BF16), jax.ShapeDtypeStruct(wd.shape, BF16)],
        scratch_shapes=[pltpu.VMEM((tm, d), F32)],
        compiler_params=_params("arbitrary", "arbitrary", "arbitrary", "arbitrary"),
        name="moe_experts",
    )(h, comb, x, wg, wu, wd, gn, gate)


def _ssm_params(lam_re, lam_im, log_dt, b_re, b_im, c_re, c_im, wdtype):
    dt = jnp.exp(log_dt)[:, None]
    mag = jnp.exp(lam_re * dt)
    a_re, a_im = mag * jnp.cos(lam_im * dt), mag * jnp.sin(lam_im * dt)
    den = lam_re * lam_re + lam_im * lam_im
    f_re = ((a_re - 1.0) * lam_re + a_im * lam_im) / den
    f_im = (a_im * lam_re - (a_re - 1.0) * lam_im) / den
    bb_re = f_re[..., None] * b_re - f_im[..., None] * b_im
    bb_im = f_re[..., None] * b_im + f_im[..., None] * b_re
    eye = jnp.eye(GROUPS_PER_SLAB, dtype=F32)

    def in_map(bb):
        t = bb.reshape(SLABS, GROUPS_PER_SLAB, SSM_STATE, SSM_CH)
        w = jnp.einsum('sgpc,gh->sgchp', t, eye)
        return w.reshape(SLABS, LANES, SLAB_STATES).astype(wdtype)

    def out_map(cc):
        t = cc.reshape(SLABS, GROUPS_PER_SLAB, SSM_CH, SSM_STATE)
        w = jnp.einsum('sgcp,gh->sgphc', t, eye)
        return w.reshape(SLABS, SLAB_STATES, LANES).astype(wdtype)

    return a_re, a_im, in_map(bb_re), in_map(bb_im), out_map(c_re), out_map(c_im)


def _rope_tables(pos):
    half = ROT_DIM // 2
    inv = ROPE_THETA ** (-jnp.arange(half, dtype=F32) / half)
    ang = pos.astype(F32)[:, None] * inv[None, :]
    cos, sin = jnp.cos(ang), jnp.sin(ang)
    n = pos.shape[0]
    pad = jnp.zeros((n, HEAD_DIM - ROT_DIM), F32)
    zero = jnp.zeros((n, half), F32)
    cos_h = jnp.concatenate([cos, cos, pad + 1.0], axis=1)
    lo_h = jnp.concatenate([-sin, zero, pad], axis=1)
    hi_h = jnp.concatenate([zero, sin, pad], axis=1)
    rep = LANES // HEAD_DIM
    return jnp.tile(cos_h, (1, rep)), jnp.tile(lo_h, (1, rep)), jnp.tile(hi_h, (1, rep))


def _trunk(x, mods, pos, wts, prompt, ssm0=None, cache=None):
    bm, r, d = x.shape
    tm = 512 if prompt else r
    wd = BF16 if prompt else F32
    cast = lambda w: w.astype(wd)
    norm_g = wts['norm_g'].reshape(DEPTH, 4, 1, d)

    sh1, sc1, g1, sh2, sc2, g2 = mods[0]
    u = _inproj_call(x, norm_g[0, 0], sc1, sh1, cast(wts['ssm_w_in'][0]), tm)
    a_re, a_im = wts['ssm_prepared'][:2]
    wbr, wbi, cr, ci = (cast(w) for w in wts['ssm_prepared'][2:])
    dvec = wts['ssm_d'][0].reshape(1, d)
    if prompt:
        z, fr, fi = _ssm_prompt_call(u, wbr, wbi, a_re.reshape(STATE_TILES, LANES),
                                     a_im.reshape(STATE_TILES, LANES), cr, ci, dvec, SCAN_CHUNK)
    else:
        z, fr, fi = _ssm_sample_call(u[0], ssm0[0].reshape(r, SSM_FLAT), ssm0[1].reshape(r, SSM_FLAT),
                                     wbr, wbi, a_re.reshape(1, SSM_FLAT), a_im.reshape(1, SSM_FLAT),
                                     cr, ci, dvec)
        z = z[None]
    fr = fr.reshape(-1, SSM_GROUPS, SSM_STATE)[None]
    fi = fi.reshape(-1, SSM_GROUPS, SSM_STATE)[None]
    x = _s5out_call(z, x, cast(wts['ssm_w_glu'][0]), cast(wts['ssm_w_out'][0]), norm_g[0, 1], g1, tm)
    tf = wts['ffn_w_gate'].shape[2] // 2
    x = _ffn_call(x, norm_g[0, 2], sc2, sh2, cast(wts['ffn_w_gate'][0]), cast(wts['ffn_w_up'][0]),
                  cast(wts['ffn_w_down'][0]), norm_g[0, 3], g2, tm, tf)

    sh1, sc1, g1, sh2, sc2, g2 = mods[1]
    cos, slo, shi = _rope_tables(pos)
    qkv = _qkv_call(x, wts['kv_norm_g'].reshape(1, d), norm_g[1, 0], sc1, sh1, cast(wts['w_kv']),
                    cast(wts['attn_w_q'][0]), cos, slo, shi, tm, prompt)
    wo = cast(wts['attn_w_o'][0])
    wr = cast(jnp.zeros((d, LANES), F32).at[:, :N_EXPERTS].set(wts['moe_w_router'][0]))
    br = jnp.full((1, LANES), NEG_INF, F32).at[0, :N_EXPERTS].set(wts['moe_b_router'][0])
    experts = None
    if prompt:
        wg, wu, wdn = wts['experts_bf16']
        k, v, kd, vd, qd = qkv
        os_, ls_ = [], []
        for g, (_, dil) in enumerate(DILATION_GROUPS):
            o, lse = _band_call(qd[g], kd[g], vd[g], min(4, r // dil // SPAN))
            os_.append(o)
            ls_.append(lse)
        x = _attn_out_call(os_, ls_, x, wo, norm_g[1, 1], g1, tm)
        x = _moe_sparse(x, norm_g[1, 2], sc2, sh2, wr, br, wg, wu, wdn, norm_g[1, 3], g2, tm)
    else:
        k, v, q = qkv
        o, k, v = _sample_attn_call(q[0].reshape(r, N_DIL_GROUPS, N_KV_HEADS, HEAD_DIM),
                                    k[0].reshape(r, N_KV_HEADS, HEAD_DIM),
                                    v[0].reshape(r, N_KV_HEADS, HEAD_DIM),
                                    cache[0].transpose(0, 2, 3, 1), cache[1].transpose(0, 2, 3, 1))
        k, v = k.transpose(0, 3, 1, 2), v.transpose(0, 3, 1, 2)
        x = _proj_res_call(o.reshape(1, r, KV_DIM), x, wo, norm_g[1, 1], g1, tm)
        h, comb = _router_call(x, norm_g[1, 2], sc2, sh2, wr, br, wd, tm)
        wg, wu, wdn = wts['moe_w_gate'][0], wts['moe_w_up'][0], wts['moe_w_down'][0]
        x, *experts = _moe_call(h, comb, x, wg, wu, wdn, norm_g[1, 3], g2, tm, wg.shape[2] // 4)
    return x, fr, fi, k, v, experts


def kernel(x_prompt, x_sample, state_ssm_re, state_ssm_im, cache_k, cache_v, c_prompt, c_sample, ada_w, ada_b, norm_g, ssm_w_in, ssm_lam_re, ssm_lam_im, ssm_log_dt, ssm_b_re, ssm_b_im, ssm_c_re, ssm_c_im, ssm_d, ssm_w_glu, ssm_w_out, kv_norm_g, w_kv, attn_w_q, attn_w_o, ffn_w_gate, ffn_w_up, ffn_w_down, moe_w_router, moe_b_router, moe_w_gate, moe_w_up, moe_w_down):
    wts = dict(norm_g=norm_g, ssm_w_in=ssm_w_in, ssm_lam_re=ssm_lam_re, ssm_lam_im=ssm_lam_im,
               ssm_log_dt=ssm_log_dt, ssm_b_re=ssm_b_re, ssm_b_im=ssm_b_im, ssm_c_re=ssm_c_re,
               ssm_c_im=ssm_c_im, ssm_d=ssm_d, ssm_w_glu=ssm_w_glu, ssm_w_out=ssm_w_out,
               kv_norm_g=kv_norm_g, w_kv=w_kv, attn_w_q=attn_w_q, attn_w_o=attn_w_o,
               ffn_w_gate=ffn_w_gate, ffn_w_up=ffn_w_up, ffn_w_down=ffn_w_down,
               moe_w_router=moe_w_router, moe_b_router=moe_b_router, moe_w_gate=moe_w_gate,
               moe_w_up=moe_w_up, moe_w_down=moe_w_down)
    nb, seq, d = x_prompt.shape
    ns = x_sample.shape[0]
    assert x_sample.shape[1] == 1 and ns % SUBLANES == 0

    pad = (-(ns + nb)) % SUBLANES
    c_all = jnp.concatenate([c_sample, c_prompt, jnp.zeros((pad, d), F32)], axis=0)
    mod_all = _ada_call(c_all, ada_w, ada_b)
    mods_s = [[mod_all[l, k, :ns][None] for k in range(6)] for l in range(DEPTH)]
    mods_p = [[mod_all[l, k, ns:ns + nb][:, None, :] for k in range(6)] for l in range(DEPTH)]

    pos_p = jnp.arange(seq, dtype=jnp.int32)
    pos_s = jnp.full((ns,), PAST_LEN, dtype=jnp.int32)
    wts['ssm_prepared'] = _ssm_params(ssm_lam_re[0], ssm_lam_im[0], ssm_log_dt[0], ssm_b_re[0], ssm_b_im[0],
                                      ssm_c_re[0], ssm_c_im[0], F32)
    xs = x_sample.reshape(1, ns, d)
    y_s, sr_s, si_s, k_s, v_s, wts['experts_bf16'] = _trunk(
        xs, mods_s, pos_s, wts, False, ssm0=(state_ssm_re[0], state_ssm_im[0]), cache=(cache_k, cache_v))
    y_p, sr_p, si_p, k_p, v_p, _ = _trunk(x_prompt, mods_p, pos_p, wts, True)

    keep = k_p.shape[1]
    k_rows_p = k_p.reshape(nb, keep, N_KV_HEADS, HEAD_DIM)
    v_rows_p = v_p.reshape(nb, keep, N_KV_HEADS, HEAD_DIM)
    return (y_p, y_s.reshape(ns, 1, d), sr_p, si_p, sr_s, si_s, k_rows_p, v_rows_p, k_s, v_s)
```

```python
import functools
import math

import jax
import jax.numpy as jnp
from jax import lax
from jax.experimental import pallas as pl
from jax.experimental.pallas import tpu as pltpu

F32 = jnp.float32
BF16 = jnp.bfloat16
HIGHEST = lax.Precision.HIGHEST

D_MODEL = 1024
DEPTH = 2
PAST_LEN = 16384
SSM_CH = 16
SSM_GROUPS = D_MODEL // SSM_CH
SSM_STATE = 64
SSM_FLAT = SSM_GROUPS * SSM_STATE
HEAD_DIM = 64
N_KV_HEADS = D_MODEL // 128
KV_DIM = N_KV_HEADS * HEAD_DIM
DILATION_GROUPS = ((128, 1), (512, 4), (2048, 16))
N_DIL_GROUPS = len(DILATION_GROUPS)
SPAN = 128
MAX_WINDOW = 2048
ROT_DIM = HEAD_DIM // 4
ROPE_THETA = 500000.0
N_EXPERTS = 8
RMS_EPS = 1e-6
NEG_INF = -1e30

LANES = 128
SUBLANES = 8
SLABS = D_MODEL // LANES
GROUPS_PER_SLAB = LANES // SSM_CH
SLAB_STATES = GROUPS_PER_SLAB * SSM_STATE
STATE_TILES = SSM_FLAT // LANES
SCAN_STEP_PITCH = 2
SCAN_TILE_PITCH = 15
SCAN_SLAB_ROWS = 480
SCAN_CHUNK = 128
MOE_TILE_ROWS = 512
VMEM_LIMIT = 56 * 1024 * 1024


def _params(*sem):
    return pltpu.CompilerParams(dimension_semantics=sem, vmem_limit_bytes=VMEM_LIMIT)


def _mm(a, w, precise=True):
    if w.dtype == BF16 or not precise:
        return jnp.dot(a.astype(BF16), w.astype(BF16), preferred_element_type=F32)
    return jnp.dot(a.astype(F32), w, preferred_element_type=F32, precision=HIGHEST)


def _rms(x, g):
    return x * lax.rsqrt(jnp.mean(x * x, axis=-1, keepdims=True) + RMS_EPS) * g


def _sigmoid(x):
    return 1.0 / (1.0 + jnp.exp(-x))


def _silu(x):
    return x * _sigmoid(x)


def _gelu_tanh(x):
    return 0.5 * x * (1.0 + jnp.tanh(math.sqrt(2.0 / math.pi) * (x + 0.044715 * (x * x * x))))


def _rows(tm, n):
    return pl.BlockSpec((None, tm, n), lambda b, i, *_: (b, i, 0))


def _mod(arr, tm):
    if arr.shape[1] == 1:
        return pl.BlockSpec((None, 1, arr.shape[2]), lambda b, i, *_: (b, 0, 0))
    return pl.BlockSpec((None, tm, arr.shape[2]), lambda b, i, *_: (b, i, 0))


def _const(arr):
    nd = arr.ndim
    return pl.BlockSpec(arr.shape, lambda *_: (0,) * nd)


def _ada_kernel(c_ref, w_ref, b_ref, o_ref):
    o_ref[...] = _mm(_silu(c_ref[...]), w_ref[...]) + b_ref[...]


def _ada_call(c_all, ada_w, ada_b):
    rows = c_all.shape[0]
    b4 = ada_b.reshape(DEPTH, 6, 1, D_MODEL)
    return pl.pallas_call(
        _ada_kernel,
        grid=(DEPTH, 6),
        in_specs=[pl.BlockSpec((rows, D_MODEL), lambda l, k: (0, 0)),
                  pl.BlockSpec((None, D_MODEL, D_MODEL), lambda l, k: (l, 0, k)),
                  pl.BlockSpec((None, None, 1, D_MODEL), lambda l, k: (l, k, 0, 0))],
        out_specs=pl.BlockSpec((None, None, rows, D_MODEL), lambda l, k: (l, k, 0, 0)),
        out_shape=jax.ShapeDtypeStruct((DEPTH, 6, rows, D_MODEL), F32),
        compiler_params=_params("arbitrary", "arbitrary"),
        name="ada_modulation",
    )(c_all, ada_w, b4)


def _inproj_kernel(x_ref, g_ref, sc_ref, sh_ref, w_ref, u_ref):
    h = _rms(x_ref[...], g_ref[...]) * (1.0 + sc_ref[...]) + sh_ref[...]
    u_ref[...] = _mm(h, w_ref[...])


def _inproj_call(x, g, sc, sh, w, tm):
    bm, r, d = x.shape
    return pl.pallas_call(
        _inproj_kernel,
        grid=(bm, r // tm),
        in_specs=[_rows(tm, d), _const(g), _mod(sc, tm), _mod(sh, tm), _const(w)],
        out_specs=_rows(tm, w.shape[1]),
        out_shape=jax.ShapeDtypeStruct((bm, r, w.shape[1]), F32),
        compiler_params=_params("arbitrary", "arbitrary"),
        name="s5_in_proj",
    )(x, g, sc, sh, w)


def _ssm_prompt_kernel(u_ref, wbr_ref, wbi_ref, ar_ref, ai_ref, c_ref, d_ref,
                       z_ref, fr_ref, fi_ref, br_scr, bi_scr, sr_scr, si_scr, *, t_chunk, nseq):
    tb_n = t_chunk // SUBLANES
    tiles = SLAB_STATES // LANES

    def tile_rows(j):
        return pl.ds(SCAN_TILE_PITCH * j, SUBLANES, stride=SCAN_STEP_PITCH)

    @pl.when(pl.program_id(0) == 0)
    def _():
        sr_scr[...] = jnp.zeros_like(sr_scr)
        si_scr[...] = jnp.zeros_like(si_scr)

    ub = u_ref[...].reshape(nseq * t_chunk, D_MODEL).astype(BF16)
    for s in range(SLABS):
        us = ub[:, s * LANES:(s + 1) * LANES]
        bre = jnp.dot(us, wbr_ref[s], preferred_element_type=F32)
        bim = jnp.dot(us, wbi_ref[s], preferred_element_type=F32)
        for jj in range(tiles):
            cols = slice(jj * LANES, (jj + 1) * LANES)
            for q in range(nseq):
                for tb in range(tb_n):
                    rows = slice((q * tb_n + tb) * SUBLANES, (q * tb_n + tb + 1) * SUBLANES)
                    br_scr[q, tb, tile_rows(s * tiles + jj), :] = bre[rows, cols]
                    bi_scr[q, tb, tile_rows(s * tiles + jj), :] = bim[rows, cols]

    ar = ar_ref[...]
    ai = ai_ref[...]

    def body(tb, carry):
        carry = list(carry)
        for r in range(SUBLANES):
            step = pl.ds(SCAN_STEP_PITCH * r, STATE_TILES, stride=SCAN_TILE_PITCH)
            for q in range(nseq):
                sr, si = carry[q]
                nsr = ar * sr - ai * si + br_scr[q, tb, step, :]
                nsi = ar * si + ai * sr + bi_scr[q, tb, step, :]
                carry[q] = (nsr, nsi)
                br_scr[q, tb, step, :] = nsr
                bi_scr[q, tb, step, :] = nsi
        return tuple(carry)

    final = lax.fori_loop(0, tb_n, body, tuple((sr_scr[q], si_scr[q]) for q in range(nseq)))
    for q in range(nseq):
        sr_scr[q], si_scr[q] = final[q]
        fr_ref[q], fi_ref[q] = final[q]

    def states(scr, j):
        return jnp.concatenate([scr[q, tb, tile_rows(j), :] for q in range(nseq) for tb in range(tb_n)],
                               axis=0).astype(BF16)

    for s in range(SLABS):
        lhs = jnp.concatenate([states(scr, s * tiles + jj) for scr in (br_scr, bi_scr) for jj in range(tiles)],
                              axis=1)
        y = jnp.dot(lhs, c_ref[s], preferred_element_type=F32)
        sl = slice(s * LANES, (s + 1) * LANES)
        for q in range(nseq):
            yq = y[q * t_chunk:(q + 1) * t_chunk]
            z_ref[q, :, sl] = _gelu_tanh(yq + d_ref[:, sl] * u_ref[q, :, sl])


def _ssm_prompt_call(u, wbr, wbi, ar, ai, c, d, t_chunk):
    b, l, dm = u.shape
    kern = functools.partial(_ssm_prompt_kernel, t_chunk=t_chunk, nseq=b)
    seq_spec = pl.BlockSpec((b, t_chunk, dm), lambda c: (0, c, 0))
    state_spec = pl.BlockSpec((b, STATE_TILES, LANES), lambda c: (0, 0, 0))
    scr = pltpu.VMEM((b, t_chunk // SUBLANES, SCAN_SLAB_ROWS, LANES), F32)
    carry = pltpu.VMEM((b, STATE_TILES, LANES), F32)
    return pl.pallas_call(
        kern,
        grid=(l // t_chunk,),
        in_specs=[seq_spec, _const(wbr), _const(wbi), _const(ar), _const(ai),
                  _const(c), _const(d)],
        out_specs=[seq_spec, state_spec, state_spec],
        out_shape=[jax.ShapeDtypeStruct((b, l, dm), F32),
                   jax.ShapeDtypeStruct((b, STATE_TILES, LANES), F32),
                   jax.ShapeDtypeStruct((b, STATE_TILES, LANES), F32)],
        scratch_shapes=[scr, scr, carry, carry],
        compiler_params=_params("arbitrary"),
        name="s5_scan_prompt",
    )(u, wbr, wbi, ar, ai, c, d)


def _ssm_sample_kernel(u_ref, x0r_ref, x0i_ref, wbr_ref, wbi_ref, ar_ref, ai_ref, cr_ref, ci_ref,
                       d_ref, z_ref, fr_ref, fi_ref):
    for s in range(SLABS):
        sl = slice(s * LANES, (s + 1) * LANES)
        st = slice(s * SLAB_STATES, (s + 1) * SLAB_STATES)
        us = u_ref[:, sl]
        a_r, a_i = ar_ref[:, st], ai_ref[:, st]
        x0r, x0i = x0r_ref[:, st], x0i_ref[:, st]
        sr = _mm(us, wbr_ref[s]) + a_r * x0r - a_i * x0i
        si = _mm(us, wbi_ref[s]) + a_r * x0i + a_i * x0r
        fr_ref[:, st] = sr
        fi_ref[:, st] = si
        y = _mm(sr, cr_ref[s]) - _mm(si, ci_ref[s])
        z_ref[:, sl] = _gelu_tanh(y + d_ref[:, sl] * us)


def _ssm_sample_call(u, x0r, x0i, wbr, wbi, ar, ai, cr, ci, d):
    n = u.shape[0]
    args = (u, x0r, x0i, wbr, wbi, ar, ai, cr, ci, d)
    return pl.pallas_call(
        _ssm_sample_kernel,
        grid=(1,),
        in_specs=[_const(a) for a in args],
        out_specs=[pl.BlockSpec((n, D_MODEL), lambda i: (0, 0)),
                   pl.BlockSpec((n, SSM_FLAT), lambda i: (0, 0)),
                   pl.BlockSpec((n, SSM_FLAT), lambda i: (0, 0))],
        out_shape=[jax.ShapeDtypeStruct((n, D_MODEL), F32),
                   jax.ShapeDtypeStruct((n, SSM_FLAT), F32),
                   jax.ShapeDtypeStruct((n, SSM_FLAT), F32)],
        compiler_params=_params("arbitrary"),
        name="s5_step_sample",
    )(*args)


def _s5out_kernel(z_ref, x_ref, wg_ref, wo_ref, gn_ref, gate_ref, o_ref):
    z = z_ref[...]
    gl = z * _sigmoid(_mm(z, wg_ref[...]))
    out = _mm(gl, wo_ref[...])
    o_ref[...] = x_ref[...] + gate_ref[...] * _rms(out, gn_ref[...])


def _s5out_call(z, x, wg, wo, gn, gate, tm):
    bm, r, d = x.shape
    return pl.pallas_call(
        _s5out_kernel,
        grid=(bm, r // tm),
        in_specs=[_rows(tm, d), _rows(tm, d), _const(wg), _const(wo), _const(gn), _mod(gate, tm)],
        out_specs=_rows(tm, d),
        out_shape=jax.ShapeDtypeStruct((bm, r, d), F32),
        compiler_params=_params("arbitrary", "arbitrary"),
        name="s5_glu_out_proj",
    )(z, x, wg, wo, gn, gate)


def _ffn_kernel(x_ref, g_ref, sc_ref, sh_ref, wg_ref, wu_ref, wd_ref, gn_ref, gate_ref, o_ref,
                h_scr, acc_scr):
    f = pl.program_id(2)

    @pl.when(f == 0)
    def _():
        h = _rms(x_ref[...], g_ref[...]) * (1.0 + sc_ref[...]) + sh_ref[...]
        h_scr[...] = h.astype(h_scr.dtype)
        acc_scr[...] = jnp.zeros_like(acc_scr)

    h = h_scr[...]
    he = _silu(_mm(h, wg_ref[...])) * _mm(h, wu_ref[...])
    acc_scr[...] += _mm(he, wd_ref[...])

    @pl.when(f == pl.num_programs(2) - 1)
    def _():
        o_ref[...] = x_ref[...] + gate_ref[...] * _rms(acc_scr[...], gn_ref[...])


def _ffn_call(x, g, sc, sh, wg, wu, wd, gn, gate, tm, tf):
    bm, r, d = x.shape
    ff = wg.shape[1]
    return pl.pallas_call(
        _ffn_kernel,
        grid=(bm, r // tm, ff // tf),
        in_specs=[_rows(tm, d), _const(g), _mod(sc, tm), _mod(sh, tm),
                  pl.BlockSpec((d, tf), lambda b, i, f: (0, f)),
                  pl.BlockSpec((d, tf), lambda b, i, f: (0, f)),
                  pl.BlockSpec((tf, d), lambda b, i, f: (f, 0)),
                  _const(gn), _mod(gate, tm)],
        out_specs=_rows(tm, d),
        out_shape=jax.ShapeDtypeStruct((bm, r, d), F32),
        scratch_shapes=[pltpu.VMEM((tm, d), wg.dtype), pltpu.VMEM((tm, d), F32)],
        compiler_params=_params("arbitrary", "arbitrary", "arbitrary"),
        name="dense_swiglu",
    )(x, g, sc, sh, wg, wu, wd, gn, gate)


def _rope(t, cos, sin_lo, sin_hi):
    half = ROT_DIM // 2
    outs = []
    for s in range(t.shape[1] // LANES):
        ts = t[:, s * LANES:(s + 1) * LANES]
        outs.append(ts * cos + pltpu.roll(ts, LANES - half, 1) * sin_lo + pltpu.roll(ts, half, 1) * sin_hi)
    return jnp.concatenate(outs, axis=1)


def _qkv_values(x_ref, gkv_ref, g_ref, sc_ref, sh_ref, wkv_ref, wq_ref, cos_ref, slo_ref, shi_ref):
    x = x_ref[...]
    xn = x * lax.rsqrt(jnp.mean(x * x, axis=-1, keepdims=True) + RMS_EPS)
    kv = _mm(xn * gkv_ref[...], wkv_ref[...])
    q = _mm((xn * g_ref[...]) * (1.0 + sc_ref[...]) + sh_ref[...], wq_ref[...])
    cos, slo, shi = cos_ref[...], slo_ref[...], shi_ref[...]
    k = _rope(kv[:, :KV_DIM], cos, slo, shi)
    v = kv[:, KV_DIM:]
    q = _rope(q, cos, slo, shi) * (HEAD_DIM ** -0.5)
    return k, v, q


def _qkv_sample_kernel(*refs):
    k_ref, v_ref, q_ref = refs[10:]
    k, v, q = _qkv_values(*refs[:10])
    k_ref[...] = k
    v_ref[...] = v
    q_ref[...] = q


RESIDUE_STEP = 4


def _split_rows(val, scr, step):
    rows, slabs = val.shape[0], val.shape[1] // LANES
    for s in range(slabs):
        scr[s, 0:rows, :] = val[:, s * LANES:(s + 1) * LANES]
    n = rows // step
    return [jnp.concatenate([scr[s, pl.ds(r, n, stride=step), :] for s in range(slabs)], axis=1)
            for r in range(step)]


def _merge_rows(parts, scr):
    step, n, slabs = len(parts), parts[0].shape[0], parts[0].shape[1] // LANES
    for r in range(step):
        for s in range(slabs):
            scr[s, pl.ds(r, n, stride=step), :] = parts[r][:, s * LANES:(s + 1) * LANES]
    return jnp.concatenate([scr[s, 0:step * n, :] for s in range(slabs)], axis=1)


def _split_steps(dil):
    steps, done = [], 1
    while done < dil:
        steps.append(min(RESIDUE_STEP, dil // done))
        done *= steps[-1]
    return steps


def _residue_classes(val, scrs, dil):
    classes, done = [val], 1
    for step in _split_steps(dil):
        nxt = [None] * (done * step)
        for r, arr in enumerate(classes):
            for m, part in enumerate(_split_rows(arr, scrs[r % len(scrs)], step)):
                nxt[r + done * m] = part
        classes, done = nxt, done * step
    return classes


def _interleave_classes(classes, scrs):
    done = len(classes)
    for step in reversed(_split_steps(len(classes))):
        done //= step
        classes = [_merge_rows([classes[r + done * m] for m in range(step)], scrs[r % len(scrs)])
                   for r in range(done)]
    return classes[0]


def _store_residues(dst_ref, val, scrs, dil):
    for r, part in enumerate(_residue_classes(val, scrs, dil)):
        dst_ref[r] = part.astype(dst_ref.dtype)


def _dilation_classes(val, scrs):
    out, prev_dil, prev = {}, 1, [val]
    for _, dil in DILATION_GROUPS:
        nxt = [None] * dil
        for r, arr in enumerate(prev):
            for m, part in enumerate(_residue_classes(arr, scrs, dil // prev_dil)):
                nxt[r + prev_dil * m] = part
        out[dil], prev_dil, prev = nxt, dil, nxt
    return out


def _qkv_prompt_kernel(*refs):
    k_ref, v_ref = refs[10:12]
    kd_refs = refs[12:12 + N_DIL_GROUPS]
    vd_refs = refs[12 + N_DIL_GROUPS:12 + 2 * N_DIL_GROUPS]
    qd_refs = refs[12 + 2 * N_DIL_GROUPS:12 + 3 * N_DIL_GROUPS]
    scrs = refs[12 + 3 * N_DIL_GROUPS:]
    k, v, q = _qkv_values(*refs[:10])
    k_ref[...] = k
    v_ref[...] = v
    for val, dst_refs in ((k, kd_refs), (v, vd_refs)):
        classes = _dilation_classes(val, scrs)
        for g, (_, dil) in enumerate(DILATION_GROUPS):
            for r, part in enumerate(classes[dil]):
                dst_refs[g][r] = part.astype(dst_refs[g].dtype)
    for g, (_, dil) in enumerate(DILATION_GROUPS):
        _store_residues(qd_refs[g], q[:, g * KV_DIM:(g + 1) * KV_DIM], scrs, dil)


def _qkv_call(x, gkv, g, sc, sh, wkv, wq, cos, slo, shi, tm, prompt):
    bm, r, d = x.shape
    nq = wq.shape[1]
    tab = pl.BlockSpec((tm, LANES), lambda b, i: (i, 0))
    in_specs = [_rows(tm, d), _const(gkv), _const(g), _mod(sc, tm), _mod(sh, tm),
                _const(wkv), _const(wq), tab, tab, tab]
    out_specs = [_rows(tm, KV_DIM), _rows(tm, KV_DIM)]
    out_shape = [jax.ShapeDtypeStruct((bm, r, KV_DIM), F32), jax.ShapeDtypeStruct((bm, r, KV_DIM), F32)]
    if not prompt:
        return pl.pallas_call(
            _qkv_sample_kernel,
            grid=(bm, r // tm),
            in_specs=in_specs,
            out_specs=out_specs + [_rows(tm, nq)],
            out_shape=out_shape + [jax.ShapeDtypeStruct((bm, r, nq), F32)],
            compiler_params=_params("arbitrary", "arbitrary"),
            name="qkv_proj_rope_sample",
        )(x, gkv, g, sc, sh, wkv, wq, cos, slo, shi)
    for _ in range(3):
        for _, dil in DILATION_GROUPS:
            out_specs.append(pl.BlockSpec((None, dil, tm // dil, KV_DIM), lambda b, i: (b, 0, i, 0)))
            out_shape.append(jax.ShapeDtypeStruct((bm, dil, r // dil, KV_DIM), BF16))
    outs = pl.pallas_call(
        _qkv_prompt_kernel,
        grid=(bm, r // tm),
        in_specs=in_specs,
        out_specs=out_specs,
        out_shape=out_shape,
        scratch_shapes=[pltpu.VMEM((KV_DIM // LANES, tm, LANES), F32)] * 2,
        compiler_params=_params("arbitrary", "arbitrary"),
        name="qkv_proj_rope_prompt",
    )(x, gkv, g, sc, sh, wkv, wq, cos, slo, shi)
    n = N_DIL_GROUPS
    return outs[0], outs[1], outs[2:2 + n], outs[2 + n:2 + 2 * n], outs[2 + 2 * n:2 + 3 * n]


def _band_kernel(q_ref, kp_ref, kc_ref, vp_ref, vc_ref, o_ref, l_ref, *, nb):
    qi = lax.broadcasted_iota(jnp.int32, (SPAN, 2 * SPAN), 0)
    kj = lax.broadcasted_iota(jnp.int32, (SPAN, 2 * SPAN), 1)
    band = (kj >= qi) & (kj <= qi + SPAN)
    band_first = band & (kj >= jnp.where(pl.program_id(2) == 0, SPAN, 0))
    lane = lax.broadcasted_iota(jnp.int32, (SPAN, LANES), 1)
    head0 = lane < HEAD_DIM
    for i in range(nb):
        cur = slice(i * SPAN, (i + 1) * SPAN)
        prev = slice((i - 1) * SPAN, i * SPAN)
        for hp in range(KV_DIM // LANES):
            cols = slice(hp * LANES, (hp + 1) * LANES)
            q2 = q_ref[cur, cols]
            if i == 0:
                k2 = jnp.concatenate([kp_ref[:, cols], kc_ref[cur, cols]], axis=0)
                v2 = jnp.concatenate([vp_ref[:, cols], vc_ref[cur, cols]], axis=0)
                mask = band_first
            else:
                k2 = jnp.concatenate([kc_ref[prev, cols], kc_ref[cur, cols]], axis=0)
                v2 = jnp.concatenate([vc_ref[prev, cols], vc_ref[cur, cols]], axis=0)
                mask = band
            o_pair = None
            l_pair = None
            for hh in range(2):
                sel = head0 if hh == 0 else jnp.logical_not(head0)
                qm = jnp.where(sel, q2, jnp.zeros_like(q2))
                s = lax.dot_general(qm, k2, (((1,), (1,)), ((), ())), preferred_element_type=F32)
                s = jnp.where(mask, s, NEG_INF)
                m = jnp.max(s, axis=-1, keepdims=True)
                p = jnp.exp(s - m)
                den = jnp.sum(p, axis=-1, keepdims=True)
                o = jnp.dot(p.astype(BF16), v2, preferred_element_type=F32) / den
                lse = jnp.broadcast_to(m + jnp.log(den), (SPAN, LANES))
                o_pair = o if o_pair is None else jnp.where(head0, o_pair, o)
                l_pair = lse if l_pair is None else jnp.where(head0, l_pair, lse)
            o_ref[cur, cols] = o_pair.astype(o_ref.dtype)
            l_ref[cur, cols] = l_pair


def _band_call(q, k, v, nb):
    b, dil, m, _ = k.shape
    tq = nb * SPAN
    cur = pl.BlockSpec((None, None, tq, KV_DIM), lambda bb, r, n: (bb, r, n, 0))
    prev = pl.BlockSpec((None, None, SPAN, KV_DIM), lambda bb, r, n: (bb, r, jnp.maximum(n * nb - 1, 0), 0))
    return pl.pallas_call(
        functools.partial(_band_kernel, nb=nb),
        grid=(b, dil, m // tq),
        in_specs=[cur, prev, cur, prev, cur],
        out_specs=[cur, cur],
        out_shape=[jax.ShapeDtypeStruct((b, dil, m, KV_DIM), BF16),
                   jax.ShapeDtypeStruct((b, dil, m, KV_DIM), F32)],
        compiler_params=_params("arbitrary", "arbitrary", "arbitrary"),
        name=f"band_attention_dil{dil}",
    )(q, k, k, v, v)


def _transpose_small(x):
    c = x.shape[1]
    eye = jnp.where(lax.broadcasted_iota(jnp.int32, (c, c), 0) == lax.broadcasted_iota(jnp.int32, (c, c), 1),
                    1.0, 0.0).astype(F32)
    return lax.dot_general(eye, x, (((1,), (1,)), ((), ())), preferred_element_type=F32, precision=HIGHEST)


def _sample_attn_kernel(q_ref, kn_ref, vn_ref, kc_ref, vc_ref, o_ref, ko_ref, vo_ref):
    cl = kc_ref.shape[-1]
    is_last = lax.broadcasted_iota(jnp.int32, (HEAD_DIM, cl), 1) == cl - 1
    kn_rows, vn_rows = kn_ref[0], vn_ref[0]
    kn_cols, vn_cols = _transpose_small(kn_rows), _transpose_small(vn_rows)
    for h in range(N_KV_HEADS):
        ko_ref[0, h] = jnp.where(is_last, kn_cols[:, h:h + 1], pltpu.roll(kc_ref[0, h], cl - 1, 1))
        vo_ref[0, h] = jnp.where(is_last, vn_cols[:, h:h + 1], pltpu.roll(vc_ref[0, h], cl - 1, 1))
    outs, lses = [], []
    for g, (win, dil) in enumerate(DILATION_GROUPS):
        q_rows = q_ref[0, g]
        q_cols = _transpose_small(q_rows)
        s = jnp.concatenate(
            [jnp.sum(kc_ref[0, h, :, cl - win:] * q_cols[:, h:h + 1], axis=0, keepdims=True)
             for h in range(N_KV_HEADS)], axis=0)
        if dil > 1:
            lane = lax.broadcasted_iota(jnp.int32, s.shape, 1)
            s = jnp.where((lane & (dil - 1)) == 0, s, NEG_INF)
        s_new = jnp.sum(kn_rows * q_rows, axis=-1, keepdims=True)
        m = jnp.maximum(jnp.max(s, axis=-1, keepdims=True), s_new)
        p = jnp.exp(s - m)
        p_new = jnp.exp(s_new - m)
        den = jnp.sum(p, axis=-1, keepdims=True) + p_new
        pn = p / den
        o_cols = jnp.concatenate(
            [jnp.sum(vc_ref[0, h, :, cl - win:] * pn[h:h + 1, :], axis=-1, keepdims=True)
             for h in range(N_KV_HEADS)], axis=1)
        outs.append(_transpose_small(o_cols) + (p_new / den) * vn_rows)
        lses.append(m + jnp.log(den))
    mx = jnp.maximum(jnp.maximum(lses[0], lses[1]), lses[2])
    es = [jnp.exp(l - mx) for l in lses]
    o_ref[0] = (es[0] * outs[0] + es[1] * outs[1] + es[2] * outs[2]) / (es[0] + es[1] + es[2])


def _sample_attn_call(q, k_new, v_new, cache_k, cache_v):
    n, nh, hd, cl = cache_k.shape
    col = pl.BlockSpec((1, nh, hd), lambda i: (i, 0, 0))
    cache = pl.BlockSpec((1, nh, hd, cl), lambda i: (i, 0, 0, 0))
    return pl.pallas_call(
        _sample_attn_kernel,
        grid=(n,),
        in_specs=[pl.BlockSpec((1, N_DIL_GROUPS, nh, hd), lambda i: (i, 0, 0, 0)), col, col,
                  cache, cache],
        out_specs=[col, cache, cache],
        out_shape=[jax.ShapeDtypeStruct((n, nh, hd), F32),
                   jax.ShapeDtypeStruct(cache_k.shape, cache_k.dtype),
                   jax.ShapeDtypeStruct(cache_v.shape, cache_v.dtype)],
        compiler_params=_params("arbitrary"),
        name="window_attention_cache_append",
    )(q, k_new, v_new, cache_k, cache_v)


def _attn_out_kernel(o1_ref, o2_ref, o3_ref, l1_ref, l2_ref, l3_ref, x_ref, wo_ref, gn_ref, gate_ref,
                     y_ref, *scrs):
    vals = []
    for i, (ref, (_, dil)) in enumerate(zip((o1_ref, o2_ref, o3_ref, l1_ref, l2_ref, l3_ref),
                                            DILATION_GROUPS + DILATION_GROUPS)):
        classes = [ref[r].astype(F32) for r in range(dil)]
        vals.append(_interleave_classes(classes, scrs[2 * (i % 2):2 * (i % 2) + 2]))
    o1, o2, o3, l1, l2, l3 = vals
    mx = jnp.maximum(jnp.maximum(l1, l2), l3)
    e1, e2, e3 = jnp.exp(l1 - mx), jnp.exp(l2 - mx), jnp.exp(l3 - mx)
    o = (e1 * o1 + e2 * o2 + e3 * o3) / (e1 + e2 + e3)
    out = _mm(o, wo_ref[...])
    y_ref[...] = x_ref[...] + gate_ref[...] * _rms(out, gn_ref[...])


def _attn_out_call(os_, ls_, x, wo, gn, gate, tm):
    bm, r, d = x.shape
    res = [pl.BlockSpec((None, dil, tm // dil, KV_DIM), lambda b, i: (b, 0, i, 0))
           for _, dil in DILATION_GROUPS]
    return pl.pallas_call(
        _attn_out_kernel,
        grid=(bm, r // tm),
        in_specs=res + res + [_rows(tm, d), _const(wo), _const(gn), _mod(gate, tm)],
        out_specs=_rows(tm, d),
        out_shape=jax.ShapeDtypeStruct((bm, r, d), F32),
        scratch_shapes=[pltpu.VMEM((KV_DIM // LANES, tm, LANES), F32)] * 4,
        compiler_params=_params("arbitrary", "arbitrary"),
        name="attn_merge_out_proj",
    )(*os_, *ls_, x, wo, gn, gate)


def _proj_res_kernel(o_ref, x_ref, wo_ref, gn_ref, gate_ref, y_ref):
    out = _mm(o_ref[...], wo_ref[...])
    y_ref[...] = x_ref[...] + gate_ref[...] * _rms(out, gn_ref[...])


def _proj_res_call(o, x, wo, gn, gate, tm):
    bm, r, d = x.shape
    return pl.pallas_call(
        _proj_res_kernel,
        grid=(bm, r // tm),
        in_specs=[_rows(tm, o.shape[2]), _rows(tm, d), _const(wo), _const(gn), _mod(gate, tm)],
        out_specs=_rows(tm, d),
        out_shape=jax.ShapeDtypeStruct((bm, r, d), F32),
        compiler_params=_params("arbitrary", "arbitrary"),
        name="attn_out_proj",
    )(o, x, wo, gn, gate)


def _top2(logits):
    lane = lax.broadcasted_iota(jnp.int32, logits.shape, 1).astype(F32)
    m1 = jnp.max(logits, axis=-1, keepdims=True)
    i1 = jnp.min(jnp.where(logits == m1, lane, float(LANES)), axis=-1, keepdims=True)
    rest = jnp.where(lane == i1, -jnp.inf, logits)
    m2 = jnp.max(rest, axis=-1, keepdims=True)
    i2 = jnp.min(jnp.where(rest == m2, lane, float(LANES)), axis=-1, keepdims=True)
    e = jnp.exp(m2 - m1)
    return lane, i1, i2, 1.0 / (1.0 + e), e / (1.0 + e)


def _route_kernel(x_ref, g_ref, sc_ref, sh_ref, wr_ref, br_ref, h_ref, route_ref, cnt_ref, carry_scr,
                  *, stride):
    @pl.when((pl.program_id(0) == 0) & (pl.program_id(1) == 0))
    def _():
        carry_scr[...] = jnp.zeros_like(carry_scr)

    h = _rms(x_ref[...], g_ref[...]) * (1.0 + sc_ref[...]) + sh_ref[...]
    tm = h.shape[0]
    for c in range(SLABS):
        h_ref[pl.ds(c, tm, stride=SUBLANES), :] = h[:, c * LANES:(c + 1) * LANES]
    logits = _mm(h, wr_ref[...]) + br_ref[...]
    lane, i1, i2, g1, g2 = _top2(logits)
    onehot = jnp.where(lane == i1, 1.0, 0.0) + jnp.where(lane == i2, 1.0, 0.0)
    row = lax.broadcasted_iota(jnp.int32, (tm, tm), 0)
    col = lax.broadcasted_iota(jnp.int32, (tm, tm), 1)
    earlier = jnp.where(col < row, 1.0, 0.0).astype(BF16)
    before = jnp.dot(earlier, onehot.astype(BF16), preferred_element_type=F32) + carry_scr[0:1, :]
    r1 = jnp.sum(jnp.where(lane == i1, before, 0.0), axis=-1, keepdims=True)
    r2 = jnp.sum(jnp.where(lane == i2, before, 0.0), axis=-1, keepdims=True)
    p1 = i1 * float(stride) + r1
    p2 = i2 * float(stride) + r2
    route_ref[...] = (jnp.where(lane == 0.0, p1, 0.0) + jnp.where(lane == 1.0, p2, 0.0)
                      + jnp.where(lane == 2.0, g1, 0.0) + jnp.where(lane == 3.0, g2, 0.0))
    carry = carry_scr[...] + jnp.sum(onehot, axis=0, keepdims=True)
    carry_scr[...] = carry
    cnt_ref[...] = carry


def _route_call(x, g, sc, sh, wr, br, tm, stride):
    bm, r, d = x.shape
    nt = r // tm
    return pl.pallas_call(
        functools.partial(_route_kernel, stride=stride),
        grid=(bm, nt),
        in_specs=[_rows(tm, d), _const(g), _mod(sc, tm), _mod(sh, tm), _const(wr), _const(br)],
        out_specs=[pl.BlockSpec((tm * SUBLANES, LANES), lambda b, i: (b * nt + i, 0)),
                   _rows(tm, LANES),
                   pl.BlockSpec((SUBLANES, LANES), lambda b, i: (0, 0))],
        out_shape=[jax.ShapeDtypeStruct((bm * r * SUBLANES, LANES), F32),
                   jax.ShapeDtypeStruct((bm, r, LANES), F32),
                   jax.ShapeDtypeStruct((SUBLANES, LANES), F32)],
        scratch_shapes=[pltpu.VMEM((SUBLANES, LANES), F32)],
        compiler_params=_params("arbitrary", "arbitrary"),
        name="moe_route",
    )(x, g, sc, sh, wr, br)


def _token_tile(ref, idx):
    return ref.at[pl.ds(pl.multiple_of(idx * SUBLANES, SUBLANES), SUBLANES), :]


def _dispatch_kernel(cnt_ref, pos_ref, h_ref, xs_hbm, zero_scr, sem, zsem, *, tm, stride, tmx):
    i = pl.program_id(0)

    def issue(t, carry):
        src = _token_tile(h_ref, t)
        for k in range(2):
            pltpu.make_async_copy(src, _token_tile(xs_hbm, pos_ref[0, k * tm + t]), sem).start(priority=k)
        return carry

    lax.fori_loop(0, tm, issue, 0)
    for _ in range(2):
        pltpu.make_async_copy(h_ref, xs_hbm.at[pl.ds(0, tm * SUBLANES), :], sem).wait()

    @pl.when(i == pl.num_programs(0) - 1)
    def _():
        zero_scr[...] = jnp.zeros_like(zero_scr)
        copies = []
        for e in range(N_EXPERTS):
            start = pl.multiple_of((e * stride + cnt_ref[e]) * SUBLANES, SUBLANES)
            copies.append(pltpu.make_async_copy(zero_scr, xs_hbm.at[pl.ds(start, tmx * SUBLANES), :], zsem))
            copies[-1].start()
        for cp in copies:
            cp.wait()


def _dispatch_call(cnt, pos, h2d, tm, stride, tmx):
    n_tiles = pos.shape[0]
    rows = N_EXPERTS * stride * SUBLANES
    return pl.pallas_call(
        functools.partial(_dispatch_kernel, tm=tm, stride=stride, tmx=tmx),
        grid_spec=pltpu.PrefetchScalarGridSpec(
            num_scalar_prefetch=1,
            grid=(n_tiles,),
            in_specs=[pl.BlockSpec((None, 1, 2 * tm), lambda i, cnt: (i, 0, 0), memory_space=pltpu.SMEM),
                      pl.BlockSpec((tm * SUBLANES, LANES), lambda i, cnt: (i, 0))],
            out_specs=pl.BlockSpec(memory_space=pl.ANY),
            scratch_shapes=[pltpu.VMEM((tmx * SUBLANES, LANES), F32),
                            pltpu.SemaphoreType.DMA(()), pltpu.SemaphoreType.DMA(())],
        ),
        out_shape=jax.ShapeDtypeStruct((rows, LANES), F32),
        compiler_params=_params("arbitrary"),
        name="moe_dispatch",
    )(cnt, pos, h2d)


def _expert_kernel(blk_ref, exp_ref, nused_ref, xs_ref, wg_ref, wu_ref, wd_ref, o_ref, x_scr, acc_scr,
                   *, tmx):
    j = pl.program_id(0)
    f = pl.program_id(1)

    @pl.when(j < nused_ref[0])
    def _():
        @pl.when(f == 0)
        def _():
            for c in range(SLABS):
                x_scr[:, c * LANES:(c + 1) * LANES] = (
                    xs_ref[pl.ds(c, tmx, stride=SUBLANES), :].astype(x_scr.dtype))
            acc_scr[...] = jnp.zeros_like(acc_scr)

        x = x_scr[...]
        he = _silu(_mm(x, wg_ref[...])) * _mm(x, wu_ref[...])
        acc_scr[...] += _mm(he, wd_ref[...])

        @pl.when(f == pl.num_programs(1) - 1)
        def _():
            for c in range(SLABS):
                o_ref[pl.ds(c, tmx, stride=SUBLANES), :] = acc_scr[:, c * LANES:(c + 1) * LANES]


def _expert_call(blk, exp, nused, xs, wg, wu, wd, tmx, tf):
    ne, d, ff = wg.shape
    nf = ff // tf

    def fidx(j, f, nused):
        return jnp.where(j < nused[0], f, nf - 1)

    row_spec = pl.BlockSpec((tmx * SUBLANES, LANES), lambda j, f, blk, exp, nu: (blk[j], 0))
    return pl.pallas_call(
        functools.partial(_expert_kernel, tmx=tmx),
        grid_spec=pltpu.PrefetchScalarGridSpec(
            num_scalar_prefetch=3,
            grid=(blk.shape[0], nf),
            in_specs=[row_spec,
                      pl.BlockSpec((None, d, tf), lambda j, f, blk, exp, nu: (exp[j], 0, fidx(j, f, nu))),
                      pl.BlockSpec((None, d, tf), lambda j, f, blk, exp, nu: (exp[j], 0, fidx(j, f, nu))),
                      pl.BlockSpec((None, tf, d), lambda j, f, blk, exp, nu: (exp[j], fidx(j, f, nu), 0))],
            out_specs=row_spec,
            scratch_shapes=[pltpu.VMEM((tmx, d), wg.dtype), pltpu.VMEM((tmx, d), F32)],
        ),
        out_shape=jax.ShapeDtypeStruct(xs.shape, F32),
        compiler_params=_params("arbitrary", "arbitrary"),
        name="moe_grouped_experts",
    )(blk, exp, nused, xs, wg, wu, wd)


def _combine_kernel(pos_ref, route_ref, x_ref, gn_ref, gate_ref, out_hbm, y_ref, buf, sem, *, tm):
    def issue(t, carry):
        for k in range(2):
            pltpu.make_async_copy(_token_tile(out_hbm, pos_ref[0, k * tm + t]),
                                  _token_tile(buf, k * tm + t), sem).start(priority=k)
        return carry

    lax.fori_loop(0, tm, issue, 0)
    pltpu.make_async_copy(out_hbm.at[pl.ds(0, 2 * tm * SUBLANES), :], buf, sem).wait()
    route = route_ref[...]
    g1, g2 = route[:, 2:3], route[:, 3:4]
    parts = []
    for c in range(SLABS):
        parts.append(g1 * buf[pl.ds(c, tm, stride=SUBLANES), :]
                     + g2 * buf[pl.ds(tm * SUBLANES + c, tm, stride=SUBLANES), :])
    y = jnp.concatenate(parts, axis=1)
    y_ref[...] = x_ref[...] + gate_ref[...] * _rms(y, gn_ref[...])


def _combine_call(pos, route, x, gn, gate, out2d, tm):
    bm, r, d = x.shape
    nt = r // tm
    return pl.pallas_call(
        functools.partial(_combine_kernel, tm=tm),
        grid=(bm, nt),
        in_specs=[pl.BlockSpec((None, 1, 2 * tm), lambda b, i: (b * nt + i, 0, 0), memory_space=pltpu.SMEM),
                  _rows(tm, LANES), _rows(tm, d), _const(gn), _mod(gate, tm),
                  pl.BlockSpec(memory_space=pl.ANY)],
        out_specs=_rows(tm, d),
        out_shape=jax.ShapeDtypeStruct((bm, r, d), F32),
        scratch_shapes=[pltpu.VMEM((2 * tm * SUBLANES, LANES), F32), pltpu.SemaphoreType.DMA(())],
        compiler_params=_params("arbitrary", "arbitrary"),
        name="moe_combine",
    )(pos, route, x, gn, gate, out2d)


def _moe_sparse(x, g, sc, sh, wr, br, wg, wu, wd, gn, gate, tm):
    bm, r, d = x.shape
    m = bm * r
    tmx = MOE_TILE_ROWS
    stride = m + tmx
    n_tiles = 2 * m // tmx + N_EXPERTS
    h2d, route, cnt = _route_call(x, g, sc, sh, wr, br, tm, stride)
    cnt = cnt[0, :N_EXPERTS].astype(jnp.int32)
    pos = route[..., :2].astype(jnp.int32).reshape(m // tm, tm, 2).transpose(0, 2, 1).reshape(m // tm, 1, 2 * tm)
    xs = _dispatch_call(cnt, pos, h2d, tm, stride, tmx)
    per = (cnt + tmx - 1) // tmx
    ends = jnp.cumsum(per)
    j = jnp.minimum(jnp.arange(n_tiles, dtype=jnp.int32), ends[-1] - 1)
    exp = jnp.sum((j[:, None] >= ends[None, :]).astype(jnp.int32), axis=1)
    blk = exp * (stride // tmx) + j - (ends - per)[exp]
    out2d = _expert_call(blk, exp, ends[-1:], xs, wg, wu, wd, tmx, wg.shape[2] // 2)
    return _combine_call(pos, route, x, gn, gate, out2d, tm)


def _router_kernel(x_ref, g_ref, sc_ref, sh_ref, wr_ref, br_ref, h_ref, comb_ref):
    h = _rms(x_ref[...], g_ref[...]) * (1.0 + sc_ref[...]) + sh_ref[...]
    h_ref[...] = h.astype(h_ref.dtype)
    logits = _mm(h, wr_ref[...]) + br_ref[...]
    lane, i1, i2, g1, g2 = _top2(logits)
    comb_ref[...] = jnp.where(lane == i1, g1, 0.0) + jnp.where(lane == i2, g2, 0.0)


def _router_call(x, g, sc, sh, wr, br, h_dtype, tm):
    bm, r, d = x.shape
    return pl.pallas_call(
        _router_kernel,
        grid=(bm, r // tm),
        in_specs=[_rows(tm, d), _const(g), _mod(sc, tm), _mod(sh, tm), _const(wr), _const(br)],
        out_specs=[_rows(tm, d), _rows(tm, LANES)],
        out_shape=[jax.ShapeDtypeStruct((bm, r, d), h_dtype),
                   jax.ShapeDtypeStruct((bm, r, LANES), F32)],
        compiler_params=_params("arbitrary", "arbitrary"),
        name="moe_router",
    )(x, g, sc, sh, wr, br)


def _moe_kernel(h_ref, comb_ref, x_ref, wg_ref, wu_ref, wd_ref, gn_ref, gate_ref,
                o_ref, wgb_ref, wub_ref, wdb_ref, acc_scr):
    e = pl.program_id(2)
    f = pl.program_id(3)

    @pl.when((e == 0) & (f == 0))
    def _():
        acc_scr[...] = jnp.zeros_like(acc_scr)

    comb = comb_ref[...]
    lane = lax.broadcasted_iota(jnp.int32, comb.shape, 1)
    cw = jnp.sum(jnp.where(lane == e, comb, 0.0), axis=-1, keepdims=True)
    wg, wu, wd = wg_ref[...].astype(BF16), wu_ref[...].astype(BF16), wd_ref[...].astype(BF16)
    wgb_ref[...] = wg
    wub_ref[...] = wu
    wdb_ref[...] = wd
    h = h_ref[...]
    he = _silu(_mm(h, wg)) * _mm(h, wu)
    acc_scr[...] += cw * _mm(he, wd)

    @pl.when((e == pl.num_programs(2) - 1) & (f == pl.num_programs(3) - 1))
    def _():
        o_ref[...] = x_ref[...] + gate_ref[...] * _rms(acc_scr[...], gn_ref[...])


def _moe_call(h, comb, x, wg, wu, wd, gn, gate, tm, tf):
    bm, r, d = x.shape
    ne, _, ff = wg.shape
    assert bm == 1 and r == tm
    up = pl.BlockSpec((None, d, tf), lambda b, i, e, f: (e, 0, f))
    down = pl.BlockSpec((None, tf, d), lambda b, i, e, f: (e, f, 0))
    return pl.pallas_call(
        _moe_kernel,
        grid=(bm, r // tm, ne, ff // tf),
        in_specs=[_rows(tm, d), _rows(tm, LANES), _rows(tm, d), up, up, down, _const(gn), _mod(gate, tm)],
        out_specs=[_rows(tm, d), up, up, down],
        out_shape=[jax.ShapeDtypeStruct((bm, r, d), F32), jax.ShapeDtypeStruct(wg.shape, BF16),
                   jax.ShapeDtypeStruct(wu.shape, BF16), jax.ShapeDtypeStruct(wd.shape, BF16)],
        scratch_shapes=[pltpu.VMEM((tm, d), F32)],
        compiler_params=_params("arbitrary", "arbitrary", "arbitrary", "arbitrary"),
        name="moe_experts",
    )(h, comb, x, wg, wu, wd, gn, gate)


def _ssm_params(lam_re, lam_im, log_dt, b_re, b_im, c_re, c_im, wdtype):
    dt = jnp.exp(log_dt)[:, None]
    mag = jnp.exp(lam_re * dt)
    a_re, a_im = mag * jnp.cos(lam_im * dt), mag * jnp.sin(lam_im * dt)
    den = lam_re * lam_re + lam_im * lam_im
    f_re = ((a_re - 1.0) * lam_re + a_im * lam_im) / den
    f_im = (a_im * lam_re - (a_re - 1.0) * lam_im) / den
    bb_re = f_re[..., None] * b_re - f_im[..., None] * b_im
    bb_im = f_re[..., None] * b_im + f_im[..., None] * b_re
    eye = jnp.eye(GROUPS_PER_SLAB, dtype=F32)

    def in_map(bb):
        t = bb.reshape(SLABS, GROUPS_PER_SLAB, SSM_STATE, SSM_CH)
        w = jnp.einsum('sgpc,gh->sgchp', t, eye)
        return w.reshape(SLABS, LANES, SLAB_STATES).astype(wdtype)

    def out_map(cc):
        t = cc.reshape(SLABS, GROUPS_PER_SLAB, SSM_CH, SSM_STATE)
        w = jnp.einsum('sgcp,gh->sgphc', t, eye)
        return w.reshape(SLABS, SLAB_STATES, LANES).astype(wdtype)

    return a_re, a_im, in_map(bb_re), in_map(bb_im), out_map(c_re), out_map(c_im)


def _rope_tables(pos):
    half = ROT_DIM // 2
    inv = ROPE_THETA ** (-jnp.arange(half, dtype=F32) / half)
    ang = pos.astype(F32)[:, None] * inv[None, :]
    cos, sin = jnp.cos(ang), jnp.sin(ang)
    n = pos.shape[0]
    pad = jnp.zeros((n, HEAD_DIM - ROT_DIM), F32)
    zero = jnp.zeros((n, half), F32)
    cos_h = jnp.concatenate([cos, cos, pad + 1.0], axis=1)
    lo_h = jnp.concatenate([-sin, zero, pad], axis=1)
    hi_h = jnp.concatenate([zero, sin, pad], axis=1)
    rep = LANES // HEAD_DIM
    return jnp.tile(cos_h, (1, rep)), jnp.tile(lo_h, (1, rep)), jnp.tile(hi_h, (1, rep))


def _trunk(x, mods, pos, wts, prompt, ssm0=None, cache=None):
    bm, r, d = x.shape
    tm = 512 if prompt else r
    wd = BF16 if prompt else F32
    cast = lambda w: w.astype(wd)
    norm_g = wts['norm_g'].reshape(DEPTH, 4, 1, d)

    sh1, sc1, g1, sh2, sc2, g2 = mods[0]
    u = _inproj_call(x, norm_g[0, 0], sc1, sh1, cast(wts['ssm_w_in'][0]), tm)
    a_re, a_im = wts['ssm_prepared'][:2]
    wbr, wbi, cr, ci = (cast(w) for w in wts['ssm_prepared'][2:])
    dvec = wts['ssm_d'][0].reshape(1, d)
    if prompt:
        z, fr, fi = _ssm_prompt_call(u, wbr, wbi, a_re.reshape(STATE_TILES, LANES),
                                     a_im.reshape(STATE_TILES, LANES),
                                     jnp.concatenate([cr, -ci], axis=1), dvec, SCAN_CHUNK)
    else:
        z, fr, fi = _ssm_sample_call(u[0], ssm0[0].reshape(r, SSM_FLAT), ssm0[1].reshape(r, SSM_FLAT),
                                     wbr, wbi, a_re.reshape(1, SSM_FLAT), a_im.reshape(1, SSM_FLAT),
                                     cr, ci, dvec)
        z = z[None]
    fr = fr.reshape(-1, SSM_GROUPS, SSM_STATE)[None]
    fi = fi.reshape(-1, SSM_GROUPS, SSM_STATE)[None]
    x = _s5out_call(z, x, cast(wts['ssm_w_glu'][0]), cast(wts['ssm_w_out'][0]), norm_g[0, 1], g1, tm)
    tf = wts['ffn_w_gate'].shape[2] // 2
    x = _ffn_call(x, norm_g[0, 2], sc2, sh2, cast(wts['ffn_w_gate'][0]), cast(wts['ffn_w_up'][0]),
                  cast(wts['ffn_w_down'][0]), norm_g[0, 3], g2, tm, tf)

    sh1, sc1, g1, sh2, sc2, g2 = mods[1]
    cos, slo, shi = _rope_tables(pos)
    qkv = _qkv_call(x, wts['kv_norm_g'].reshape(1, d), norm_g[1, 0], sc1, sh1, cast(wts['w_kv']),
                    cast(wts['attn_w_q'][0]), cos, slo, shi, tm, prompt)
    wo = cast(wts['attn_w_o'][0])
    wr = cast(jnp.zeros((d, LANES), F32).at[:, :N_EXPERTS].set(wts['moe_w_router'][0]))
    br = jnp.full((1, LANES), NEG_INF, F32).at[0, :N_EXPERTS].set(wts['moe_b_router'][0])
    experts = None
    if prompt:
        wg, wu, wdn = wts['experts_bf16']
        k, v, kd, vd, qd = qkv
        os_, ls_ = [], []
        for g, (_, dil) in enumerate(DILATION_GROUPS):
            o, lse = _band_call(qd[g], kd[g], vd[g], min(4, r // dil // SPAN))
            os_.append(o)
            ls_.append(lse)
        x = _attn_out_call(os_, ls_, x, wo, norm_g[1, 1], g1, tm)
        x = _moe_sparse(x, norm_g[1, 2], sc2, sh2, wr, br, wg, wu, wdn, norm_g[1, 3], g2, tm)
    else:
        k, v, q = qkv
        o, k, v = _sample_attn_call(q[0].reshape(r, N_DIL_GROUPS, N_KV_HEADS, HEAD_DIM),
                                    k[0].reshape(r, N_KV_HEADS, HEAD_DIM),
                                    v[0].reshape(r, N_KV_HEADS, HEAD_DIM),
                                    cache[0].transpose(0, 2, 3, 1), cache[1].transpose(0, 2, 3, 1))
        k, v = k.transpose(0, 3, 1, 2), v.transpose(0, 3, 1, 2)
        x = _proj_res_call(o.reshape(1, r, KV_DIM), x, wo, norm_g[1, 1], g1, tm)
        h, comb = _router_call(x, norm_g[1, 2], sc2, sh2, wr, br, wd, tm)
        wg, wu, wdn = wts['moe_w_gate'][0], wts['moe_w_up'][0], wts['moe_w_down'][0]
        x, *experts = _moe_call(h, comb, x, wg, wu, wdn, norm_g[1, 3], g2, tm, wg.shape[2] // 4)
    return x, fr, fi, k, v, experts


def kernel(x_prompt, x_sample, state_ssm_re, state_ssm_im, cache_k, cache_v, c_prompt, c_sample, ada_w, ada_b, norm_g, ssm_w_in, ssm_lam_re, ssm_lam_im, ssm_log_dt, ssm_b_re, ssm_b_im, ssm_c_re, ssm_c_im, ssm_d, ssm_w_glu, ssm_w_out, kv_norm_g, w_kv, attn_w_q, attn_w_o, ffn_w_gate, ffn_w_up, ffn_w_down, moe_w_router, moe_b_router, moe_w_gate, moe_w_up, moe_w_down):
    wts = dict(norm_g=norm_g, ssm_w_in=ssm_w_in, ssm_lam_re=ssm_lam_re, ssm_lam_im=ssm_lam_im,
               ssm_log_dt=ssm_log_dt, ssm_b_re=ssm_b_re, ssm_b_im=ssm_b_im, ssm_c_re=ssm_c_re,
               ssm_c_im=ssm_c_im, ssm_d=ssm_d, ssm_w_glu=ssm_w_glu, ssm_w_out=ssm_w_out,
               kv_norm_g=kv_norm_g, w_kv=w_kv, attn_w_q=attn_w_q, attn_w_o=attn_w_o,
               ffn_w_gate=ffn_w_gate, ffn_w_up=ffn_w_up, ffn_w_down=ffn_w_down,
               moe_w_router=moe_w_router, moe_b_router=moe_b_router, moe_w_gate=moe_w_gate,
               moe_w_up=moe_w_up, moe_w_down=moe_w_down)
    nb, seq, d = x_prompt.shape
    ns = x_sample.shape[0]
    assert x_sample.shape[1] == 1 and ns % SUBLANES == 0

    pad = (-(ns + nb)) % SUBLANES
    c_all = jnp.concatenate([c_sample, c_prompt, jnp.zeros((pad, d), F32)], axis=0)
    mod_all = _ada_call(c_all, ada_w, ada_b)
    mods_s = [[mod_all[l, k, :ns][None] for k in range(6)] for l in range(DEPTH)]
    mods_p = [[mod_all[l, k, ns:ns + nb][:, None, :] for k in range(6)] for l in range(DEPTH)]

    pos_p = jnp.arange(seq, dtype=jnp.int32)
    pos_s = jnp.full((ns,), PAST_LEN, dtype=jnp.int32)
    wts['ssm_prepared'] = _ssm_params(ssm_lam_re[0], ssm_lam_im[0], ssm_log_dt[0], ssm_b_re[0], ssm_b_im[0],
                                      ssm_c_re[0], ssm_c_im[0], F32)
    xs = x_sample.reshape(1, ns, d)
    y_s, sr_s, si_s, k_s, v_s, wts['experts_bf16'] = _trunk(
        xs, mods_s, pos_s, wts, False, ssm0=(state_ssm_re[0], state_ssm_im[0]), cache=(cache_k, cache_v))
    y_p, sr_p, si_p, k_p, v_p, _ = _trunk(x_prompt, mods_p, pos_p, wts, True)

    keep = min(MAX_WINDOW, seq)
    k_rows_p = k_p[:, seq - keep:].reshape(nb, keep, N_KV_HEADS, HEAD_DIM)
    v_rows_p = v_p[:, seq - keep:].reshape(nb, keep, N_KV_HEADS, HEAD_DIM)
    return (y_p, y_s.reshape(ns, 1, d), sr_p, si_p, sr_s, si_s, k_rows_p, v_rows_p, k_s, v_s)
```

```python
import functools
import math

import jax
import jax.numpy as jnp
from jax import lax
from jax.experimental import pallas as pl
from jax.experimental.pallas import tpu as pltpu

F32 = jnp.float32
BF16 = jnp.bfloat16
HIGHEST = lax.Precision.HIGHEST

D_MODEL = 1024
DEPTH = 2
PAST_LEN = 16384
SSM_CH = 16
SSM_GROUPS = D_MODEL // SSM_CH
SSM_STATE = 64
SSM_FLAT = SSM_GROUPS * SSM_STATE
HEAD_DIM = 64
N_KV_HEADS = D_MODEL // 128
KV_DIM = N_KV_HEADS * HEAD_DIM
DILATION_GROUPS = ((128, 1), (512, 4), (2048, 16))
N_DIL_GROUPS = len(DILATION_GROUPS)
SPAN = 128
MAX_WINDOW = 2048
ROT_DIM = HEAD_DIM // 4
ROPE_THETA = 500000.0
N_EXPERTS = 8
RMS_EPS = 1e-6
NEG_INF = -1e30

LANES = 128
SUBLANES = 8
SLABS = D_MODEL // LANES
GROUPS_PER_SLAB = LANES // SSM_CH
SLAB_STATES = GROUPS_PER_SLAB * SSM_STATE
STATE_TILES = SSM_FLAT // LANES
SCAN_STEP_PITCH = 2
SCAN_TILE_PITCH = 15
SCAN_SLAB_ROWS = 480
SCAN_CHUNK = 128
MOE_TILE_ROWS = 512
VMEM_LIMIT = 56 * 1024 * 1024


def _params(*sem):
    return pltpu.CompilerParams(dimension_semantics=sem, vmem_limit_bytes=VMEM_LIMIT)


def _mm(a, w, precise=True):
    if w.dtype == BF16 or not precise:
        return jnp.dot(a.astype(BF16), w.astype(BF16), preferred_element_type=F32)
    return jnp.dot(a.astype(F32), w, preferred_element_type=F32, precision=HIGHEST)


def _rms(x, g):
    return x * lax.rsqrt(jnp.mean(x * x, axis=-1, keepdims=True) + RMS_EPS) * g


def _sigmoid(x):
    return 1.0 / (1.0 + jnp.exp(-x))


def _silu(x):
    return x * _sigmoid(x)


def _gelu_tanh(x):
    return 0.5 * x * (1.0 + jnp.tanh(math.sqrt(2.0 / math.pi) * (x + 0.044715 * (x * x * x))))


def _rows(tm, n):
    return pl.BlockSpec((None, tm, n), lambda b, i, *_: (b, i, 0))


def _mod(arr, tm):
    if arr.shape[1] == 1:
        return pl.BlockSpec((None, 1, arr.shape[2]), lambda b, i, *_: (b, 0, 0))
    return pl.BlockSpec((None, tm, arr.shape[2]), lambda b, i, *_: (b, i, 0))


def _const(arr):
    nd = arr.ndim
    return pl.BlockSpec(arr.shape, lambda *_: (0,) * nd)


def _ada_kernel(c_ref, w_ref, b_ref, o_ref):
    o_ref[...] = _mm(_silu(c_ref[...]), w_ref[...]) + b_ref[...]


def _ada_call(c_all, ada_w, ada_b):
    rows = c_all.shape[0]
    b4 = ada_b.reshape(DEPTH, 6, 1, D_MODEL)
    return pl.pallas_call(
        _ada_kernel,
        grid=(DEPTH, 6),
        in_specs=[pl.BlockSpec((rows, D_MODEL), lambda l, k: (0, 0)),
                  pl.BlockSpec((None, D_MODEL, D_MODEL), lambda l, k: (l, 0, k)),
                  pl.BlockSpec((None, None, 1, D_MODEL), lambda l, k: (l, k, 0, 0))],
        out_specs=pl.BlockSpec((None, None, rows, D_MODEL), lambda l, k: (l, k, 0, 0)),
        out_shape=jax.ShapeDtypeStruct((DEPTH, 6, rows, D_MODEL), F32),
        compiler_params=_params("arbitrary", "arbitrary"),
        name="ada_modulation",
    )(c_all, ada_w, b4)


def _inproj_kernel(x_ref, g_ref, sc_ref, sh_ref, w_ref, u_ref):
    h = _rms(x_ref[...], g_ref[...]) * (1.0 + sc_ref[...]) + sh_ref[...]
    u_ref[...] = _mm(h, w_ref[...])


def _inproj_call(x, g, sc, sh, w, tm):
    bm, r, d = x.shape
    return pl.pallas_call(
        _inproj_kernel,
        grid=(bm, r // tm),
        in_specs=[_rows(tm, d), _const(g), _mod(sc, tm), _mod(sh, tm), _const(w)],
        out_specs=_rows(tm, w.shape[1]),
        out_shape=jax.ShapeDtypeStruct((bm, r, w.shape[1]), F32),
        compiler_params=_params("arbitrary", "arbitrary"),
        name="s5_in_proj",
    )(x, g, sc, sh, w)


def _ssm_prompt_kernel(x_ref, g_ref, sc_ref, sh_ref, win_ref, wbr_ref, wbi_ref, ar_ref, ai_ref, c_ref, d_ref,
                       wglu_ref, wout_ref, gn_ref, gate_ref,
                       y_ref, fr_ref, fi_ref, u_scr, br_scr, bi_scr, sr_scr, si_scr, *, t_chunk, nseq):
    tb_n = t_chunk // SUBLANES
    tiles = SLAB_STATES // LANES

    def tile_rows(j):
        return pl.ds(SCAN_TILE_PITCH * j, SUBLANES, stride=SCAN_STEP_PITCH)

    @pl.when(pl.program_id(0) == 0)
    def _():
        sr_scr[...] = jnp.zeros_like(sr_scr)
        si_scr[...] = jnp.zeros_like(si_scr)

    for q in range(nseq):
        h = _rms(x_ref[q], g_ref[...]) * (1.0 + sc_ref[q]) + sh_ref[q]
        u_scr[q] = _mm(h, win_ref[...])
    ub = u_scr[...].reshape(nseq * t_chunk, D_MODEL).astype(BF16)
    for s in range(SLABS):
        us = ub[:, s * LANES:(s + 1) * LANES]
        bre = jnp.dot(us, wbr_ref[s], preferred_element_type=F32)
        bim = jnp.dot(us, wbi_ref[s], preferred_element_type=F32)
        for jj in range(tiles):
            cols = slice(jj * LANES, (jj + 1) * LANES)
            for q in range(nseq):
                for tb in range(tb_n):
                    rows = slice((q * tb_n + tb) * SUBLANES, (q * tb_n + tb + 1) * SUBLANES)
                    br_scr[q, tb, tile_rows(s * tiles + jj), :] = bre[rows, cols]
                    bi_scr[q, tb, tile_rows(s * tiles + jj), :] = bim[rows, cols]

    ar = ar_ref[...]
    ai = ai_ref[...]

    def body(tb, carry):
        carry = list(carry)
        for r in range(SUBLANES):
            step = pl.ds(SCAN_STEP_PITCH * r, STATE_TILES, stride=SCAN_TILE_PITCH)
            for q in range(nseq):
                sr, si = carry[q]
                nsr = ar * sr - ai * si + br_scr[q, tb, step, :]
                nsi = ar * si + ai * sr + bi_scr[q, tb, step, :]
                carry[q] = (nsr, nsi)
                br_scr[q, tb, step, :] = nsr
                bi_scr[q, tb, step, :] = nsi
        return tuple(carry)

    final = lax.fori_loop(0, tb_n, body, tuple((sr_scr[q], si_scr[q]) for q in range(nseq)))
    for q in range(nseq):
        sr_scr[q], si_scr[q] = final[q]
        fr_ref[q], fi_ref[q] = final[q]

    def states(scr, j):
        return jnp.concatenate([scr[q, tb, tile_rows(j), :] for q in range(nseq) for tb in range(tb_n)],
                               axis=0).astype(BF16)

    u = u_scr[...].reshape(nseq * t_chunk, D_MODEL)
    z_slabs = []
    for s in range(SLABS):
        lhs = jnp.concatenate([states(scr, s * tiles + jj) for scr in (br_scr, bi_scr) for jj in range(tiles)],
                              axis=1)
        y = jnp.dot(lhs, c_ref[s], preferred_element_type=F32)
        sl = slice(s * LANES, (s + 1) * LANES)
        z_slabs.append(_gelu_tanh(y + d_ref[:, sl] * u[:, sl]))
    z = jnp.concatenate(z_slabs, axis=1)
    out = _mm(z * _sigmoid(_mm(z, wglu_ref[...])), wout_ref[...])
    for q in range(nseq):
        rows = slice(q * t_chunk, (q + 1) * t_chunk)
        y_ref[q] = x_ref[q] + gate_ref[q] * _rms(out[rows], gn_ref[...])


def _ssm_prompt_call(x, g, sc, sh, win, wbr, wbi, ar, ai, c, d, wglu, wout, gn, gate, t_chunk):
    b, l, dm = x.shape
    kern = functools.partial(_ssm_prompt_kernel, t_chunk=t_chunk, nseq=b)
    seq_spec = pl.BlockSpec((b, t_chunk, dm), lambda c: (0, c, 0))
    state_spec = pl.BlockSpec((b, STATE_TILES, LANES), lambda c: (0, 0, 0))
    scr = pltpu.VMEM((b, t_chunk // SUBLANES, SCAN_SLAB_ROWS, LANES), F32)
    carry = pltpu.VMEM((b, STATE_TILES, LANES), F32)
    consts = (g, sc, sh, win, wbr, wbi, ar, ai, c, d, wglu, wout, gn, gate)
    return pl.pallas_call(
        kern,
        grid=(l // t_chunk,),
        in_specs=[seq_spec] + [_const(a) for a in consts],
        out_specs=[seq_spec, state_spec, state_spec],
        out_shape=[jax.ShapeDtypeStruct((b, l, dm), F32),
                   jax.ShapeDtypeStruct((b, STATE_TILES, LANES), F32),
                   jax.ShapeDtypeStruct((b, STATE_TILES, LANES), F32)],
        scratch_shapes=[pltpu.VMEM((b, t_chunk, dm), F32), scr, scr, carry, carry],
        compiler_params=_params("arbitrary"),
        name="s5_mixer_prompt",
    )(x, *consts)


def _ssm_sample_kernel(u_ref, x0r_ref, x0i_ref, wbr_ref, wbi_ref, ar_ref, ai_ref, cr_ref, ci_ref,
                       d_ref, z_ref, fr_ref, fi_ref):
    for s in range(SLABS):
        sl = slice(s * LANES, (s + 1) * LANES)
        st = slice(s * SLAB_STATES, (s + 1) * SLAB_STATES)
        us = u_ref[:, sl]
        a_r, a_i = ar_ref[:, st], ai_ref[:, st]
        x0r, x0i = x0r_ref[:, st], x0i_ref[:, st]
        sr = _mm(us, wbr_ref[s]) + a_r * x0r - a_i * x0i
        si = _mm(us, wbi_ref[s]) + a_r * x0i + a_i * x0r
        fr_ref[:, st] = sr
        fi_ref[:, st] = si
        y = _mm(sr, cr_ref[s]) - _mm(si, ci_ref[s])
        z_ref[:, sl] = _gelu_tanh(y + d_ref[:, sl] * us)


def _ssm_sample_call(u, x0r, x0i, wbr, wbi, ar, ai, cr, ci, d):
    n = u.shape[0]
    args = (u, x0r, x0i, wbr, wbi, ar, ai, cr, ci, d)
    return pl.pallas_call(
        _ssm_sample_kernel,
        grid=(1,),
        in_specs=[_const(a) for a in args],
        out_specs=[pl.BlockSpec((n, D_MODEL), lambda i: (0, 0)),
                   pl.BlockSpec((n, SSM_FLAT), lambda i: (0, 0)),
                   pl.BlockSpec((n, SSM_FLAT), lambda i: (0, 0))],
        out_shape=[jax.ShapeDtypeStruct((n, D_MODEL), F32),
                   jax.ShapeDtypeStruct((n, SSM_FLAT), F32),
                   jax.ShapeDtypeStruct((n, SSM_FLAT), F32)],
        compiler_params=_params("arbitrary"),
        name="s5_step_sample",
    )(*args)


def _s5out_kernel(z_ref, x_ref, wg_ref, wo_ref, gn_ref, gate_ref, o_ref):
    z = z_ref[...]
    gl = z * _sigmoid(_mm(z, wg_ref[...]))
    out = _mm(gl, wo_ref[...])
    o_ref[...] = x_ref[...] + gate_ref[...] * _rms(out, gn_ref[...])


def _s5out_call(z, x, wg, wo, gn, gate, tm):
    bm, r, d = x.shape
    return pl.pallas_call(
        _s5out_kernel,
        grid=(bm, r // tm),
        in_specs=[_rows(tm, d), _rows(tm, d), _const(wg), _const(wo), _const(gn), _mod(gate, tm)],
        out_specs=_rows(tm, d),
        out_shape=jax.ShapeDtypeStruct((bm, r, d), F32),
        compiler_params=_params("arbitrary", "arbitrary"),
        name="s5_glu_out_proj",
    )(z, x, wg, wo, gn, gate)


def _ffn_kernel(x_ref, g_ref, sc_ref, sh_ref, wg_ref, wu_ref, wd_ref, gn_ref, gate_ref, o_ref,
                h_scr, acc_scr):
    f = pl.program_id(2)

    @pl.when(f == 0)
    def _():
        h = _rms(x_ref[...], g_ref[...]) * (1.0 + sc_ref[...]) + sh_ref[...]
        h_scr[...] = h.astype(h_scr.dtype)
        acc_scr[...] = jnp.zeros_like(acc_scr)

    h = h_scr[...]
    he = _silu(_mm(h, wg_ref[...])) * _mm(h, wu_ref[...])
    acc_scr[...] += _mm(he, wd_ref[...])

    @pl.when(f == pl.num_programs(2) - 1)
    def _():
        o_ref[...] = x_ref[...] + gate_ref[...] * _rms(acc_scr[...], gn_ref[...])


def _ffn_call(x, g, sc, sh, wg, wu, wd, gn, gate, tm, tf):
    bm, r, d = x.shape
    ff = wg.shape[1]
    return pl.pallas_call(
        _ffn_kernel,
        grid=(bm, r // tm, ff // tf),
        in_specs=[_rows(tm, d), _const(g), _mod(sc, tm), _mod(sh, tm),
                  pl.BlockSpec((d, tf), lambda b, i, f: (0, f)),
                  pl.BlockSpec((d, tf), lambda b, i, f: (0, f)),
                  pl.BlockSpec((tf, d), lambda b, i, f: (f, 0)),
                  _const(gn), _mod(gate, tm)],
        out_specs=_rows(tm, d),
        out_shape=jax.ShapeDtypeStruct((bm, r, d), F32),
        scratch_shapes=[pltpu.VMEM((tm, d), wg.dtype), pltpu.VMEM((tm, d), F32)],
        compiler_params=_params("arbitrary", "arbitrary", "arbitrary"),
        name="dense_swiglu",
    )(x, g, sc, sh, wg, wu, wd, gn, gate)


def _rope(t, cos, sin_lo, sin_hi):
    half = ROT_DIM // 2
    outs = []
    for s in range(t.shape[1] // LANES):
        ts = t[:, s * LANES:(s + 1) * LANES]
        outs.append(ts * cos + pltpu.roll(ts, LANES - half, 1) * sin_lo + pltpu.roll(ts, half, 1) * sin_hi)
    return jnp.concatenate(outs, axis=1)


def _qkv_values(x_ref, gkv_ref, g_ref, sc_ref, sh_ref, wkv_ref, wq_ref, cos_ref, slo_ref, shi_ref):
    x = x_ref[...]
    xn = x * lax.rsqrt(jnp.mean(x * x, axis=-1, keepdims=True) + RMS_EPS)
    kv = _mm(xn * gkv_ref[...], wkv_ref[...])
    q = _mm((xn * g_ref[...]) * (1.0 + sc_ref[...]) + sh_ref[...], wq_ref[...])
    cos, slo, shi = cos_ref[...], slo_ref[...], shi_ref[...]
    k = _rope(kv[:, :KV_DIM], cos, slo, shi)
    v = kv[:, KV_DIM:]
    q = _rope(q, cos, slo, shi) * (HEAD_DIM ** -0.5)
    return k, v, q


def _qkv_sample_kernel(*refs):
    k_ref, v_ref, q_ref = refs[10:]
    k, v, q = _qkv_values(*refs[:10])
    k_ref[...] = k
    v_ref[...] = v
    q_ref[...] = q


RESIDUE_STEP = 4


def _split_rows(val, scr, step):
    rows, slabs = val.shape[0], val.shape[1] // LANES
    for s in range(slabs):
        scr[s, 0:rows, :] = val[:, s * LANES:(s + 1) * LANES]
    n = rows // step
    return [jnp.concatenate([scr[s, pl.ds(r, n, stride=step), :] for s in range(slabs)], axis=1)
            for r in range(step)]


def _merge_rows(parts, scr):
    step, n, slabs = len(parts), parts[0].shape[0], parts[0].shape[1] // LANES
    for r in range(step):
        for s in range(slabs):
            scr[s, pl.ds(r, n, stride=step), :] = parts[r][:, s * LANES:(s + 1) * LANES]
    return jnp.concatenate([scr[s, 0:step * n, :] for s in range(slabs)], axis=1)


def _split_steps(dil):
    steps, done = [], 1
    while done < dil:
        steps.append(min(RESIDUE_STEP, dil // done))
        done *= steps[-1]
    return steps


def _residue_classes(val, scrs, dil):
    classes, done = [val], 1
    for step in _split_steps(dil):
        nxt = [None] * (done * step)
        for r, arr in enumerate(classes):
            for m, part in enumerate(_split_rows(arr, scrs[r % len(scrs)], step)):
                nxt[r + done * m] = part
        classes, done = nxt, done * step
    return classes


def _interleave_classes(classes, scrs):
    done = len(classes)
    for step in reversed(_split_steps(len(classes))):
        done //= step
        classes = [_merge_rows([classes[r + done * m] for m in range(step)], scrs[r % len(scrs)])
                   for r in range(done)]
    return classes[0]


def _store_residues(dst_ref, val, scrs, dil):
    for r, part in enumerate(_residue_classes(val, scrs, dil)):
        dst_ref[r] = part.astype(dst_ref.dtype)


def _dilation_classes(val, scrs):
    out, prev_dil, prev = {}, 1, [val]
    for _, dil in DILATION_GROUPS:
        nxt = [None] * dil
        for r, arr in enumerate(prev):
            for m, part in enumerate(_residue_classes(arr, scrs, dil // prev_dil)):
                nxt[r + prev_dil * m] = part
        out[dil], prev_dil, prev = nxt, dil, nxt
    return out


def _qkv_prompt_kernel(*refs):
    k_ref, v_ref = refs[10:12]
    kd_refs = refs[12:12 + N_DIL_GROUPS]
    vd_refs = refs[12 + N_DIL_GROUPS:12 + 2 * N_DIL_GROUPS]
    qd_refs = refs[12 + 2 * N_DIL_GROUPS:12 + 3 * N_DIL_GROUPS]
    scrs = refs[12 + 3 * N_DIL_GROUPS:]
    k, v, q = _qkv_values(*refs[:10])
    k_ref[...] = k
    v_ref[...] = v
    for val, dst_refs in ((k, kd_refs), (v, vd_refs)):
        classes = _dilation_classes(val, scrs)
        for g, (_, dil) in enumerate(DILATION_GROUPS):
            for r, part in enumerate(classes[dil]):
                dst_refs[g][r] = part.astype(dst_refs[g].dtype)
    for g, (_, dil) in enumerate(DILATION_GROUPS):
        _store_residues(qd_refs[g], q[:, g * KV_DIM:(g + 1) * KV_DIM], scrs, dil)


def _qkv_call(x, gkv, g, sc, sh, wkv, wq, cos, slo, shi, tm, prompt):
    bm, r, d = x.shape
    nq = wq.shape[1]
    tab = pl.BlockSpec((tm, LANES), lambda b, i: (i, 0))
    in_specs = [_rows(tm, d), _const(gkv), _const(g), _mod(sc, tm), _mod(sh, tm),
                _const(wkv), _const(wq), tab, tab, tab]
    out_specs = [_rows(tm, KV_DIM), _rows(tm, KV_DIM)]
    out_shape = [jax.ShapeDtypeStruct((bm, r, KV_DIM), F32), jax.ShapeDtypeStruct((bm, r, KV_DIM), F32)]
    if not prompt:
        return pl.pallas_call(
            _qkv_sample_kernel,
            grid=(bm, r // tm),
            in_specs=in_specs,
            out_specs=out_specs + [_rows(tm, nq)],
            out_shape=out_shape + [jax.ShapeDtypeStruct((bm, r, nq), F32)],
            compiler_params=_params("arbitrary", "arbitrary"),
            name="qkv_proj_rope_sample",
        )(x, gkv, g, sc, sh, wkv, wq, cos, slo, shi)
    for _ in range(3):
        for _, dil in DILATION_GROUPS:
            out_specs.append(pl.BlockSpec((None, dil, tm // dil, KV_DIM), lambda b, i: (b, 0, i, 0)))
            out_shape.append(jax.ShapeDtypeStruct((bm, dil, r // dil, KV_DIM), BF16))
    outs = pl.pallas_call(
        _qkv_prompt_kernel,
        grid=(bm, r // tm),
        in_specs=in_specs,
        out_specs=out_specs,
        out_shape=out_shape,
        scratch_shapes=[pltpu.VMEM((KV_DIM // LANES, tm, LANES), F32)] * 2,
        compiler_params=_params("arbitrary", "arbitrary"),
        name="qkv_proj_rope_prompt",
    )(x, gkv, g, sc, sh, wkv, wq, cos, slo, shi)
    n = N_DIL_GROUPS
    return outs[0], outs[1], outs[2:2 + n], outs[2 + n:2 + 2 * n], outs[2 + 2 * n:2 + 3 * n]


def _band_kernel(q_ref, kp_ref, kc_ref, vp_ref, vc_ref, o_ref, l_ref, *, nb):
    qi = lax.broadcasted_iota(jnp.int32, (SPAN, 2 * SPAN), 0)
    kj = lax.broadcasted_iota(jnp.int32, (SPAN, 2 * SPAN), 1)
    band = (kj >= qi) & (kj <= qi + SPAN)
    band_first = band & (kj >= jnp.where(pl.program_id(2) == 0, SPAN, 0))
    lane = lax.broadcasted_iota(jnp.int32, (SPAN, LANES), 1)
    head0 = lane < HEAD_DIM
    for i in range(nb):
        cur = slice(i * SPAN, (i + 1) * SPAN)
        prev = slice((i - 1) * SPAN, i * SPAN)
        for hp in range(KV_DIM // LANES):
            cols = slice(hp * LANES, (hp + 1) * LANES)
            q2 = q_ref[cur, cols]
            if i == 0:
                k2 = jnp.concatenate([kp_ref[:, cols], kc_ref[cur, cols]], axis=0)
                v2 = jnp.concatenate([vp_ref[:, cols], vc_ref[cur, cols]], axis=0)
                mask = band_first
            else:
                k2 = jnp.concatenate([kc_ref[prev, cols], kc_ref[cur, cols]], axis=0)
                v2 = jnp.concatenate([vc_ref[prev, cols], vc_ref[cur, cols]], axis=0)
                mask = band
            o_pair = None
            l_pair = None
            for hh in range(2):
                sel = head0 if hh == 0 else jnp.logical_not(head0)
                qm = jnp.where(sel, q2, jnp.zeros_like(q2))
                s = lax.dot_general(qm, k2, (((1,), (1,)), ((), ())), preferred_element_type=F32)
                s = jnp.where(mask, s, NEG_INF)
                m = jnp.max(s, axis=-1, keepdims=True)
                p = jnp.exp(s - m)
                den = jnp.sum(p, axis=-1, keepdims=True)
                o = jnp.dot(p.astype(BF16), v2, preferred_element_type=F32) / den
                lse = jnp.broadcast_to(m + jnp.log(den), (SPAN, LANES))
                o_pair = o if o_pair is None else jnp.where(head0, o_pair, o)
                l_pair = lse if l_pair is None else jnp.where(head0, l_pair, lse)
            o_ref[cur, cols] = o_pair.astype(o_ref.dtype)
            l_ref[cur, cols] = l_pair


def _band_call(q, k, v, nb):
    b, dil, m, _ = k.shape
    tq = nb * SPAN
    cur = pl.BlockSpec((None, None, tq, KV_DIM), lambda bb, r, n: (bb, r, n, 0))
    prev = pl.BlockSpec((None, None, SPAN, KV_DIM), lambda bb, r, n: (bb, r, jnp.maximum(n * nb - 1, 0), 0))
    return pl.pallas_call(
        functools.partial(_band_kernel, nb=nb),
        grid=(b, dil, m // tq),
        in_specs=[cur, prev, cur, prev, cur],
        out_specs=[cur, cur],
        out_shape=[jax.ShapeDtypeStruct((b, dil, m, KV_DIM), BF16),
                   jax.ShapeDtypeStruct((b, dil, m, KV_DIM), F32)],
        compiler_params=_params("arbitrary", "arbitrary", "arbitrary"),
        name=f"band_attention_dil{dil}",
    )(q, k, k, v, v)


def _transpose_small(x):
    c = x.shape[1]
    eye = jnp.where(lax.broadcasted_iota(jnp.int32, (c, c), 0) == lax.broadcasted_iota(jnp.int32, (c, c), 1),
                    1.0, 0.0).astype(F32)
    return lax.dot_general(eye, x, (((1,), (1,)), ((), ())), preferred_element_type=F32, precision=HIGHEST)


def _sample_attn_kernel(q_ref, kn_ref, vn_ref, kc_ref, vc_ref, o_ref, ko_ref, vo_ref):
    cl = kc_ref.shape[-1]
    is_last = lax.broadcasted_iota(jnp.int32, (HEAD_DIM, cl), 1) == cl - 1
    kn_rows, vn_rows = kn_ref[0], vn_ref[0]
    kn_cols, vn_cols = _transpose_small(kn_rows), _transpose_small(vn_rows)
    for h in range(N_KV_HEADS):
        ko_ref[0, h] = jnp.where(is_last, kn_cols[:, h:h + 1], pltpu.roll(kc_ref[0, h], cl - 1, 1))
        vo_ref[0, h] = jnp.where(is_last, vn_cols[:, h:h + 1], pltpu.roll(vc_ref[0, h], cl - 1, 1))
    outs, lses = [], []
    for g, (win, dil) in enumerate(DILATION_GROUPS):
        q_rows = q_ref[0, g]
        q_cols = _transpose_small(q_rows)
        s = jnp.concatenate(
            [jnp.sum(kc_ref[0, h, :, cl - win:] * q_cols[:, h:h + 1], axis=0, keepdims=True)
             for h in range(N_KV_HEADS)], axis=0)
        if dil > 1:
            lane = lax.broadcasted_iota(jnp.int32, s.shape, 1)
            s = jnp.where((lane & (dil - 1)) == 0, s, NEG_INF)
        s_new = jnp.sum(kn_rows * q_rows, axis=-1, keepdims=True)
        m = jnp.maximum(jnp.max(s, axis=-1, keepdims=True), s_new)
        p = jnp.exp(s - m)
        p_new = jnp.exp(s_new - m)
        den = jnp.sum(p, axis=-1, keepdims=True) + p_new
        pn = p / den
        o_cols = jnp.concatenate(
            [jnp.sum(vc_ref[0, h, :, cl - win:] * pn[h:h + 1, :], axis=-1, keepdims=True)
             for h in range(N_KV_HEADS)], axis=1)
        outs.append(_transpose_small(o_cols) + (p_new / den) * vn_rows)
        lses.append(m + jnp.log(den))
    mx = jnp.maximum(jnp.maximum(lses[0], lses[1]), lses[2])
    es = [jnp.exp(l - mx) for l in lses]
    o_ref[0] = (es[0] * outs[0] + es[1] * outs[1] + es[2] * outs[2]) / (es[0] + es[1] + es[2])


def _sample_attn_call(q, k_new, v_new, cache_k, cache_v):
    n, nh, hd, cl = cache_k.shape
    col = pl.BlockSpec((1, nh, hd), lambda i: (i, 0, 0))
    cache = pl.BlockSpec((1, nh, hd, cl), lambda i: (i, 0, 0, 0))
    return pl.pallas_call(
        _sample_attn_kernel,
        grid=(n,),
        in_specs=[pl.BlockSpec((1, N_DIL_GROUPS, nh, hd), lambda i: (i, 0, 0, 0)), col, col,
                  cache, cache],
        out_specs=[col, cache, cache],
        out_shape=[jax.ShapeDtypeStruct((n, nh, hd), F32),
                   jax.ShapeDtypeStruct(cache_k.shape, cache_k.dtype),
                   jax.ShapeDtypeStruct(cache_v.shape, cache_v.dtype)],
        compiler_params=_params("arbitrary"),
        name="window_attention_cache_append",
    )(q, k_new, v_new, cache_k, cache_v)


def _attn_out_kernel(o1_ref, o2_ref, o3_ref, l1_ref, l2_ref, l3_ref, x_ref, wo_ref, gn_ref, gate_ref,
                     y_ref, *scrs):
    vals = []
    for i, (ref, (_, dil)) in enumerate(zip((o1_ref, o2_ref, o3_ref, l1_ref, l2_ref, l3_ref),
                                            DILATION_GROUPS + DILATION_GROUPS)):
        classes = [ref[r].astype(F32) for r in range(dil)]
        vals.append(_interleave_classes(classes, scrs[2 * (i % 2):2 * (i % 2) + 2]))
    o1, o2, o3, l1, l2, l3 = vals
    mx = jnp.maximum(jnp.maximum(l1, l2), l3)
    e1, e2, e3 = jnp.exp(l1 - mx), jnp.exp(l2 - mx), jnp.exp(l3 - mx)
    o = (e1 * o1 + e2 * o2 + e3 * o3) / (e1 + e2 + e3)
    out = _mm(o, wo_ref[...])
    y_ref[...] = x_ref[...] + gate_ref[...] * _rms(out, gn_ref[...])


def _attn_out_call(os_, ls_, x, wo, gn, gate, tm):
    bm, r, d = x.shape
    res = [pl.BlockSpec((None, dil, tm // dil, KV_DIM), lambda b, i: (b, 0, i, 0))
           for _, dil in DILATION_GROUPS]
    return pl.pallas_call(
        _attn_out_kernel,
        grid=(bm, r // tm),
        in_specs=res + res + [_rows(tm, d), _const(wo), _const(gn), _mod(gate, tm)],
        out_specs=_rows(tm, d),
        out_shape=jax.ShapeDtypeStruct((bm, r, d), F32),
        scratch_shapes=[pltpu.VMEM((KV_DIM // LANES, tm, LANES), F32)] * 4,
        compiler_params=_params("arbitrary", "arbitrary"),
        name="attn_merge_out_proj",
    )(*os_, *ls_, x, wo, gn, gate)


def _proj_res_kernel(o_ref, x_ref, wo_ref, gn_ref, gate_ref, y_ref):
    out = _mm(o_ref[...], wo_ref[...])
    y_ref[...] = x_ref[...] + gate_ref[...] * _rms(out, gn_ref[...])


def _proj_res_call(o, x, wo, gn, gate, tm):
    bm, r, d = x.shape
    return pl.pallas_call(
        _proj_res_kernel,
        grid=(bm, r // tm),
        in_specs=[_rows(tm, o.shape[2]), _rows(tm, d), _const(wo), _const(gn), _mod(gate, tm)],
        out_specs=_rows(tm, d),
        out_shape=jax.ShapeDtypeStruct((bm, r, d), F32),
        compiler_params=_params("arbitrary", "arbitrary"),
        name="attn_out_proj",
    )(o, x, wo, gn, gate)


def _top2(logits):
    lane = lax.broadcasted_iota(jnp.int32, logits.shape, 1).astype(F32)
    m1 = jnp.max(logits, axis=-1, keepdims=True)
    i1 = jnp.min(jnp.where(logits == m1, lane, float(LANES)), axis=-1, keepdims=True)
    rest = jnp.where(lane == i1, -jnp.inf, logits)
    m2 = jnp.max(rest, axis=-1, keepdims=True)
    i2 = jnp.min(jnp.where(rest == m2, lane, float(LANES)), axis=-1, keepdims=True)
    e = jnp.exp(m2 - m1)
    return lane, i1, i2, 1.0 / (1.0 + e), e / (1.0 + e)


def _route_kernel(x_ref, g_ref, sc_ref, sh_ref, wr_ref, br_ref, h_ref, route_ref, cnt_ref, carry_scr,
                  *, stride):
    @pl.when((pl.program_id(0) == 0) & (pl.program_id(1) == 0))
    def _():
        carry_scr[...] = jnp.zeros_like(carry_scr)

    h = _rms(x_ref[...], g_ref[...]) * (1.0 + sc_ref[...]) + sh_ref[...]
    tm = h.shape[0]
    for c in range(SLABS):
        h_ref[pl.ds(c, tm, stride=SUBLANES), :] = h[:, c * LANES:(c + 1) * LANES]
    logits = _mm(h, wr_ref[...]) + br_ref[...]
    lane, i1, i2, g1, g2 = _top2(logits)
    onehot = jnp.where(lane == i1, 1.0, 0.0) + jnp.where(lane == i2, 1.0, 0.0)
    row = lax.broadcasted_iota(jnp.int32, (tm, tm), 0)
    col = lax.broadcasted_iota(jnp.int32, (tm, tm), 1)
    earlier = jnp.where(col < row, 1.0, 0.0).astype(BF16)
    before = jnp.dot(earlier, onehot.astype(BF16), preferred_element_type=F32) + carry_scr[0:1, :]
    r1 = jnp.sum(jnp.where(lane == i1, before, 0.0), axis=-1, keepdims=True)
    r2 = jnp.sum(jnp.where(lane == i2, before, 0.0), axis=-1, keepdims=True)
    p1 = i1 * float(stride) + r1
    p2 = i2 * float(stride) + r2
    route_ref[...] = (jnp.where(lane == 0.0, p1, 0.0) + jnp.where(lane == 1.0, p2, 0.0)
                      + jnp.where(lane == 2.0, g1, 0.0) + jnp.where(lane == 3.0, g2, 0.0))
    carry = carry_scr[...] + jnp.sum(onehot, axis=0, keepdims=True)
    carry_scr[...] = carry
    cnt_ref[...] = carry


def _route_call(x, g, sc, sh, wr, br, tm, stride):
    bm, r, d = x.shape
    nt = r // tm
    return pl.pallas_call(
        functools.partial(_route_kernel, stride=stride),
        grid=(bm, nt),
        in_specs=[_rows(tm, d), _const(g), _mod(sc, tm), _mod(sh, tm), _const(wr), _const(br)],
        out_specs=[pl.BlockSpec((tm * SUBLANES, LANES), lambda b, i: (b * nt + i, 0)),
                   _rows(tm, LANES),
                   pl.BlockSpec((SUBLANES, LANES), lambda b, i: (0, 0))],
        out_shape=[jax.ShapeDtypeStruct((bm * r * SUBLANES, LANES), F32),
                   jax.ShapeDtypeStruct((bm, r, LANES), F32),
                   jax.ShapeDtypeStruct((SUBLANES, LANES), F32)],
        scratch_shapes=[pltpu.VMEM((SUBLANES, LANES), F32)],
        compiler_params=_params("arbitrary", "arbitrary"),
        name="moe_route",
    )(x, g, sc, sh, wr, br)


def _token_tile(ref, idx):
    return ref.at[pl.ds(pl.multiple_of(idx * SUBLANES, SUBLANES), SUBLANES), :]


def _dispatch_kernel(cnt_ref, pos_ref, h_ref, xs_hbm, zero_scr, sem, zsem, *, tm, stride, tmx):
    i = pl.program_id(0)

    def issue(t, carry):
        src = _token_tile(h_ref, t)
        for k in range(2):
            pltpu.make_async_copy(src, _token_tile(xs_hbm, pos_ref[0, k * tm + t]), sem).start(priority=k)
        return carry

    lax.fori_loop(0, tm, issue, 0)
    for _ in range(2):
        pltpu.make_async_copy(h_ref, xs_hbm.at[pl.ds(0, tm * SUBLANES), :], sem).wait()

    @pl.when(i == pl.num_programs(0) - 1)
    def _():
        zero_scr[...] = jnp.zeros_like(zero_scr)
        copies = []
        for e in range(N_EXPERTS):
            start = pl.multiple_of((e * stride + cnt_ref[e]) * SUBLANES, SUBLANES)
            copies.append(pltpu.make_async_copy(zero_scr, xs_hbm.at[pl.ds(start, tmx * SUBLANES), :], zsem))
            copies[-1].start()
        for cp in copies:
            cp.wait()


def _dispatch_call(cnt, pos, h2d, tm, stride, tmx):
    n_tiles = pos.shape[0]
    rows = N_EXPERTS * stride * SUBLANES
    return pl.pallas_call(
        functools.partial(_dispatch_kernel, tm=tm, stride=stride, tmx=tmx),
        grid_spec=pltpu.PrefetchScalarGridSpec(
            num_scalar_prefetch=1,
            grid=(n_tiles,),
            in_specs=[pl.BlockSpec((None, 1, 2 * tm), lambda i, cnt: (i, 0, 0), memory_space=pltpu.SMEM),
                      pl.BlockSpec((tm * SUBLANES, LANES), lambda i, cnt: (i, 0))],
            out_specs=pl.BlockSpec(memory_space=pl.ANY),
            scratch_shapes=[pltpu.VMEM((tmx * SUBLANES, LANES), F32),
                            pltpu.SemaphoreType.DMA(()), pltpu.SemaphoreType.DMA(())],
        ),
        out_shape=jax.ShapeDtypeStruct((rows, LANES), F32),
        compiler_params=_params("arbitrary"),
        name="moe_dispatch",
    )(cnt, pos, h2d)


def _expert_kernel(blk_ref, exp_ref, nused_ref, xs_ref, wg_ref, wu_ref, wd_ref, o_ref, x_scr, acc_scr,
                   *, tmx):
    j = pl.program_id(0)
    f = pl.program_id(1)

    @pl.when(j < nused_ref[0])
    def _():
        @pl.when(f == 0)
        def _():
            for c in range(SLABS):
                x_scr[:, c * LANES:(c + 1) * LANES] = (
                    xs_ref[pl.ds(c, tmx, stride=SUBLANES), :].astype(x_scr.dtype))
            acc_scr[...] = jnp.zeros_like(acc_scr)

        x = x_scr[...]
        he = _silu(_mm(x, wg_ref[...])) * _mm(x, wu_ref[...])
        acc_scr[...] += _mm(he, wd_ref[...])

        @pl.when(f == pl.num_programs(1) - 1)
        def _():
            for c in range(SLABS):
                o_ref[pl.ds(c, tmx, stride=SUBLANES), :] = acc_scr[:, c * LANES:(c + 1) * LANES]


def _expert_call(blk, exp, nused, xs, wg, wu, wd, tmx, tf):
    ne, d, ff = wg.shape
    nf = ff // tf

    def fidx(j, f, nused):
        return jnp.where(j < nused[0], f, nf - 1)

    row_spec = pl.BlockSpec((tmx * SUBLANES, LANES), lambda j, f, blk, exp, nu: (blk[j], 0))
    return pl.pallas_call(
        functools.partial(_expert_kernel, tmx=tmx),
        grid_spec=pltpu.PrefetchScalarGridSpec(
            num_scalar_prefetch=3,
            grid=(blk.shape[0], nf),
            in_specs=[row_spec,
                      pl.BlockSpec((None, d, tf), lambda j, f, blk, exp, nu: (exp[j], 0, fidx(j, f, nu))),
                      pl.BlockSpec((None, d, tf), lambda j, f, blk, exp, nu: (exp[j], 0, fidx(j, f, nu))),
                      pl.BlockSpec((None, tf, d), lambda j, f, blk, exp, nu: (exp[j], fidx(j, f, nu), 0))],
            out_specs=row_spec,
            scratch_shapes=[pltpu.VMEM((tmx, d), wg.dtype), pltpu.VMEM((tmx, d), F32)],
        ),
        out_shape=jax.ShapeDtypeStruct(xs.shape, F32),
        compiler_params=_params("arbitrary", "arbitrary"),
        name="moe_grouped_experts",
    )(blk, exp, nused, xs, wg, wu, wd)


COMBINE_PARTS = 4


def _combine_kernel(pos_ref, route_ref, x_ref, gn_ref, gate_ref, out_hbm, y_ref, buf, sems, *, tm):
    pt = tm // COMBINE_PARTS

    def issue(part):
        def body(t, carry):
            for k in range(2):
                pltpu.make_async_copy(_token_tile(out_hbm, pos_ref[0, k * tm + t]),
                                      _token_tile(buf, k * tm + t), sems.at[part]).start(priority=k)
            return carry
        lax.fori_loop(part * pt, (part + 1) * pt, body, 0)

    for part in range(COMBINE_PARTS):
        issue(part)
    for part in range(COMBINE_PARTS):
        for _ in range(2):
            pltpu.make_async_copy(out_hbm.at[pl.ds(0, pt * SUBLANES), :], buf.at[pl.ds(0, pt * SUBLANES), :],
                                  sems.at[part]).wait()
        rows = slice(part * pt, (part + 1) * pt)
        route = route_ref[rows, :]
        g1, g2 = route[:, 2:3], route[:, 3:4]
        slabs = []
        for c in range(SLABS):
            slabs.append(g1 * buf[pl.ds(part * pt * SUBLANES + c, pt, stride=SUBLANES), :]
                         + g2 * buf[pl.ds((tm + part * pt) * SUBLANES + c, pt, stride=SUBLANES), :])
        y = jnp.concatenate(slabs, axis=1)
        gate = gate_ref[...] if gate_ref.shape[0] == 1 else gate_ref[rows, :]
        y_ref[rows, :] = x_ref[rows, :] + gate * _rms(y, gn_ref[...])


def _combine_call(pos, route, x, gn, gate, out2d, tm):
    bm, r, d = x.shape
    nt = r // tm
    return pl.pallas_call(
        functools.partial(_combine_kernel, tm=tm),
        grid=(bm, nt),
        in_specs=[pl.BlockSpec((None, 1, 2 * tm), lambda b, i: (b * nt + i, 0, 0), memory_space=pltpu.SMEM),
                  _rows(tm, LANES), _rows(tm, d), _const(gn), _mod(gate, tm),
                  pl.BlockSpec(memory_space=pl.ANY)],
        out_specs=_rows(tm, d),
        out_shape=jax.ShapeDtypeStruct((bm, r, d), F32),
        scratch_shapes=[pltpu.VMEM((2 * tm * SUBLANES, LANES), F32),
                        pltpu.SemaphoreType.DMA((COMBINE_PARTS,))],
        compiler_params=_params("arbitrary", "arbitrary"),
        name="moe_combine",
    )(pos, route, x, gn, gate, out2d)


def _moe_sparse(x, g, sc, sh, wr, br, wg, wu, wd, gn, gate, tm):
    bm, r, d = x.shape
    m = bm * r
    tmx = MOE_TILE_ROWS
    stride = m + tmx
    n_tiles = 2 * m // tmx + N_EXPERTS
    h2d, route, cnt = _route_call(x, g, sc, sh, wr, br, tm, stride)
    cnt = cnt[0, :N_EXPERTS].astype(jnp.int32)
    pos = route[..., :2].astype(jnp.int32).reshape(m // tm, tm, 2).transpose(0, 2, 1).reshape(m // tm, 1, 2 * tm)
    xs = _dispatch_call(cnt, pos, h2d, tm, stride, tmx)
    per = (cnt + tmx - 1) // tmx
    ends = jnp.cumsum(per)
    j = jnp.minimum(jnp.arange(n_tiles, dtype=jnp.int32), ends[-1] - 1)
    exp = jnp.sum((j[:, None] >= ends[None, :]).astype(jnp.int32), axis=1)
    blk = exp * (stride // tmx) + j - (ends - per)[exp]
    out2d = _expert_call(blk, exp, ends[-1:], xs, wg, wu, wd, tmx, wg.shape[2] // 2)
    return _combine_call(pos, route, x, gn, gate, out2d, tm)


def _router_kernel(x_ref, g_ref, sc_ref, sh_ref, wr_ref, br_ref, h_ref, comb_ref):
    h = _rms(x_ref[...], g_ref[...]) * (1.0 + sc_ref[...]) + sh_ref[...]
    h_ref[...] = h.astype(h_ref.dtype)
    logits = _mm(h, wr_ref[...]) + br_ref[...]
    lane, i1, i2, g1, g2 = _top2(logits)
    comb_ref[...] = jnp.where(lane == i1, g1, 0.0) + jnp.where(lane == i2, g2, 0.0)


def _router_call(x, g, sc, sh, wr, br, h_dtype, tm):
    bm, r, d = x.shape
    return pl.pallas_call(
        _router_kernel,
        grid=(bm, r // tm),
        in_specs=[_rows(tm, d), _const(g), _mod(sc, tm), _mod(sh, tm), _const(wr), _const(br)],
        out_specs=[_rows(tm, d), _rows(tm, LANES)],
        out_shape=[jax.ShapeDtypeStruct((bm, r, d), h_dtype),
                   jax.ShapeDtypeStruct((bm, r, LANES), F32)],
        compiler_params=_params("arbitrary", "arbitrary"),
        name="moe_router",
    )(x, g, sc, sh, wr, br)


def _moe_kernel(h_ref, comb_ref, x_ref, wg_ref, wu_ref, wd_ref, gn_ref, gate_ref,
                o_ref, wgb_ref, wub_ref, wdb_ref, acc_scr):
    e = pl.program_id(2)
    f = pl.program_id(3)

    @pl.when((e == 0) & (f == 0))
    def _():
        acc_scr[...] = jnp.zeros_like(acc_scr)

    comb = comb_ref[...]
    lane = lax.broadcasted_iota(jnp.int32, comb.shape, 1)
    cw = jnp.sum(jnp.where(lane == e, comb, 0.0), axis=-1, keepdims=True)
    wg, wu, wd = wg_ref[...].astype(BF16), wu_ref[...].astype(BF16), wd_ref[...].astype(BF16)
    wgb_ref[...] = wg
    wub_ref[...] = wu
    wdb_ref[...] = wd
    h = h_ref[...]
    he = _silu(_mm(h, wg)) * _mm(h, wu)
    acc_scr[...] += cw * _mm(he, wd)

    @pl.when((e == pl.num_programs(2) - 1) & (f == pl.num_programs(3) - 1))
    def _():
        o_ref[...] = x_ref[...] + gate_ref[...] * _rms(acc_scr[...], gn_ref[...])


def _moe_call(h, comb, x, wg, wu, wd, gn, gate, tm, tf):
    bm, r, d = x.shape
    ne, _, ff = wg.shape
    assert bm == 1 and r == tm
    up = pl.BlockSpec((None, d, tf), lambda b, i, e, f: (e, 0, f))
    down = pl.BlockSpec((None, tf, d), lambda b, i, e, f: (e, f, 0))
    return pl.pallas_call(
        _moe_kernel,
        grid=(bm, r // tm, ne, ff // tf),
        in_specs=[_rows(tm, d), _rows(tm, LANES), _rows(tm, d), up, up, down, _const(gn), _mod(gate, tm)],
        out_specs=[_rows(tm, d), up, up, down],
        out_shape=[jax.ShapeDtypeStruct((bm, r, d), F32), jax.ShapeDtypeStruct(wg.shape, BF16),
                   jax.ShapeDtypeStruct(wu.shape, BF16), jax.ShapeDtypeStruct(wd.shape, BF16)],
        scratch_shapes=[pltpu.VMEM((tm, d), F32)],
        compiler_params=_params("arbitrary", "arbitrary", "arbitrary", "arbitrary"),
        name="moe_experts",
    )(h, comb, x, wg, wu, wd, gn, gate)


def _ssm_params(lam_re, lam_im, log_dt, b_re, b_im, c_re, c_im, wdtype):
    dt = jnp.exp(log_dt)[:, None]
    mag = jnp.exp(lam_re * dt)
    a_re, a_im = mag * jnp.cos(lam_im * dt), mag * jnp.sin(lam_im * dt)
    den = lam_re * lam_re + lam_im * lam_im
    f_re = ((a_re - 1.0) * lam_re + a_im * lam_im) / den
    f_im = (a_im * lam_re - (a_re - 1.0) * lam_im) / den
    bb_re = f_re[..., None] * b_re - f_im[..., None] * b_im
    bb_im = f_re[..., None] * b_im + f_im[..., None] * b_re
    eye = jnp.eye(GROUPS_PER_SLAB, dtype=F32)

    def in_map(bb):
        t = bb.reshape(SLABS, GROUPS_PER_SLAB, SSM_STATE, SSM_CH)
        w = jnp.einsum('sgpc,gh->sgchp', t, eye)
        return w.reshape(SLABS, LANES, SLAB_STATES).astype(wdtype)

    def out_map(cc):
        t = cc.reshape(SLABS, GROUPS_PER_SLAB, SSM_CH, SSM_STATE)
        w = jnp.einsum('sgcp,gh->sgphc', t, eye)
        return w.reshape(SLABS, SLAB_STATES, LANES).astype(wdtype)

    return a_re, a_im, in_map(bb_re), in_map(bb_im), out_map(c_re), out_map(c_im)


def _rope_tables(pos):
    half = ROT_DIM // 2
    inv = ROPE_THETA ** (-jnp.arange(half, dtype=F32) / half)
    ang = pos.astype(F32)[:, None] * inv[None, :]
    cos, sin = jnp.cos(ang), jnp.sin(ang)
    n = pos.shape[0]
    pad = jnp.zeros((n, HEAD_DIM - ROT_DIM), F32)
    zero = jnp.zeros((n, half), F32)
    cos_h = jnp.concatenate([cos, cos, pad + 1.0], axis=1)
    lo_h = jnp.concatenate([-sin, zero, pad], axis=1)
    hi_h = jnp.concatenate([zero, sin, pad], axis=1)
    rep = LANES // HEAD_DIM
    return jnp.tile(cos_h, (1, rep)), jnp.tile(lo_h, (1, rep)), jnp.tile(hi_h, (1, rep))


def _trunk(x, mods, pos, wts, prompt, ssm0=None, cache=None):
    bm, r, d = x.shape
    tm = 512 if prompt else r
    wd = BF16 if prompt else F32
    cast = lambda w: w.astype(wd)
    norm_g = wts['norm_g'].reshape(DEPTH, 4, 1, d)

    sh1, sc1, g1, sh2, sc2, g2 = mods[0]
    a_re, a_im = wts['ssm_prepared'][:2]
    wbr, wbi, cr, ci = (cast(w) for w in wts['ssm_prepared'][2:])
    dvec = wts['ssm_d'][0].reshape(1, d)
    w_in, w_glu, w_out = cast(wts['ssm_w_in'][0]), cast(wts['ssm_w_glu'][0]), cast(wts['ssm_w_out'][0])
    if prompt:
        x, fr, fi = _ssm_prompt_call(x, norm_g[0, 0], sc1, sh1, w_in, wbr, wbi,
                                     a_re.reshape(STATE_TILES, LANES), a_im.reshape(STATE_TILES, LANES),
                                     jnp.concatenate([cr, -ci], axis=1), dvec, w_glu, w_out,
                                     norm_g[0, 1], g1, SCAN_CHUNK)
    else:
        u = _inproj_call(x, norm_g[0, 0], sc1, sh1, w_in, tm)
        z, fr, fi = _ssm_sample_call(u[0], ssm0[0].reshape(r, SSM_FLAT), ssm0[1].reshape(r, SSM_FLAT),
                                     wbr, wbi, a_re.reshape(1, SSM_FLAT), a_im.reshape(1, SSM_FLAT),
                                     cr, ci, dvec)
        x = _s5out_call(z[None], x, w_glu, w_out, norm_g[0, 1], g1, tm)
    fr = fr.reshape(-1, SSM_GROUPS, SSM_STATE)[None]
    fi = fi.reshape(-1, SSM_GROUPS, SSM_STATE)[None]
    tf = wts['ffn_w_gate'].shape[2] // 2
    x = _ffn_call(x, norm_g[0, 2], sc2, sh2, cast(wts['ffn_w_gate'][0]), cast(wts['ffn_w_up'][0]),
                  cast(wts['ffn_w_down'][0]), norm_g[0, 3], g2, tm, tf)

    sh1, sc1, g1, sh2, sc2, g2 = mods[1]
    cos, slo, shi = _rope_tables(pos)
    qkv = _qkv_call(x, wts['kv_norm_g'].reshape(1, d), norm_g[1, 0], sc1, sh1, cast(wts['w_kv']),
                    cast(wts['attn_w_q'][0]), cos, slo, shi, tm, prompt)
    wo = cast(wts['attn_w_o'][0])
    wr = cast(jnp.zeros((d, LANES), F32).at[:, :N_EXPERTS].set(wts['moe_w_router'][0]))
    br = jnp.full((1, LANES), NEG_INF, F32).at[0, :N_EXPERTS].set(wts['moe_b_router'][0])
    experts = None
    if prompt:
        wg, wu, wdn = wts['experts_bf16']
        k, v, kd, vd, qd = qkv
        os_, ls_ = [], []
        for g, (_, dil) in enumerate(DILATION_GROUPS):
            o, lse = _band_call(qd[g], kd[g], vd[g], min(4, r // dil // SPAN))
            os_.append(o)
            ls_.append(lse)
        x = _attn_out_call(os_, ls_, x, wo, norm_g[1, 1], g1, tm)
        x = _moe_sparse(x, norm_g[1, 2], sc2, sh2, wr, br, wg, wu, wdn, norm_g[1, 3], g2, tm)
    else:
        k, v, q = qkv
        o, k, v = _sample_attn_call(q[0].reshape(r, N_DIL_GROUPS, N_KV_HEADS, HEAD_DIM),
                                    k[0].reshape(r, N_KV_HEADS, HEAD_DIM),
                                    v[0].reshape(r, N_KV_HEADS, HEAD_DIM),
                                    cache[0].transpose(0, 2, 3, 1), cache[1].transpose(0, 2, 3, 1))
        k, v = k.transpose(0, 3, 1, 2), v.transpose(0, 3, 1, 2)
        x = _proj_res_call(o.reshape(1, r, KV_DIM), x, wo, norm_g[1, 1], g1, tm)
        h, comb = _router_call(x, norm_g[1, 2], sc2, sh2, wr, br, wd, tm)
        wg, wu, wdn = wts['moe_w_gate'][0], wts['moe_w_up'][0], wts['moe_w_down'][0]
        x, *experts = _moe_call(h, comb, x, wg, wu, wdn, norm_g[1, 3], g2, tm, wg.shape[2] // 4)
    return x, fr, fi, k, v, experts


def kernel(x_prompt, x_sample, state_ssm_re, state_ssm_im, cache_k, cache_v, c_prompt, c_sample, ada_w, ada_b, norm_g, ssm_w_in, ssm_lam_re, ssm_lam_im, ssm_log_dt, ssm_b_re, ssm_b_im, ssm_c_re, ssm_c_im, ssm_d, ssm_w_glu, ssm_w_out, kv_norm_g, w_kv, attn_w_q, attn_w_o, ffn_w_gate, ffn_w_up, ffn_w_down, moe_w_router, moe_b_router, moe_w_gate, moe_w_up, moe_w_down):
    wts = dict(norm_g=norm_g, ssm_w_in=ssm_w_in, ssm_lam_re=ssm_lam_re, ssm_lam_im=ssm_lam_im,
               ssm_log_dt=ssm_log_dt, ssm_b_re=ssm_b_re, ssm_b_im=ssm_b_im, ssm_c_re=ssm_c_re,
               ssm_c_im=ssm_c_im, ssm_d=ssm_d, ssm_w_glu=ssm_w_glu, ssm_w_out=ssm_w_out,
               kv_norm_g=kv_norm_g, w_kv=w_kv, attn_w_q=attn_w_q, attn_w_o=attn_w_o,
               ffn_w_gate=ffn_w_gate, ffn_w_up=ffn_w_up, ffn_w_down=ffn_w_down,
               moe_w_router=moe_w_router, moe_b_router=moe_b_router, moe_w_gate=moe_w_gate,
               moe_w_up=moe_w_up, moe_w_down=moe_w_down)
    nb, seq, d = x_prompt.shape
    ns = x_sample.shape[0]
    assert x_sample.shape[1] == 1 and ns % SUBLANES == 0

    pad = (-(ns + nb)) % SUBLANES
    c_all = jnp.concatenate([c_sample, c_prompt, jnp.zeros((pad, d), F32)], axis=0)
    mod_all = _ada_call(c_all, ada_w, ada_b)
    mods_s = [[mod_all[l, k, :ns][None] for k in range(6)] for l in range(DEPTH)]
    mods_p = [[mod_all[l, k, ns:ns + nb][:, None, :] for k in range(6)] for l in range(DEPTH)]

    pos_p = jnp.arange(seq, dtype=jnp.int32)
    pos_s = jnp.full((ns,), PAST_LEN, dtype=jnp.int32)
    wts['ssm_prepared'] = _ssm_params(ssm_lam_re[0], ssm_lam_im[0], ssm_log_dt[0], ssm_b_re[0], ssm_b_im[0],
                                      ssm_c_re[0], ssm_c_im[0], F32)
    xs = x_sample.reshape(1, ns, d)
    y_s, sr_s, si_s, k_s, v_s, wts['experts_bf16'] = _trunk(
        xs, mods_s, pos_s, wts, False, ssm0=(state_ssm_re[0], state_ssm_im[0]), cache=(cache_k, cache_v))
    y_p, sr_p, si_p, k_p, v_p, _ = _trunk(x_prompt, mods_p, pos_p, wts, True)

    keep = min(MAX_WINDOW, seq)
    k_rows_p = k_p[:, seq - keep:].reshape(nb, keep, N_KV_HEADS, HEAD_DIM)
    v_rows_p = v_p[:, seq - keep:].reshape(nb, keep, N_KV_HEADS, HEAD_DIM)
    return (y_p, y_s.reshape(ns, 1, d), sr_p, si_p, sr_s, si_s, k_rows_p, v_rows_p, k_s, v_s)
```

```python
import functools
import math

import jax
import jax.numpy as jnp
from jax import lax
from jax.experimental import pallas as pl
from jax.experimental.pallas import tpu as pltpu

F32 = jnp.float32
BF16 = jnp.bfloat16
HIGHEST = lax.Precision.HIGHEST

D_MODEL = 1024
DEPTH = 2
PAST_LEN = 16384
SSM_CH = 16
SSM_GROUPS = D_MODEL // SSM_CH
SSM_STATE = 64
SSM_FLAT = SSM_GROUPS * SSM_STATE
HEAD_DIM = 64
N_KV_HEADS = D_MODEL // 128
KV_DIM = N_KV_HEADS * HEAD_DIM
DILATION_GROUPS = ((128, 1), (512, 4), (2048, 16))
N_DIL_GROUPS = len(DILATION_GROUPS)
SPAN = 128
MAX_WINDOW = 2048
ROT_DIM = HEAD_DIM // 4
ROPE_THETA = 500000.0
N_EXPERTS = 8
RMS_EPS = 1e-6
NEG_INF = -1e30

LANES = 128
SUBLANES = 8
SLABS = D_MODEL // LANES
GROUPS_PER_SLAB = LANES // SSM_CH
SLAB_STATES = GROUPS_PER_SLAB * SSM_STATE
STATE_TILES = SSM_FLAT // LANES
SCAN_STEP_PITCH = 2
SCAN_TILE_PITCH = 15
SCAN_SLAB_ROWS = 480
SCAN_CHUNK = 128
MOE_TILE_ROWS = 512
VMEM_LIMIT = 56 * 1024 * 1024


def _params(*sem):
    return pltpu.CompilerParams(dimension_semantics=sem, vmem_limit_bytes=VMEM_LIMIT)


def _mm(a, w, precise=True):
    if w.dtype == BF16 or not precise:
        return jnp.dot(a.astype(BF16), w.astype(BF16), preferred_element_type=F32)
    return jnp.dot(a.astype(F32), w, preferred_element_type=F32, precision=HIGHEST)


def _rms(x, g):
    return x * lax.rsqrt(jnp.mean(x * x, axis=-1, keepdims=True) + RMS_EPS) * g


def _sigmoid(x):
    return 1.0 / (1.0 + jnp.exp(-x))


def _silu(x):
    return x * _sigmoid(x)


def _gelu_tanh(x):
    return 0.5 * x * (1.0 + jnp.tanh(math.sqrt(2.0 / math.pi) * (x + 0.044715 * (x * x * x))))


def _rows(tm, n):
    return pl.BlockSpec((None, tm, n), lambda b, i, *_: (b, i, 0))


def _mod(arr, tm):
    if arr.shape[1] == 1:
        return pl.BlockSpec((None, 1, arr.shape[2]), lambda b, i, *_: (b, 0, 0))
    return pl.BlockSpec((None, tm, arr.shape[2]), lambda b, i, *_: (b, i, 0))


def _const(arr):
    nd = arr.ndim
    return pl.BlockSpec(arr.shape, lambda *_: (0,) * nd)


def _ada_kernel(c_ref, w_ref, b_ref, o_ref):
    o_ref[...] = _mm(_silu(c_ref[...]), w_ref[...]) + b_ref[...]


def _ada_call(c_all, ada_w, ada_b):
    rows = c_all.shape[0]
    b4 = ada_b.reshape(DEPTH, 6, 1, D_MODEL)
    return pl.pallas_call(
        _ada_kernel,
        grid=(DEPTH, 6),
        in_specs=[pl.BlockSpec((rows, D_MODEL), lambda l, k: (0, 0)),
                  pl.BlockSpec((None, D_MODEL, D_MODEL), lambda l, k: (l, 0, k)),
                  pl.BlockSpec((None, None, 1, D_MODEL), lambda l, k: (l, k, 0, 0))],
        out_specs=pl.BlockSpec((None, None, rows, D_MODEL), lambda l, k: (l, k, 0, 0)),
        out_shape=jax.ShapeDtypeStruct((DEPTH, 6, rows, D_MODEL), F32),
        compiler_params=_params("arbitrary", "arbitrary"),
        name="ada_modulation",
    )(c_all, ada_w, b4)


def _inproj_kernel(x_ref, g_ref, sc_ref, sh_ref, w_ref, u_ref):
    h = _rms(x_ref[...], g_ref[...]) * (1.0 + sc_ref[...]) + sh_ref[...]
    u_ref[...] = _mm(h, w_ref[...])


def _inproj_call(x, g, sc, sh, w, tm):
    bm, r, d = x.shape
    return pl.pallas_call(
        _inproj_kernel,
        grid=(bm, r // tm),
        in_specs=[_rows(tm, d), _const(g), _mod(sc, tm), _mod(sh, tm), _const(w)],
        out_specs=_rows(tm, w.shape[1]),
        out_shape=jax.ShapeDtypeStruct((bm, r, w.shape[1]), F32),
        compiler_params=_params("arbitrary", "arbitrary"),
        name="s5_in_proj",
    )(x, g, sc, sh, w)


def _ssm_prompt_kernel(x_ref, g_ref, sc_ref, sh_ref, win_ref, wbr_ref, wbi_ref, ar_ref, ai_ref, c_ref, d_ref,
                       wglu_ref, wout_ref, gn_ref, gate_ref,
                       y_ref, fr_ref, fi_ref, u_scr, br_scr, bi_scr, sr_scr, si_scr, *, t_chunk, nseq):
    tb_n = t_chunk // SUBLANES
    tiles = SLAB_STATES // LANES

    def tile_rows(j):
        return pl.ds(SCAN_TILE_PITCH * j, SUBLANES, stride=SCAN_STEP_PITCH)

    @pl.when(pl.program_id(0) == 0)
    def _():
        sr_scr[...] = jnp.zeros_like(sr_scr)
        si_scr[...] = jnp.zeros_like(si_scr)

    for q in range(nseq):
        h = _rms(x_ref[q], g_ref[...]) * (1.0 + sc_ref[q]) + sh_ref[q]
        u_scr[q] = _mm(h, win_ref[...])
    ub = u_scr[...].reshape(nseq * t_chunk, D_MODEL).astype(BF16)
    for s in range(SLABS):
        us = ub[:, s * LANES:(s + 1) * LANES]
        bre = jnp.dot(us, wbr_ref[s], preferred_element_type=F32)
        bim = jnp.dot(us, wbi_ref[s], preferred_element_type=F32)
        for jj in range(tiles):
            cols = slice(jj * LANES, (jj + 1) * LANES)
            for q in range(nseq):
                for tb in range(tb_n):
                    rows = slice((q * tb_n + tb) * SUBLANES, (q * tb_n + tb + 1) * SUBLANES)
                    br_scr[q, tb, tile_rows(s * tiles + jj), :] = bre[rows, cols]
                    bi_scr[q, tb, tile_rows(s * tiles + jj), :] = bim[rows, cols]

    ar = ar_ref[...]
    ai = ai_ref[...]

    def body(tb, carry):
        carry = list(carry)
        for r in range(SUBLANES):
            step = pl.ds(SCAN_STEP_PITCH * r, STATE_TILES, stride=SCAN_TILE_PITCH)
            for q in range(nseq):
                sr, si = carry[q]
                nsr = ar * sr - ai * si + br_scr[q, tb, step, :]
                nsi = ar * si + ai * sr + bi_scr[q, tb, step, :]
                carry[q] = (nsr, nsi)
                br_scr[q, tb, step, :] = nsr
                bi_scr[q, tb, step, :] = nsi
        return tuple(carry)

    final = lax.fori_loop(0, tb_n, body, tuple((sr_scr[q], si_scr[q]) for q in range(nseq)))
    for q in range(nseq):
        sr_scr[q], si_scr[q] = final[q]
        fr_ref[q], fi_ref[q] = final[q]

    def states(scr, j):
        return jnp.concatenate([scr[q, tb, tile_rows(j), :] for q in range(nseq) for tb in range(tb_n)],
                               axis=0).astype(BF16)

    u = u_scr[...].reshape(nseq * t_chunk, D_MODEL)
    z_slabs = []
    for s in range(SLABS):
        lhs = jnp.concatenate([states(scr, s * tiles + jj) for scr in (br_scr, bi_scr) for jj in range(tiles)],
                              axis=1)
        y = jnp.dot(lhs, c_ref[s], preferred_element_type=F32)
        sl = slice(s * LANES, (s + 1) * LANES)
        z_slabs.append(_gelu_tanh(y + d_ref[:, sl] * u[:, sl]))
    z = jnp.concatenate(z_slabs, axis=1)
    out = _mm(z * _sigmoid(_mm(z, wglu_ref[...])), wout_ref[...])
    for q in range(nseq):
        rows = slice(q * t_chunk, (q + 1) * t_chunk)
        y_ref[q] = x_ref[q] + gate_ref[q] * _rms(out[rows], gn_ref[...])


def _ssm_prompt_call(x, g, sc, sh, win, wbr, wbi, ar, ai, c, d, wglu, wout, gn, gate, t_chunk):
    b, l, dm = x.shape
    kern = functools.partial(_ssm_prompt_kernel, t_chunk=t_chunk, nseq=b)
    seq_spec = pl.BlockSpec((b, t_chunk, dm), lambda c: (0, c, 0))
    state_spec = pl.BlockSpec((b, STATE_TILES, LANES), lambda c: (0, 0, 0))
    scr = pltpu.VMEM((b, t_chunk // SUBLANES, SCAN_SLAB_ROWS, LANES), F32)
    carry = pltpu.VMEM((b, STATE_TILES, LANES), F32)
    consts = (g, sc, sh, win, wbr, wbi, ar, ai, c, d, wglu, wout, gn, gate)
    return pl.pallas_call(
        kern,
        grid=(l // t_chunk,),
        in_specs=[seq_spec] + [_const(a) for a in consts],
        out_specs=[seq_spec, state_spec, state_spec],
        out_shape=[jax.ShapeDtypeStruct((b, l, dm), F32),
                   jax.ShapeDtypeStruct((b, STATE_TILES, LANES), F32),
                   jax.ShapeDtypeStruct((b, STATE_TILES, LANES), F32)],
        scratch_shapes=[pltpu.VMEM((b, t_chunk, dm), F32), scr, scr, carry, carry],
        compiler_params=_params("arbitrary"),
        name="s5_mixer_prompt",
    )(x, *consts)


def _ssm_sample_kernel(u_ref, x0r_ref, x0i_ref, wbr_ref, wbi_ref, ar_ref, ai_ref, cr_ref, ci_ref,
                       d_ref, z_ref, fr_ref, fi_ref):
    for s in range(SLABS):
        sl = slice(s * LANES, (s + 1) * LANES)
        st = slice(s * SLAB_STATES, (s + 1) * SLAB_STATES)
        us = u_ref[:, sl]
        a_r, a_i = ar_ref[:, st], ai_ref[:, st]
        x0r, x0i = x0r_ref[:, st], x0i_ref[:, st]
        sr = _mm(us, wbr_ref[s]) + a_r * x0r - a_i * x0i
        si = _mm(us, wbi_ref[s]) + a_r * x0i + a_i * x0r
        fr_ref[:, st] = sr
        fi_ref[:, st] = si
        y = _mm(sr, cr_ref[s]) - _mm(si, ci_ref[s])
        z_ref[:, sl] = _gelu_tanh(y + d_ref[:, sl] * us)


def _ssm_sample_call(u, x0r, x0i, wbr, wbi, ar, ai, cr, ci, d):
    n = u.shape[0]
    args = (u, x0r, x0i, wbr, wbi, ar, ai, cr, ci, d)
    return pl.pallas_call(
        _ssm_sample_kernel,
        grid=(1,),
        in_specs=[_const(a) for a in args],
        out_specs=[pl.BlockSpec((n, D_MODEL), lambda i: (0, 0)),
                   pl.BlockSpec((n, SSM_FLAT), lambda i: (0, 0)),
                   pl.BlockSpec((n, SSM_FLAT), lambda i: (0, 0))],
        out_shape=[jax.ShapeDtypeStruct((n, D_MODEL), F32),
                   jax.ShapeDtypeStruct((n, SSM_FLAT), F32),
                   jax.ShapeDtypeStruct((n, SSM_FLAT), F32)],
        compiler_params=_params("arbitrary"),
        name="s5_step_sample",
    )(*args)


def _s5out_kernel(z_ref, x_ref, wg_ref, wo_ref, gn_ref, gate_ref, o_ref):
    z = z_ref[...]
    gl = z * _sigmoid(_mm(z, wg_ref[...]))
    out = _mm(gl, wo_ref[...])
    o_ref[...] = x_ref[...] + gate_ref[...] * _rms(out, gn_ref[...])


def _s5out_call(z, x, wg, wo, gn, gate, tm):
    bm, r, d = x.shape
    return pl.pallas_call(
        _s5out_kernel,
        grid=(bm, r // tm),
        in_specs=[_rows(tm, d), _rows(tm, d), _const(wg), _const(wo), _const(gn), _mod(gate, tm)],
        out_specs=_rows(tm, d),
        out_shape=jax.ShapeDtypeStruct((bm, r, d), F32),
        compiler_params=_params("arbitrary", "arbitrary"),
        name="s5_glu_out_proj",
    )(z, x, wg, wo, gn, gate)


def _ffn_kernel(x_ref, g_ref, sc_ref, sh_ref, wg_ref, wu_ref, wd_ref, gn_ref, gate_ref, o_ref,
                h_scr, acc_scr):
    f = pl.program_id(2)

    @pl.when(f == 0)
    def _():
        h = _rms(x_ref[...], g_ref[...]) * (1.0 + sc_ref[...]) + sh_ref[...]
        h_scr[...] = h.astype(h_scr.dtype)
        acc_scr[...] = jnp.zeros_like(acc_scr)

    h = h_scr[...]
    he = _silu(_mm(h, wg_ref[...])) * _mm(h, wu_ref[...])
    acc_scr[...] += _mm(he, wd_ref[...])

    @pl.when(f == pl.num_programs(2) - 1)
    def _():
        o_ref[...] = x_ref[...] + gate_ref[...] * _rms(acc_scr[...], gn_ref[...])


def _ffn_call(x, g, sc, sh, wg, wu, wd, gn, gate, tm, tf):
    bm, r, d = x.shape
    ff = wg.shape[1]
    return pl.pallas_call(
        _ffn_kernel,
        grid=(bm, r // tm, ff // tf),
        in_specs=[_rows(tm, d), _const(g), _mod(sc, tm), _mod(sh, tm),
                  pl.BlockSpec((d, tf), lambda b, i, f: (0, f)),
                  pl.BlockSpec((d, tf), lambda b, i, f: (0, f)),
                  pl.BlockSpec((tf, d), lambda b, i, f: (f, 0)),
                  _const(gn), _mod(gate, tm)],
        out_specs=_rows(tm, d),
        out_shape=jax.ShapeDtypeStruct((bm, r, d), F32),
        scratch_shapes=[pltpu.VMEM((tm, d), wg.dtype), pltpu.VMEM((tm, d), F32)],
        compiler_params=_params("arbitrary", "arbitrary", "arbitrary"),
        name="dense_swiglu",
    )(x, g, sc, sh, wg, wu, wd, gn, gate)


def _rope(t, cos, sin_lo, sin_hi):
    half = ROT_DIM // 2
    outs = []
    for s in range(t.shape[1] // LANES):
        ts = t[:, s * LANES:(s + 1) * LANES]
        outs.append(ts * cos + pltpu.roll(ts, LANES - half, 1) * sin_lo + pltpu.roll(ts, half, 1) * sin_hi)
    return jnp.concatenate(outs, axis=1)


def _qkv_values(x_ref, gkv_ref, g_ref, sc_ref, sh_ref, wkv_ref, wq_ref, cos_ref, slo_ref, shi_ref):
    x = x_ref[...]
    xn = x * lax.rsqrt(jnp.mean(x * x, axis=-1, keepdims=True) + RMS_EPS)
    kv = _mm(xn * gkv_ref[...], wkv_ref[...])
    q = _mm((xn * g_ref[...]) * (1.0 + sc_ref[...]) + sh_ref[...], wq_ref[...])
    cos, slo, shi = cos_ref[...], slo_ref[...], shi_ref[...]
    k = _rope(kv[:, :KV_DIM], cos, slo, shi)
    v = kv[:, KV_DIM:]
    q = _rope(q, cos, slo, shi) * (HEAD_DIM ** -0.5)
    return k, v, q


def _qkv_sample_kernel(*refs):
    k_ref, v_ref, q_ref = refs[10:]
    k, v, q = _qkv_values(*refs[:10])
    k_ref[...] = k
    v_ref[...] = v
    q_ref[...] = q


RESIDUE_STEP = 4


def _split_rows(val, scr, step):
    rows, slabs = val.shape[0], val.shape[1] // LANES
    for s in range(slabs):
        scr[s, 0:rows, :] = val[:, s * LANES:(s + 1) * LANES]
    n = rows // step
    return [jnp.concatenate([scr[s, pl.ds(r, n, stride=step), :] for s in range(slabs)], axis=1)
            for r in range(step)]


def _merge_rows(parts, scr):
    step, n, slabs = len(parts), parts[0].shape[0], parts[0].shape[1] // LANES
    for r in range(step):
        for s in range(slabs):
            scr[s, pl.ds(r, n, stride=step), :] = parts[r][:, s * LANES:(s + 1) * LANES]
    return jnp.concatenate([scr[s, 0:step * n, :] for s in range(slabs)], axis=1)


def _split_steps(dil):
    steps, done = [], 1
    while done < dil:
        steps.append(min(RESIDUE_STEP, dil // done))
        done *= steps[-1]
    return steps


def _residue_classes(val, scrs, dil):
    classes, done = [val], 1
    for step in _split_steps(dil):
        nxt = [None] * (done * step)
        for r, arr in enumerate(classes):
            for m, part in enumerate(_split_rows(arr, scrs[r % len(scrs)], step)):
                nxt[r + done * m] = part
        classes, done = nxt, done * step
    return classes


def _interleave_classes(classes, scrs):
    done = len(classes)
    for step in reversed(_split_steps(len(classes))):
        done //= step
        classes = [_merge_rows([classes[r + done * m] for m in range(step)], scrs[r % len(scrs)])
                   for r in range(done)]
    return classes[0]


def _store_residues(dst_ref, val, scrs, dil):
    for r, part in enumerate(_residue_classes(val, scrs, dil)):
        dst_ref[r] = part.astype(dst_ref.dtype)


def _dilation_classes(val, scrs):
    out, prev_dil, prev = {}, 1, [val]
    for _, dil in DILATION_GROUPS:
        nxt = [None] * dil
        for r, arr in enumerate(prev):
            for m, part in enumerate(_residue_classes(arr, scrs, dil // prev_dil)):
                nxt[r + prev_dil * m] = part
        out[dil], prev_dil, prev = nxt, dil, nxt
    return out


def _qkv_prompt_kernel(*refs):
    k_ref, v_ref = refs[10:12]
    kd_refs = refs[12:12 + N_DIL_GROUPS]
    vd_refs = refs[12 + N_DIL_GROUPS:12 + 2 * N_DIL_GROUPS]
    qd_refs = refs[12 + 2 * N_DIL_GROUPS:12 + 3 * N_DIL_GROUPS]
    scrs = refs[12 + 3 * N_DIL_GROUPS:]
    k, v, q = _qkv_values(*refs[:10])
    k_ref[...] = k
    v_ref[...] = v
    for val, dst_refs in ((k, kd_refs), (v, vd_refs)):
        classes = _dilation_classes(val, scrs)
        for g, (_, dil) in enumerate(DILATION_GROUPS):
            for r, part in enumerate(classes[dil]):
                dst_refs[g][r] = part.astype(dst_refs[g].dtype)
    for g, (_, dil) in enumerate(DILATION_GROUPS):
        _store_residues(qd_refs[g], q[:, g * KV_DIM:(g + 1) * KV_DIM], scrs, dil)


def _qkv_call(x, gkv, g, sc, sh, wkv, wq, cos, slo, shi, tm, prompt):
    bm, r, d = x.shape
    nq = wq.shape[1]
    tab = pl.BlockSpec((tm, LANES), lambda b, i: (i, 0))
    in_specs = [_rows(tm, d), _const(gkv), _const(g), _mod(sc, tm), _mod(sh, tm),
                _const(wkv), _const(wq), tab, tab, tab]
    out_specs = [_rows(tm, KV_DIM), _rows(tm, KV_DIM)]
    out_shape = [jax.ShapeDtypeStruct((bm, r, KV_DIM), F32), jax.ShapeDtypeStruct((bm, r, KV_DIM), F32)]
    if not prompt:
        return pl.pallas_call(
            _qkv_sample_kernel,
            grid=(bm, r // tm),
            in_specs=in_specs,
            out_specs=out_specs + [_rows(tm, nq)],
            out_shape=out_shape + [jax.ShapeDtypeStruct((bm, r, nq), F32)],
            compiler_params=_params("arbitrary", "arbitrary"),
            name="qkv_proj_rope_sample",
        )(x, gkv, g, sc, sh, wkv, wq, cos, slo, shi)
    for _ in range(3):
        for _, dil in DILATION_GROUPS:
            out_specs.append(pl.BlockSpec((None, dil, tm // dil, KV_DIM), lambda b, i: (b, 0, i, 0)))
            out_shape.append(jax.ShapeDtypeStruct((bm, dil, r // dil, KV_DIM), BF16))
    outs = pl.pallas_call(
        _qkv_prompt_kernel,
        grid=(bm, r // tm),
        in_specs=in_specs,
        out_specs=out_specs,
        out_shape=out_shape,
        scratch_shapes=[pltpu.VMEM((KV_DIM // LANES, tm, LANES), F32)] * 2,
        compiler_params=_params("arbitrary", "arbitrary"),
        name="qkv_proj_rope_prompt",
    )(x, gkv, g, sc, sh, wkv, wq, cos, slo, shi)
    n = N_DIL_GROUPS
    return outs[0], outs[1], outs[2:2 + n], outs[2 + n:2 + 2 * n], outs[2 + 2 * n:2 + 3 * n]


def _band_kernel(q_ref, kp_ref, kc_ref, vp_ref, vc_ref, o_ref, l_ref, *, nb):
    qi = lax.broadcasted_iota(jnp.int32, (SPAN, 2 * SPAN), 0)
    kj = lax.broadcasted_iota(jnp.int32, (SPAN, 2 * SPAN), 1)
    band = (kj >= qi) & (kj <= qi + SPAN)
    band_first = band & (kj >= jnp.where(pl.program_id(2) == 0, SPAN, 0))
    lane = lax.broadcasted_iota(jnp.int32, (SPAN, LANES), 1)
    head0 = lane < HEAD_DIM
    for i in range(nb):
        cur = slice(i * SPAN, (i + 1) * SPAN)
        prev = slice((i - 1) * SPAN, i * SPAN)
        for hp in range(KV_DIM // LANES):
            cols = slice(hp * LANES, (hp + 1) * LANES)
            q2 = q_ref[cur, cols]
            if i == 0:
                k2 = jnp.concatenate([kp_ref[:, cols], kc_ref[cur, cols]], axis=0)
                v2 = jnp.concatenate([vp_ref[:, cols], vc_ref[cur, cols]], axis=0)
                mask = band_first
            else:
                k2 = jnp.concatenate([kc_ref[prev, cols], kc_ref[cur, cols]], axis=0)
                v2 = jnp.concatenate([vc_ref[prev, cols], vc_ref[cur, cols]], axis=0)
                mask = band
            o_pair = None
            l_pair = None
            for hh in range(2):
                sel = head0 if hh == 0 else jnp.logical_not(head0)
                qm = jnp.where(sel, q2, jnp.zeros_like(q2))
                s = lax.dot_general(qm, k2, (((1,), (1,)), ((), ())), preferred_element_type=F32)
                s = jnp.where(mask, s, NEG_INF)
                m = jnp.max(s, axis=-1, keepdims=True)
                p = jnp.exp(s - m)
                den = jnp.sum(p, axis=-1, keepdims=True)
                o = jnp.dot(p.astype(BF16), v2, preferred_element_type=F32) / den
                lse = jnp.broadcast_to(m + jnp.log(den), (SPAN, LANES))
                o_pair = o if o_pair is None else jnp.where(head0, o_pair, o)
                l_pair = lse if l_pair is None else jnp.where(head0, l_pair, lse)
            o_ref[cur, cols] = o_pair.astype(o_ref.dtype)
            l_ref[cur, cols] = l_pair


def _band_call(q, k, v, nb):
    b, dil, m, _ = k.shape
    tq = nb * SPAN
    cur = pl.BlockSpec((None, None, tq, KV_DIM), lambda bb, r, n: (bb, r, n, 0))
    prev = pl.BlockSpec((None, None, SPAN, KV_DIM), lambda bb, r, n: (bb, r, jnp.maximum(n * nb - 1, 0), 0))
    return pl.pallas_call(
        functools.partial(_band_kernel, nb=nb),
        grid=(b, dil, m // tq),
        in_specs=[cur, prev, cur, prev, cur],
        out_specs=[cur, cur],
        out_shape=[jax.ShapeDtypeStruct((b, dil, m, KV_DIM), BF16),
                   jax.ShapeDtypeStruct((b, dil, m, KV_DIM), F32)],
        compiler_params=_params("arbitrary", "arbitrary", "arbitrary"),
        name=f"band_attention_dil{dil}",
    )(q, k, k, v, v)


def _transpose_small(x):
    c = x.shape[1]
    eye = jnp.where(lax.broadcasted_iota(jnp.int32, (c, c), 0) == lax.broadcasted_iota(jnp.int32, (c, c), 1),
                    1.0, 0.0).astype(F32)
    return lax.dot_general(eye, x, (((1,), (1,)), ((), ())), preferred_element_type=F32, precision=HIGHEST)


def _sample_attn_kernel(q_ref, kn_ref, vn_ref, kc_ref, vc_ref, o_ref, ko_ref, vo_ref):
    cl = kc_ref.shape[-1]
    is_last = lax.broadcasted_iota(jnp.int32, (HEAD_DIM, cl), 1) == cl - 1
    kn_rows, vn_rows = kn_ref[0], vn_ref[0]
    kn_cols, vn_cols = _transpose_small(kn_rows), _transpose_small(vn_rows)
    for h in range(N_KV_HEADS):
        ko_ref[0, h] = jnp.where(is_last, kn_cols[:, h:h + 1], pltpu.roll(kc_ref[0, h], cl - 1, 1))
        vo_ref[0, h] = jnp.where(is_last, vn_cols[:, h:h + 1], pltpu.roll(vc_ref[0, h], cl - 1, 1))
    outs, lses = [], []
    for g, (win, dil) in enumerate(DILATION_GROUPS):
        q_rows = q_ref[0, g]
        q_cols = _transpose_small(q_rows)
        s = jnp.concatenate(
            [jnp.sum(kc_ref[0, h, :, cl - win:] * q_cols[:, h:h + 1], axis=0, keepdims=True)
             for h in range(N_KV_HEADS)], axis=0)
        if dil > 1:
            lane = lax.broadcasted_iota(jnp.int32, s.shape, 1)
            s = jnp.where((lane & (dil - 1)) == 0, s, NEG_INF)
        s_new = jnp.sum(kn_rows * q_rows, axis=-1, keepdims=True)
        m = jnp.maximum(jnp.max(s, axis=-1, keepdims=True), s_new)
        p = jnp.exp(s - m)
        p_new = jnp.exp(s_new - m)
        den = jnp.sum(p, axis=-1, keepdims=True) + p_new
        pn = p / den
        o_cols = jnp.concatenate(
            [jnp.sum(vc_ref[0, h, :, cl - win:] * pn[h:h + 1, :], axis=-1, keepdims=True)
             for h in range(N_KV_HEADS)], axis=1)
        outs.append(_transpose_small(o_cols) + (p_new / den) * vn_rows)
        lses.append(m + jnp.log(den))
    mx = jnp.maximum(jnp.maximum(lses[0], lses[1]), lses[2])
    es = [jnp.exp(l - mx) for l in lses]
    o_ref[0] = (es[0] * outs[0] + es[1] * outs[1] + es[2] * outs[2]) / (es[0] + es[1] + es[2])


def _sample_attn_call(q, k_new, v_new, cache_k, cache_v):
    n, nh, hd, cl = cache_k.shape
    col = pl.BlockSpec((1, nh, hd), lambda i: (i, 0, 0))
    cache = pl.BlockSpec((1, nh, hd, cl), lambda i: (i, 0, 0, 0))
    return pl.pallas_call(
        _sample_attn_kernel,
        grid=(n,),
        in_specs=[pl.BlockSpec((1, N_DIL_GROUPS, nh, hd), lambda i: (i, 0, 0, 0)), col, col,
                  cache, cache],
        out_specs=[col, cache, cache],
        out_shape=[jax.ShapeDtypeStruct((n, nh, hd), F32),
                   jax.ShapeDtypeStruct(cache_k.shape, cache_k.dtype),
                   jax.ShapeDtypeStruct(cache_v.shape, cache_v.dtype)],
        compiler_params=_params("arbitrary"),
        name="window_attention_cache_append",
    )(q, k_new, v_new, cache_k, cache_v)


def _attn_out_kernel(o1_ref, o2_ref, o3_ref, l1_ref, l2_ref, l3_ref, x_ref, wo_ref, gn_ref, gate_ref,
                     g2_ref, sc2_ref, sh2_ref, wr_ref, br_ref,
                     y_ref, h_ref, route_ref, cnt_ref, carry_scr, *scrs, stride):
    vals = []
    for i, (ref, (_, dil)) in enumerate(zip((o1_ref, o2_ref, o3_ref, l1_ref, l2_ref, l3_ref),
                                            DILATION_GROUPS + DILATION_GROUPS)):
        classes = [ref[r].astype(F32) for r in range(dil)]
        vals.append(_interleave_classes(classes, scrs[2 * (i % 2):2 * (i % 2) + 2]))
    o1, o2, o3, l1, l2, l3 = vals
    mx = jnp.maximum(jnp.maximum(l1, l2), l3)
    e1, e2, e3 = jnp.exp(l1 - mx), jnp.exp(l2 - mx), jnp.exp(l3 - mx)
    o = (e1 * o1 + e2 * o2 + e3 * o3) / (e1 + e2 + e3)
    out = _mm(o, wo_ref[...])
    y = x_ref[...] + gate_ref[...] * _rms(out, gn_ref[...])
    y_ref[...] = y
    _route_tile(y, g2_ref, sc2_ref, sh2_ref, wr_ref, br_ref, h_ref, route_ref, cnt_ref, carry_scr, stride)


def _attn_out_call(os_, ls_, x, wo, gn, gate, g2, sc2, sh2, wr, br, tm):
    bm, r, d = x.shape
    nt = r // tm
    res = [pl.BlockSpec((None, dil, tm // dil, KV_DIM), lambda b, i: (b, 0, i, 0))
           for _, dil in DILATION_GROUPS]
    return pl.pallas_call(
        functools.partial(_attn_out_kernel, stride=_expert_stride(bm * r)),
        grid=(bm, nt),
        in_specs=res + res + [_rows(tm, d), _const(wo), _const(gn), _mod(gate, tm),
                              _const(g2), _mod(sc2, tm), _mod(sh2, tm), _const(wr), _const(br)],
        out_specs=[_rows(tm, d),
                   pl.BlockSpec((tm * SUBLANES, LANES), lambda b, i: (b * nt + i, 0)),
                   _rows(tm, LANES),
                   pl.BlockSpec((SUBLANES, LANES), lambda b, i: (0, 0))],
        out_shape=[jax.ShapeDtypeStruct((bm, r, d), F32),
                   jax.ShapeDtypeStruct((bm * r * SUBLANES, LANES), F32),
                   jax.ShapeDtypeStruct((bm, r, LANES), F32),
                   jax.ShapeDtypeStruct((SUBLANES, LANES), F32)],
        scratch_shapes=[pltpu.VMEM((SUBLANES, LANES), F32)]
        + [pltpu.VMEM((KV_DIM // LANES, tm, LANES), F32)] * 4,
        compiler_params=_params("arbitrary", "arbitrary"),
        name="attn_merge_out_proj_route",
    )(*os_, *ls_, x, wo, gn, gate, g2, sc2, sh2, wr, br)


def _proj_res_kernel(o_ref, x_ref, wo_ref, gn_ref, gate_ref, y_ref):
    out = _mm(o_ref[...], wo_ref[...])
    y_ref[...] = x_ref[...] + gate_ref[...] * _rms(out, gn_ref[...])


def _proj_res_call(o, x, wo, gn, gate, tm):
    bm, r, d = x.shape
    return pl.pallas_call(
        _proj_res_kernel,
        grid=(bm, r // tm),
        in_specs=[_rows(tm, o.shape[2]), _rows(tm, d), _const(wo), _const(gn), _mod(gate, tm)],
        out_specs=_rows(tm, d),
        out_shape=jax.ShapeDtypeStruct((bm, r, d), F32),
        compiler_params=_params("arbitrary", "arbitrary"),
        name="attn_out_proj",
    )(o, x, wo, gn, gate)


def _top2(logits):
    lane = lax.broadcasted_iota(jnp.int32, logits.shape, 1).astype(F32)
    m1 = jnp.max(logits, axis=-1, keepdims=True)
    i1 = jnp.min(jnp.where(logits == m1, lane, float(LANES)), axis=-1, keepdims=True)
    rest = jnp.where(lane == i1, -jnp.inf, logits)
    m2 = jnp.max(rest, axis=-1, keepdims=True)
    i2 = jnp.min(jnp.where(rest == m2, lane, float(LANES)), axis=-1, keepdims=True)
    e = jnp.exp(m2 - m1)
    return lane, i1, i2, 1.0 / (1.0 + e), e / (1.0 + e)


def _expert_stride(m):
    return m + MOE_TILE_ROWS


def _route_tile(x, g_ref, sc_ref, sh_ref, wr_ref, br_ref, h_ref, route_ref, cnt_ref, carry_scr, stride):
    @pl.when((pl.program_id(0) == 0) & (pl.program_id(1) == 0))
    def _():
        carry_scr[...] = jnp.zeros_like(carry_scr)

    h = _rms(x, g_ref[...]) * (1.0 + sc_ref[...]) + sh_ref[...]
    tm = h.shape[0]
    for c in range(SLABS):
        h_ref[pl.ds(c, tm, stride=SUBLANES), :] = h[:, c * LANES:(c + 1) * LANES]
    logits = _mm(h, wr_ref[...]) + br_ref[...]
    lane, i1, i2, g1, g2 = _top2(logits)
    onehot = jnp.where(lane == i1, 1.0, 0.0) + jnp.where(lane == i2, 1.0, 0.0)
    row = lax.broadcasted_iota(jnp.int32, (tm, tm), 0)
    col = lax.broadcasted_iota(jnp.int32, (tm, tm), 1)
    earlier = jnp.where(col < row, 1.0, 0.0).astype(BF16)
    before = jnp.dot(earlier, onehot.astype(BF16), preferred_element_type=F32) + carry_scr[0:1, :]
    r1 = jnp.sum(jnp.where(lane == i1, before, 0.0), axis=-1, keepdims=True)
    r2 = jnp.sum(jnp.where(lane == i2, before, 0.0), axis=-1, keepdims=True)
    p1 = i1 * float(stride) + r1
    p2 = i2 * float(stride) + r2
    route_ref[...] = (jnp.where(lane == 0.0, p1, 0.0) + jnp.where(lane == 1.0, p2, 0.0)
                      + jnp.where(lane == 2.0, g1, 0.0) + jnp.where(lane == 3.0, g2, 0.0))
    carry = carry_scr[...] + jnp.sum(onehot, axis=0, keepdims=True)
    carry_scr[...] = carry
    cnt_ref[...] = carry


def _token_tile(ref, idx):
    return ref.at[pl.ds(pl.multiple_of(idx * SUBLANES, SUBLANES), SUBLANES), :]


def _dispatch_kernel(cnt_ref, pos_ref, h_ref, xs_hbm, zero_scr, sem, zsem, *, tm, stride, tmx):
    i = pl.program_id(0)

    def issue(t, carry):
        src = _token_tile(h_ref, t)
        for k in range(2):
            pltpu.make_async_copy(src, _token_tile(xs_hbm, pos_ref[0, k * tm + t]), sem).start(priority=k)
        return carry

    lax.fori_loop(0, tm, issue, 0)
    for _ in range(2):
        pltpu.make_async_copy(h_ref, xs_hbm.at[pl.ds(0, tm * SUBLANES), :], sem).wait()

    @pl.when(i == pl.num_programs(0) - 1)
    def _():
        zero_scr[...] = jnp.zeros_like(zero_scr)
        copies = []
        for e in range(N_EXPERTS):
            start = pl.multiple_of((e * stride + cnt_ref[e]) * SUBLANES, SUBLANES)
            copies.append(pltpu.make_async_copy(zero_scr, xs_hbm.at[pl.ds(start, tmx * SUBLANES), :], zsem))
            copies[-1].start()
        for cp in copies:
            cp.wait()


def _dispatch_call(cnt, pos, h2d, tm, stride, tmx):
    n_tiles = pos.shape[0]
    rows = N_EXPERTS * stride * SUBLANES
    return pl.pallas_call(
        functools.partial(_dispatch_kernel, tm=tm, stride=stride, tmx=tmx),
        grid_spec=pltpu.PrefetchScalarGridSpec(
            num_scalar_prefetch=1,
            grid=(n_tiles,),
            in_specs=[pl.BlockSpec((None, 1, 2 * tm), lambda i, cnt: (i, 0, 0), memory_space=pltpu.SMEM),
                      pl.BlockSpec((tm * SUBLANES, LANES), lambda i, cnt: (i, 0))],
            out_specs=pl.BlockSpec(memory_space=pl.ANY),
            scratch_shapes=[pltpu.VMEM((tmx * SUBLANES, LANES), F32),
                            pltpu.SemaphoreType.DMA(()), pltpu.SemaphoreType.DMA(())],
        ),
        out_shape=jax.ShapeDtypeStruct((rows, LANES), F32),
        compiler_params=_params("arbitrary"),
        name="moe_dispatch",
    )(cnt, pos, h2d)


def _expert_kernel(blk_ref, exp_ref, nused_ref, xs_ref, wg_ref, wu_ref, wd_ref, o_ref, x_scr, acc_scr,
                   *, tmx):
    j = pl.program_id(0)
    f = pl.program_id(1)

    @pl.when(j < nused_ref[0])
    def _():
        @pl.when(f == 0)
        def _():
            for c in range(SLABS):
                x_scr[:, c * LANES:(c + 1) * LANES] = (
                    xs_ref[pl.ds(c, tmx, stride=SUBLANES), :].astype(x_scr.dtype))
            acc_scr[...] = jnp.zeros_like(acc_scr)

        x = x_scr[...]
        he = _silu(_mm(x, wg_ref[...])) * _mm(x, wu_ref[...])
        acc_scr[...] += _mm(he, wd_ref[...])

        @pl.when(f == pl.num_programs(1) - 1)
        def _():
            for c in range(SLABS):
                o_ref[pl.ds(c, tmx, stride=SUBLANES), :] = acc_scr[:, c * LANES:(c + 1) * LANES]


def _expert_call(blk, exp, nused, xs, wg, wu, wd, tmx, tf):
    ne, d, ff = wg.shape
    nf = ff // tf

    def fidx(j, f, nused):
        return jnp.where(j < nused[0], f, nf - 1)

    row_spec = pl.BlockSpec((tmx * SUBLANES, LANES), lambda j, f, blk, exp, nu: (blk[j], 0))
    return pl.pallas_call(
        functools.partial(_expert_kernel, tmx=tmx),
        grid_spec=pltpu.PrefetchScalarGridSpec(
            num_scalar_prefetch=3,
            grid=(blk.shape[0], nf),
            in_specs=[row_spec,
                      pl.BlockSpec((None, d, tf), lambda j, f, blk, exp, nu: (exp[j], 0, fidx(j, f, nu))),
                      pl.BlockSpec((None, d, tf), lambda j, f, blk, exp, nu: (exp[j], 0, fidx(j, f, nu))),
                      pl.BlockSpec((None, tf, d), lambda j, f, blk, exp, nu: (exp[j], fidx(j, f, nu), 0))],
            out_specs=row_spec,
            scratch_shapes=[pltpu.VMEM((tmx, d), wg.dtype), pltpu.VMEM((tmx, d), F32)],
        ),
        out_shape=jax.ShapeDtypeStruct(xs.shape, F32),
        compiler_params=_params("arbitrary", "arbitrary"),
        name="moe_grouped_experts",
    )(blk, exp, nused, xs, wg, wu, wd)


COMBINE_PARTS = 4


def _combine_kernel(pos_ref, route_ref, x_ref, gn_ref, gate_ref, out_hbm, y_ref, buf, sems, *, tm):
    pt = tm // COMBINE_PARTS

    def issue(part):
        def body(t, carry):
            for k in range(2):
                pltpu.make_async_copy(_token_tile(out_hbm, pos_ref[0, k * tm + t]),
                                      _token_tile(buf, k * tm + t), sems.at[part]).start(priority=k)
            return carry
        lax.fori_loop(part * pt, (part + 1) * pt, body, 0)

    for part in range(COMBINE_PARTS):
        issue(part)
    for part in range(COMBINE_PARTS):
        for _ in range(2):
            pltpu.make_async_copy(out_hbm.at[pl.ds(0, pt * SUBLANES), :], buf.at[pl.ds(0, pt * SUBLANES), :],
                                  sems.at[part]).wait()
        rows = slice(part * pt, (part + 1) * pt)
        route = route_ref[rows, :]
        g1, g2 = route[:, 2:3], route[:, 3:4]
        slabs = []
        for c in range(SLABS):
            slabs.append(g1 * buf[pl.ds(part * pt * SUBLANES + c, pt, stride=SUBLANES), :]
                         + g2 * buf[pl.ds((tm + part * pt) * SUBLANES + c, pt, stride=SUBLANES), :])
        y = jnp.concatenate(slabs, axis=1)
        gate = gate_ref[...] if gate_ref.shape[0] == 1 else gate_ref[rows, :]
        y_ref[rows, :] = x_ref[rows, :] + gate * _rms(y, gn_ref[...])


def _combine_call(pos, route, x, gn, gate, out2d, tm):
    bm, r, d = x.shape
    nt = r // tm
    return pl.pallas_call(
        functools.partial(_combine_kernel, tm=tm),
        grid=(bm, nt),
        in_specs=[pl.BlockSpec((None, 1, 2 * tm), lambda b, i: (b * nt + i, 0, 0), memory_space=pltpu.SMEM),
                  _rows(tm, LANES), _rows(tm, d), _const(gn), _mod(gate, tm),
                  pl.BlockSpec(memory_space=pl.ANY)],
        out_specs=_rows(tm, d),
        out_shape=jax.ShapeDtypeStruct((bm, r, d), F32),
        scratch_shapes=[pltpu.VMEM((2 * tm * SUBLANES, LANES), F32),
                        pltpu.SemaphoreType.DMA((COMBINE_PARTS,))],
        compiler_params=_params("arbitrary", "arbitrary"),
        name="moe_combine",
    )(pos, route, x, gn, gate, out2d)


def _moe_sparse(x, h2d, route, cnt, wg, wu, wd, gn, gate, tm):
    bm, r, d = x.shape
    m = bm * r
    tmx = MOE_TILE_ROWS
    stride = _expert_stride(m)
    n_tiles = 2 * m // tmx + N_EXPERTS
    cnt = cnt[0, :N_EXPERTS].astype(jnp.int32)
    pos = route[..., :2].astype(jnp.int32).reshape(m // tm, tm, 2).transpose(0, 2, 1).reshape(m // tm, 1, 2 * tm)
    xs = _dispatch_call(cnt, pos, h2d, tm, stride, tmx)
    per = (cnt + tmx - 1) // tmx
    ends = jnp.cumsum(per)
    j = jnp.minimum(jnp.arange(n_tiles, dtype=jnp.int32), ends[-1] - 1)
    exp = jnp.sum((j[:, None] >= ends[None, :]).astype(jnp.int32), axis=1)
    blk = exp * (stride // tmx) + j - (ends - per)[exp]
    out2d = _expert_call(blk, exp, ends[-1:], xs, wg, wu, wd, tmx, wg.shape[2] // 2)
    return _combine_call(pos, route, x, gn, gate, out2d, tm)


def _router_kernel(x_ref, g_ref, sc_ref, sh_ref, wr_ref, br_ref, h_ref, comb_ref):
    h = _rms(x_ref[...], g_ref[...]) * (1.0 + sc_ref[...]) + sh_ref[...]
    h_ref[...] = h.astype(h_ref.dtype)
    logits = _mm(h, wr_ref[...]) + br_ref[...]
    lane, i1, i2, g1, g2 = _top2(logits)
    comb_ref[...] = jnp.where(lane == i1, g1, 0.0) + jnp.where(lane == i2, g2, 0.0)


def _router_call(x, g, sc, sh, wr, br, h_dtype, tm):
    bm, r, d = x.shape
    return pl.pallas_call(
        _router_kernel,
        grid=(bm, r // tm),
        in_specs=[_rows(tm, d), _const(g), _mod(sc, tm), _mod(sh, tm), _const(wr), _const(br)],
        out_specs=[_rows(tm, d), _rows(tm, LANES)],
        out_shape=[jax.ShapeDtypeStruct((bm, r, d), h_dtype),
                   jax.ShapeDtypeStruct((bm, r, LANES), F32)],
        compiler_params=_params("arbitrary", "arbitrary"),
        name="moe_router",
    )(x, g, sc, sh, wr, br)


def _moe_kernel(h_ref, comb_ref, x_ref, wg_ref, wu_ref, wd_ref, gn_ref, gate_ref,
                o_ref, wgb_ref, wub_ref, wdb_ref, acc_scr):
    e = pl.program_id(2)
    f = pl.program_id(3)

    @pl.when((e == 0) & (f == 0))
    def _():
        acc_scr[...] = jnp.zeros_like(acc_scr)

    comb = comb_ref[...]
    lane = lax.broadcasted_iota(jnp.int32, comb.shape, 1)
    cw = jnp.sum(jnp.where(lane == e, comb, 0.0), axis=-1, keepdims=True)
    wg, wu, wd = wg_ref[...].astype(BF16), wu_ref[...].astype(BF16), wd_ref[...].astype(BF16)
    wgb_ref[...] = wg
    wub_ref[...] = wu
    wdb_ref[...] = wd
    h = h_ref[...]
    he = _silu(_mm(h, wg)) * _mm(h, wu)
    acc_scr[...] += cw * _mm(he, wd)

    @pl.when((e == pl.num_programs(2) - 1) & (f == pl.num_programs(3) - 1))
    def _():
        o_ref[...] = x_ref[...] + gate_ref[...] * _rms(acc_scr[...], gn_ref[...])


def _moe_call(h, comb, x, wg, wu, wd, gn, gate, tm, tf):
    bm, r, d = x.shape
    ne, _, ff = wg.shape
    assert bm == 1 and r == tm
    up = pl.BlockSpec((None, d, tf), lambda b, i, e, f: (e, 0, f))
    down = pl.BlockSpec((None, tf, d), lambda b, i, e, f: (e, f, 0))
    return pl.pallas_call(
        _moe_kernel,
        grid=(bm, r // tm, ne, ff // tf),
        in_specs=[_rows(tm, d), _rows(tm, LANES), _rows(tm, d), up, up, down, _const(gn), _mod(gate, tm)],
        out_specs=[_rows(tm, d), up, up, down],
        out_shape=[jax.ShapeDtypeStruct((bm, r, d), F32), jax.ShapeDtypeStruct(wg.shape, BF16),
                   jax.ShapeDtypeStruct(wu.shape, BF16), jax.ShapeDtypeStruct(wd.shape, BF16)],
        scratch_shapes=[pltpu.VMEM((tm, d), F32)],
        compiler_params=_params("arbitrary", "arbitrary", "arbitrary", "arbitrary"),
        name="moe_experts",
    )(h, comb, x, wg, wu, wd, gn, gate)


def _ssm_params(lam_re, lam_im, log_dt, b_re, b_im, c_re, c_im, wdtype):
    dt = jnp.exp(log_dt)[:, None]
    mag = jnp.exp(lam_re * dt)
    a_re, a_im = mag * jnp.cos(lam_im * dt), mag * jnp.sin(lam_im * dt)
    den = lam_re * lam_re + lam_im * lam_im
    f_re = ((a_re - 1.0) * lam_re + a_im * lam_im) / den
    f_im = (a_im * lam_re - (a_re - 1.0) * lam_im) / den
    bb_re = f_re[..., None] * b_re - f_im[..., None] * b_im
    bb_im = f_re[..., None] * b_im + f_im[..., None] * b_re
    eye = jnp.eye(GROUPS_PER_SLAB, dtype=F32)

    def in_map(bb):
        t = bb.reshape(SLABS, GROUPS_PER_SLAB, SSM_STATE, SSM_CH)
        w = jnp.einsum('sgpc,gh->sgchp', t, eye)
        return w.reshape(SLABS, LANES, SLAB_STATES).astype(wdtype)

    def out_map(cc):
        t = cc.reshape(SLABS, GROUPS_PER_SLAB, SSM_CH, SSM_STATE)
        w = jnp.einsum('sgcp,gh->sgphc', t, eye)
        return w.reshape(SLABS, SLAB_STATES, LANES).astype(wdtype)

    return a_re, a_im, in_map(bb_re), in_map(bb_im), out_map(c_re), out_map(c_im)


def _rope_tables(pos):
    half = ROT_DIM // 2
    inv = ROPE_THETA ** (-jnp.arange(half, dtype=F32) / half)
    ang = pos.astype(F32)[:, None] * inv[None, :]
    cos, sin = jnp.cos(ang), jnp.sin(ang)
    n = pos.shape[0]
    pad = jnp.zeros((n, HEAD_DIM - ROT_DIM), F32)
    zero = jnp.zeros((n, half), F32)
    cos_h = jnp.concatenate([cos, cos, pad + 1.0], axis=1)
    lo_h = jnp.concatenate([-sin, zero, pad], axis=1)
    hi_h = jnp.concatenate([zero, sin, pad], axis=1)
    rep = LANES // HEAD_DIM
    return jnp.tile(cos_h, (1, rep)), jnp.tile(lo_h, (1, rep)), jnp.tile(hi_h, (1, rep))


def _trunk(x, mods, pos, wts, prompt, ssm0=None, cache=None):
    bm, r, d = x.shape
    tm = 512 if prompt else r
    wd = BF16 if prompt else F32
    cast = lambda w: w.astype(wd)
    norm_g = wts['norm_g'].reshape(DEPTH, 4, 1, d)

    sh1, sc1, g1, sh2, sc2, g2 = mods[0]
    a_re, a_im = wts['ssm_prepared'][:2]
    wbr, wbi, cr, ci = (cast(w) for w in wts['ssm_prepared'][2:])
    dvec = wts['ssm_d'][0].reshape(1, d)
    w_in, w_glu, w_out = cast(wts['ssm_w_in'][0]), cast(wts['ssm_w_glu'][0]), cast(wts['ssm_w_out'][0])
    if prompt:
        x, fr, fi = _ssm_prompt_call(x, norm_g[0, 0], sc1, sh1, w_in, wbr, wbi,
                                     a_re.reshape(STATE_TILES, LANES), a_im.reshape(STATE_TILES, LANES),
                                     jnp.concatenate([cr, -ci], axis=1), dvec, w_glu, w_out,
                                     norm_g[0, 1], g1, SCAN_CHUNK)
    else:
        u = _inproj_call(x, norm_g[0, 0], sc1, sh1, w_in, tm)
        z, fr, fi = _ssm_sample_call(u[0], ssm0[0].reshape(r, SSM_FLAT), ssm0[1].reshape(r, SSM_FLAT),
                                     wbr, wbi, a_re.reshape(1, SSM_FLAT), a_im.reshape(1, SSM_FLAT),
                                     cr, ci, dvec)
        x = _s5out_call(z[None], x, w_glu, w_out, norm_g[0, 1], g1, tm)
    fr = fr.reshape(-1, SSM_GROUPS, SSM_STATE)[None]
    fi = fi.reshape(-1, SSM_GROUPS, SSM_STATE)[None]
    tf = wts['ffn_w_gate'].shape[2] // 2
    x = _ffn_call(x, norm_g[0, 2], sc2, sh2, cast(wts['ffn_w_gate'][0]), cast(wts['ffn_w_up'][0]),
                  cast(wts['ffn_w_down'][0]), norm_g[0, 3], g2, tm, tf)

    sh1, sc1, g1, sh2, sc2, g2 = mods[1]
    cos, slo, shi = _rope_tables(pos)
    qkv = _qkv_call(x, wts['kv_norm_g'].reshape(1, d), norm_g[1, 0], sc1, sh1, cast(wts['w_kv']),
                    cast(wts['attn_w_q'][0]), cos, slo, shi, tm, prompt)
    wo = cast(wts['attn_w_o'][0])
    wr = cast(jnp.zeros((d, LANES), F32).at[:, :N_EXPERTS].set(wts['moe_w_router'][0]))
    br = jnp.full((1, LANES), NEG_INF, F32).at[0, :N_EXPERTS].set(wts['moe_b_router'][0])
    experts = None
    if prompt:
        wg, wu, wdn = wts['experts_bf16']
        k, v, kd, vd, qd = qkv
        os_, ls_ = [], []
        for g, (_, dil) in enumerate(DILATION_GROUPS):
            o, lse = _band_call(qd[g], kd[g], vd[g], min(8, r // dil // SPAN))
            os_.append(o)
            ls_.append(lse)
        x, h2d, route, cnt = _attn_out_call(os_, ls_, x, wo, norm_g[1, 1], g1,
                                            norm_g[1, 2], sc2, sh2, wr, br, tm)
        x = _moe_sparse(x, h2d, route, cnt, wg, wu, wdn, norm_g[1, 3], g2, tm)
    else:
        k, v, q = qkv
        o, k, v = _sample_attn_call(q[0].reshape(r, N_DIL_GROUPS, N_KV_HEADS, HEAD_DIM),
                                    k[0].reshape(r, N_KV_HEADS, HEAD_DIM),
                                    v[0].reshape(r, N_KV_HEADS, HEAD_DIM),
                                    cache[0].transpose(0, 2, 3, 1), cache[1].transpose(0, 2, 3, 1))
        k, v = k.transpose(0, 3, 1, 2), v.transpose(0, 3, 1, 2)
        x = _proj_res_call(o.reshape(1, r, KV_DIM), x, wo, norm_g[1, 1], g1, tm)
        h, comb = _router_call(x, norm_g[1, 2], sc2, sh2, wr, br, wd, tm)
        wg, wu, wdn = wts['moe_w_gate'][0], wts['moe_w_up'][0], wts['moe_w_down'][0]
        x, *experts = _moe_call(h, comb, x, wg, wu, wdn, norm_g[1, 3], g2, tm, wg.shape[2] // 4)
    return x, fr, fi, k, v, experts


def kernel(x_prompt, x_sample, state_ssm_re, state_ssm_im, cache_k, cache_v, c_prompt, c_sample, ada_w, ada_b, norm_g, ssm_w_in, ssm_lam_re, ssm_lam_im, ssm_log_dt, ssm_b_re, ssm_b_im, ssm_c_re, ssm_c_im, ssm_d, ssm_w_glu, ssm_w_out, kv_norm_g, w_kv, attn_w_q, attn_w_o, ffn_w_gate, ffn_w_up, ffn_w_down, moe_w_router, moe_b_router, moe_w_gate, moe_w_up, moe_w_down):
    wts = dict(norm_g=norm_g, ssm_w_in=ssm_w_in, ssm_lam_re=ssm_lam_re, ssm_lam_im=ssm_lam_im,
               ssm_log_dt=ssm_log_dt, ssm_b_re=ssm_b_re, ssm_b_im=ssm_b_im, ssm_c_re=ssm_c_re,
               ssm_c_im=ssm_c_im, ssm_d=ssm_d, ssm_w_glu=ssm_w_glu, ssm_w_out=ssm_w_out,
               kv_norm_g=kv_norm_g, w_kv=w_kv, attn_w_q=attn_w_q, attn_w_o=attn_w_o,
               ffn_w_gate=ffn_w_gate, ffn_w_up=ffn_w_up, ffn_w_down=ffn_w_down,
               moe_w_router=moe_w_router, moe_b_router=moe_b_router, moe_w_gate=moe_w_gate,
               moe_w_up=moe_w_up, moe_w_down=moe_w_down)
    nb, seq, d = x_prompt.shape
    ns = x_sample.shape[0]
    assert x_sample.shape[1] == 1 and ns % SUBLANES == 0

    pad = (-(ns + nb)) % SUBLANES
    c_all = jnp.concatenate([c_sample, c_prompt, jnp.zeros((pad, d), F32)], axis=0)
    mod_all = _ada_call(c_all, ada_w, ada_b)
    mods_s = [[mod_all[l, k, :ns][None] for k in range(6)] for l in range(DEPTH)]
    mods_p = [[mod_all[l, k, ns:ns + nb][:, None, :] for k in range(6)] for l in range(DEPTH)]

    pos_p = jnp.arange(seq, dtype=jnp.int32)
    pos_s = jnp.full((ns,), PAST_LEN, dtype=jnp.int32)
    wts['ssm_prepared'] = _ssm_params(ssm_lam_re[0], ssm_lam_im[0], ssm_log_dt[0], ssm_b_re[0], ssm_b_im[0],
                                      ssm_c_re[0], ssm_c_im[0], F32)
    xs = x_sample.reshape(1, ns, d)
    y_s, sr_s, si_s, k_s, v_s, wts['experts_bf16'] = _trunk(
        xs, mods_s, pos_s, wts, False, ssm0=(state_ssm_re[0], state_ssm_im[0]), cache=(cache_k, cache_v))
    y_p, sr_p, si_p, k_p, v_p, _ = _trunk(x_prompt, mods_p, pos_p, wts, True)

    keep = min(MAX_WINDOW, seq)
    k_rows_p = k_p[:, seq - keep:].reshape(nb, keep, N_KV_HEADS, HEAD_DIM)
    v_rows_p = v_p[:, seq - keep:].reshape(nb, keep, N_KV_HEADS, HEAD_DIM)
    return (y_p, y_s.reshape(ns, 1, d), sr_p, si_p, sr_s, si_s, k_rows_p, v_rows_p, k_s, v_s)
```
